```python
import jax, jax.numpy as jnp
from jax import lax
import numpy as np

D_MODEL = 1024
BATCH = 2
SEQ = 16384
DEPTH = 2
DEC_BATCH = 8
DEC_SEQ = 2048
PAST_LEN = 128

N_MIXERS = 2
N_ATT_LAYERS = (DEPTH + N_MIXERS - 1) // N_MIXERS
N_CONV_LAYERS = DEPTH // N_MIXERS
N_HEADS = 16
QK_NOPE_DIM = 64
QK_ROPE_DIM = 32
QK_HEAD_DIM = QK_NOPE_DIM + QK_ROPE_DIM
V_HEAD_DIM = 64
Q_LORA_RANK = 384
KV_LORA_RANK = 256
ROPE_THETA = 10000.0
Q_BLOCK = 128
CONV_WIDTH = 3
N_GROUPS = 4
EXPERTS_PER_GROUP = 8
N_EXPERTS = N_GROUPS * EXPERTS_PER_GROUP
TOP_K = 2
D_EXPERT = 512
EXPERT_BLOCK = 128
PLE_DIM = 256
RMS_EPS = 1e-6

kernel_name = "hybrid_mla_shortconv_hmoe_encoder"


def _rms(x, g):
    xf = x.astype(jnp.float32)
    y = xf * lax.rsqrt(jnp.mean(xf * xf, axis=-1, keepdims=True) + RMS_EPS)
    return (y * g.astype(jnp.float32)).astype(x.dtype)


def _rope_tables(S, dtype):
    pos = jnp.arange(S, dtype=jnp.float32)
    inv = ROPE_THETA ** (-jnp.arange(0, QK_ROPE_DIM, 2, dtype=jnp.float32) / QK_ROPE_DIM)
    ang = pos[:, None] * inv[None, :]
    return jnp.cos(ang).astype(dtype), jnp.sin(ang).astype(dtype)


def _rope_tail(x, cos, sin):
    half = QK_ROPE_DIM // 2
    x_nope = x[..., :QK_NOPE_DIM]
    x1 = x[..., QK_NOPE_DIM:QK_NOPE_DIM + half]
    x2 = x[..., QK_NOPE_DIM + half:]
    c = cos[None, :, None, :]
    s = sin[None, :, None, :]
    return jnp.concatenate([x_nope, x1 * c - x2 * s, x2 * c + x1 * s], axis=-1)


def _block_attention(q, k, v):
    B, S, H, Dq = q.shape
    nq = S // Q_BLOCK
    scale = 1.0 / float(np.sqrt(QK_HEAD_DIM))
    qb = q.reshape(B, nq, Q_BLOCK, H, Dq).transpose(1, 0, 2, 3, 4)

    def one_block(qi):
        s = jnp.einsum('bqhd,bkhd->bhqk', qi, k).astype(jnp.float32) * scale
        pr = jax.nn.softmax(s, axis=-1).astype(v.dtype)
        return jnp.einsum('bhqk,bkhd->bqhd', pr, v)

    o = lax.map(one_block, qb)
    return o.transpose(1, 0, 2, 3, 4).reshape(B, S, H, V_HEAD_DIM)


def _mla(xn, w_in, g_qa, g_kva, w_qb, w_kvb, g_q, g_k, w_o, cos, sin):
    B, S, _ = xn.shape
    a = xn @ w_in
    c_q = _rms(a[..., :Q_LORA_RANK], g_qa)
    c_kv = _rms(a[..., Q_LORA_RANK:Q_LORA_RANK + KV_LORA_RANK], g_kva)
    k_rope = a[..., Q_LORA_RANK + KV_LORA_RANK:]
    q = (c_q @ w_qb).reshape(B, S, N_HEADS, QK_HEAD_DIM)
    kv = (c_kv @ w_kvb).reshape(B, S, N_HEADS, QK_NOPE_DIM + V_HEAD_DIM)
    k_nope = kv[..., :QK_NOPE_DIM]
    v = kv[..., QK_NOPE_DIM:]
    k = jnp.concatenate(
        [k_nope, jnp.broadcast_to(k_rope[:, :, None, :], (B, S, N_HEADS, QK_ROPE_DIM))], axis=-1)
    q = _rope_tail(_rms(q, g_q), cos, sin)
    k = _rope_tail(_rms(k, g_k), cos, sin)
    o = _block_attention(q, k, v)
    return o.reshape(B, S, N_HEADS * V_HEAD_DIM) @ w_o


def _short_conv(xn, w_in, w_conv, w_out):
    bcu = xn @ w_in
    b = bcu[..., :D_MODEL]
    c = bcu[..., D_MODEL:2 * D_MODEL]
    u = bcu[..., 2 * D_MODEL:]
    y = lax.conv_general_dilated(
        c * u, w_conv[:, None, :], window_strides=(1,), padding=((1, 1),),
        dimension_numbers=('NWC', 'WIO', 'NWC'), feature_group_count=D_MODEL)
    return (b * y) @ w_out


def _hier_moe(xn, rg_w, rg_b, re_w, re_b, w_gate, w_up, w_down):
    lead = xn.shape[:-1]
    x = xn.reshape(-1, D_MODEL)
    N = x.shape[0]
    lg = (x @ rg_w).astype(jnp.float32) + rg_b.astype(jnp.float32)
    pg = jax.nn.softmax(lg, axis=-1)
    pg_top, g_idx = lax.top_k(pg, 1)
    le = ((x @ re_w).astype(jnp.float32) + re_b.astype(jnp.float32)).reshape(N, N_GROUPS, EXPERTS_PER_GROUP)
    sel = jnp.broadcast_to(g_idx[:, :, None], (N, 1, EXPERTS_PER_GROUP))
    le_sel = jnp.take_along_axis(le, sel, axis=1)[:, 0]
    pe = jax.nn.softmax(le_sel, axis=-1)
    pe_top, e_loc = lax.top_k(pe, TOP_K)
    gate = pg_top * pe_top / jnp.sum(pe_top, axis=-1, keepdims=True)
    eid = (g_idx * EXPERTS_PER_GROUP + e_loc).reshape(-1)
    tok = jnp.repeat(jnp.arange(N, dtype=jnp.int32), TOP_K)
    gw = gate.reshape(-1)
    order = jnp.argsort(eid)
    eid_s, tok_s, gw_s = eid[order], tok[order], gw[order]
    counts = jnp.bincount(eid, length=N_EXPERTS)
    start = jnp.cumsum(counts) - counts
    padded = ((counts + EXPERT_BLOCK - 1) // EXPERT_BLOCK) * EXPERT_BLOCK
    pend = jnp.cumsum(padded)
    pstart = pend - padded
    n_assign = N * TOP_K
    dest = pstart[eid_s] + (jnp.arange(n_assign, dtype=jnp.int32) - start[eid_s])
    P = ((n_assign + EXPERT_BLOCK - 1) // EXPERT_BLOCK) * EXPERT_BLOCK + N_EXPERTS * EXPERT_BLOCK
    n_blocks = P // EXPERT_BLOCK
    tok_pad = jnp.full((P,), N, dtype=jnp.int32).at[dest].set(tok_s)
    gw_pad = jnp.zeros((P,), jnp.float32).at[dest].set(gw_s)
    blk_e = jnp.minimum(
        jnp.searchsorted(pend, jnp.arange(n_blocks, dtype=pend.dtype) * EXPERT_BLOCK, side='right'),
        N_EXPERTS - 1)
    xpad = jnp.concatenate([x, jnp.zeros((1, D_MODEL), x.dtype)], axis=0)
    xb = xpad[tok_pad].reshape(n_blocks, EXPERT_BLOCK, D_MODEL)

    def expert_block(args):
        xe, e = args
        hmid = jax.nn.silu(xe @ w_gate[e]) * (xe @ w_up[e])
        return hmid @ w_down[e]

    yb = lax.map(expert_block, (xb, blk_e)).reshape(P, D_MODEL)
    y = jnp.zeros((N + 1, D_MODEL), x.dtype).at[tok_pad].add(yb * gw_pad[:, None].astype(x.dtype))[:N]
    return y.reshape(*lead, D_MODEL)


def _trunk(h, p, attn_norm, attn_w_in, attn_q_a_norm, attn_kv_a_norm, attn_w_qb, attn_w_kvb,
           attn_q_gain, attn_k_gain, attn_w_o, conv_norm, conv_w_in, conv_w, conv_w_out,
           moe_norm, router_group_w, router_group_b, router_expert_w, router_expert_b,
           expert_w_gate, expert_w_up, expert_w_down, ple_norm, ple_w_gate, ple_w_proj):
    S = h.shape[1]
    cos, sin = _rope_tables(S, h.dtype)
    for i in range(DEPTH):
        j = i // N_MIXERS
        if i % N_MIXERS == 0:
            h = h + _mla(_rms(h, attn_norm[j]), attn_w_in[j], attn_q_a_norm[j], attn_kv_a_norm[j],
                         attn_w_qb[j], attn_w_kvb[j], attn_q_gain[j], attn_k_gain[j], attn_w_o[j], cos, sin)
        else:
            h = h + _short_conv(_rms(h, conv_norm[j]), conv_w_in[j], conv_w[j], conv_w_out[j])
        h = h + _hier_moe(_rms(h, moe_norm[i]), router_group_w[i], router_group_b[i],
                          router_expert_w[i], router_expert_b[i],
                          expert_w_gate[i], expert_w_up[i], expert_w_down[i])
        gate = jax.nn.sigmoid(_rms(h, ple_norm[i]) @ ple_w_gate[i])
        h = h + gate * (p[i] @ ple_w_proj[i])
    return h


def setup_inputs(seed: int = 0) -> dict:
    key = jax.random.key(seed)
    ks = jax.random.split(key, 32)
    f32 = jnp.float32

    def nrm(k, shape, scale):
        return jax.random.normal(k, shape, f32) * scale

    def gain(k, shape):
        return jnp.ones(shape, f32) + 0.01 * jax.random.normal(k, shape, f32)

    NA, NC = N_ATT_LAYERS, N_CONV_LAYERS
    return {
        "x_prompt": nrm(ks[0], (BATCH, SEQ, D_MODEL), 1.0),
        "x_sample": nrm(ks[1], (DEC_BATCH, DEC_SEQ, D_MODEL), 1.0),
        "p_prompt": nrm(ks[2], (DEPTH, BATCH, SEQ, PLE_DIM), 1.0),
        "p_sample": nrm(ks[3], (DEPTH, DEC_BATCH, DEC_SEQ, PLE_DIM), 1.0),
        "attn_norm": gain(ks[4], (NA, D_MODEL)),
        "attn_w_in": nrm(ks[5], (NA, D_MODEL, Q_LORA_RANK + KV_LORA_RANK + QK_ROPE_DIM), D_MODEL ** -0.5),
        "attn_q_a_norm": gain(ks[6], (NA, Q_LORA_RANK)),
        "attn_kv_a_norm": gain(ks[7], (NA, KV_LORA_RANK)),
        "attn_w_qb": nrm(ks[8], (NA, Q_LORA_RANK, N_HEADS * QK_HEAD_DIM), Q_LORA_RANK ** -0.5),
        "attn_w_kvb": nrm(ks[9], (NA, KV_LORA_RANK, N_HEADS * (QK_NOPE_DIM + V_HEAD_DIM)), KV_LORA_RANK ** -0.5),
        "attn_q_gain": gain(ks[10], (NA, QK_HEAD_DIM)),
        "attn_k_gain": gain(ks[11], (NA, QK_HEAD_DIM)),
        "attn_w_o": nrm(ks[12], (NA, N_HEADS * V_HEAD_DIM, D_MODEL), (N_HEADS * V_HEAD_DIM) ** -0.5),
        "conv_norm": gain(ks[13], (NC, D_MODEL)),
        "conv_w_in": nrm(ks[14], (NC, D_MODEL, 3 * D_MODEL), D_MODEL ** -0.5),
        "conv_w": nrm(ks[15], (NC, CONV_WIDTH, D_MODEL), CONV_WIDTH ** -0.5),
        "conv_w_out": nrm(ks[16], (NC, D_MODEL, D_MODEL), D_MODEL ** -0.5),
        "moe_norm": gain(ks[17], (DEPTH, D_MODEL)),
        "router_group_w": nrm(ks[18], (DEPTH, D_MODEL, N_GROUPS), D_MODEL ** -0.5),
        "router_group_b": nrm(ks[19], (DEPTH, N_GROUPS), 0.01),
        "router_expert_w": nrm(ks[20], (DEPTH, D_MODEL, N_EXPERTS), D_MODEL ** -0.5),
        "router_expert_b": nrm(ks[21], (DEPTH, N_EXPERTS), 0.01),
        "expert_w_gate": nrm(ks[22], (DEPTH, N_EXPERTS, D_MODEL, D_EXPERT), D_MODEL ** -0.5),
        "expert_w_up": nrm(ks[23], (DEPTH, N_EXPERTS, D_MODEL, D_EXPERT), D_MODEL ** -0.5),
        "expert_w_down": nrm(ks[24], (DEPTH, N_EXPERTS, D_EXPERT, D_MODEL), D_EXPERT ** -0.5),
        "ple_norm": gain(ks[25], (DEPTH, D_MODEL)),
        "ple_w_gate": nrm(ks[26], (DEPTH, D_MODEL, D_MODEL), D_MODEL ** -0.5),
        "ple_w_proj": nrm(ks[27], (DEPTH, PLE_DIM, D_MODEL), PLE_DIM ** -0.5),
    }


def reference(x_prompt, x_sample, p_prompt, p_sample, attn_norm, attn_w_in, attn_q_a_norm,
              attn_kv_a_norm, attn_w_qb, attn_w_kvb, attn_q_gain, attn_k_gain, attn_w_o,
              conv_norm, conv_w_in, conv_w, conv_w_out, moe_norm, router_group_w, router_group_b,
              router_expert_w, router_expert_b, expert_w_gate, expert_w_up, expert_w_down,
              ple_norm, ple_w_gate, ple_w_proj):
    weights = (attn_norm, attn_w_in, attn_q_a_norm, attn_kv_a_norm, attn_w_qb, attn_w_kvb,
               attn_q_gain, attn_k_gain, attn_w_o, conv_norm, conv_w_in, conv_w, conv_w_out,
               moe_norm, router_group_w, router_group_b, router_expert_w, router_expert_b,
               expert_w_gate, expert_w_up, expert_w_down, ple_norm, ple_w_gate, ple_w_proj)
    y_prompt = _trunk(x_prompt, p_prompt, *weights)
    y_sample = _trunk(x_sample, p_sample, *weights)
    return (y_prompt, y_sample)
```

```python
import functools

import jax
import jax.numpy as jnp
from jax import lax
from jax.experimental import pallas as pl
from jax.experimental.pallas import tpu as pltpu

F32, BF16, I32 = jnp.float32, jnp.bfloat16, jnp.int32

D_MODEL = 1024
N_HEADS = 16
QK_NOPE, QK_ROPE, V_DIM = 64, 32, 64
QK_DIM = QK_NOPE + QK_ROPE
Q_LORA, KV_LORA = 384, 256
N_GROUPS, EXPERTS_PER_GROUP = 4, 8
N_EXPERTS = N_GROUPS * EXPERTS_PER_GROUP
D_EXPERT = 512
PLE_DIM = 256
ROPE_THETA = 10000.0
RMS_EPS = 1e-6
EXPERT_BLOCK = 128

LANES = 128
SUBLANES = 8
VMEM_LIMIT = 56 * 1024 * 1024

T_ATTN_IN = 256
T_Q = 512
T_TOK = 256

W_IN_EXT = Q_LORA + KV_LORA + 2 * LANES
HEAD_PAD = N_HEADS * LANES
GROUP_COL0, EXPERT_COL0 = 0, SUBLANES


def _cparams(sem):
    return pltpu.CompilerParams(dimension_semantics=sem, vmem_limit_bytes=VMEM_LIMIT)


def _rms(x, g):
    return x * lax.rsqrt(jnp.mean(x * x, axis=-1, keepdims=True) + RMS_EPS) * g


def _dot(a, b):
    return jnp.dot(a, b, preferred_element_type=F32)


def _dot_nt(a, b):
    return lax.dot_general(a, b, (((1,), (1,)), ((), ())), preferred_element_type=F32)


def _dot_tn(a, b):
    return lax.dot_general(a, b, (((0,), (0,)), ((), ())), preferred_element_type=F32)


def _attn_in_kernel(x_ref, gn_ref, win_ref, gqa_ref, gkva_ref, wq_ref, wqs_ref, wk_ref, wvt_ref,
                    gq_ref, gk_ref, cos_ref, sin_ref, q_ref, k_ref, vt_ref):
    x = x_ref[0]
    xn = _rms(x, gn_ref[...]).astype(BF16)
    a = _dot(xn, win_ref[...])
    cq = _rms(a[:, :Q_LORA], gqa_ref[...]).astype(BF16)
    ckv = _rms(a[:, Q_LORA:Q_LORA + KV_LORA], gkva_ref[...]).astype(BF16)
    kr = a[:, Q_LORA + KV_LORA:Q_LORA + KV_LORA + LANES]
    kr_sw = a[:, Q_LORA + KV_LORA + LANES:]
    cosv, sinv = cos_ref[...], sin_ref[...]
    gq, gk = gq_ref[...], gk_ref[...]

    q = _dot(cq, wq_ref[...])
    q_sw = _dot(cq, wqs_ref[...])
    kn = _dot(ckv, wk_ref[...])
    vt_ref[0, 0] = _dot_nt(wvt_ref[...], ckv).astype(BF16)

    ssq_rope = jnp.sum(kr * kr, axis=-1, keepdims=True)
    k_rope = kr * (gk[1:2] * cosv) + kr_sw * (gk[2:3] * sinv)
    q_cos, q_sin = gq[0:1] * cosv, gq[1:2] * sinv
    inv_d = 1.0 / QK_DIM
    for h in range(N_HEADS):
        sl = slice(h * LANES, (h + 1) * LANES)
        qh = q[:, sl]
        rq = lax.rsqrt(jnp.sum(qh * qh, axis=-1, keepdims=True) * inv_d + RMS_EPS)
        q_ref[0, h] = ((qh * q_cos + q_sw[:, sl] * q_sin) * rq).astype(BF16)
        kh = kn[:, sl]
        rk = lax.rsqrt((jnp.sum(kh * kh, axis=-1, keepdims=True) + ssq_rope) * inv_d + RMS_EPS)
        k_ref[0, h] = ((kh * gk[0:1] + k_rope) * rk).astype(BF16)


def _attn_in(x, w, cos_t, sin_t):
    B, S, _ = x.shape
    T = T_ATTN_IN
    nS = S // T
    const = lambda shape: pl.BlockSpec(shape, lambda b, j: (0,) * len(shape))
    return pl.pallas_call(
        _attn_in_kernel,
        grid=(B, nS),
        in_specs=[
            pl.BlockSpec((1, T, D_MODEL), lambda b, j: (b, j, 0)),
            const((1, D_MODEL)), const((D_MODEL, W_IN_EXT)), const((1, Q_LORA)), const((1, KV_LORA)),
            const((Q_LORA, HEAD_PAD)), const((Q_LORA, HEAD_PAD)), const((KV_LORA, HEAD_PAD)),
            const((N_HEADS * V_DIM, KV_LORA)), const((SUBLANES, LANES)), const((SUBLANES, LANES)),
            pl.BlockSpec((T, LANES), lambda b, j: (j, 0)),
            pl.BlockSpec((T, LANES), lambda b, j: (j, 0)),
        ],
        out_specs=[
            pl.BlockSpec((1, N_HEADS, T, LANES), lambda b, j: (b, 0, j, 0)),
            pl.BlockSpec((1, N_HEADS, T, LANES), lambda b, j: (b, 0, j, 0)),
            pl.BlockSpec((1, 1, N_HEADS * V_DIM, T), lambda b, j: (b, j, 0, 0)),
        ],
        out_shape=[
            jax.ShapeDtypeStruct((B, N_HEADS, S, LANES), BF16),
            jax.ShapeDtypeStruct((B, N_HEADS, S, LANES), BF16),
            jax.ShapeDtypeStruct((B, nS, N_HEADS * V_DIM, T), BF16),
        ],
        compiler_params=_cparams(("parallel", "parallel")),
        name="attn_in",
    )(x, w["attn_norm"], w["w_in_ext"], w["g_qa"], w["g_kva"], w["wq"], w["wq_sw"], w["wk"], w["wvt"],
      w["gq"], w["gk"], cos_t, sin_t)


def _flash_kernel(q_ref, k_ref, vt_ref, o_ref, m_ref, l_ref, acc_ref, *, n_kv, t_kv):
    q = q_ref[0, 0]
    m_ref[...] = jnp.full(m_ref.shape, -jnp.inf, F32)
    l_ref[...] = jnp.zeros(l_ref.shape, F32)
    acc_ref[...] = jnp.zeros(acc_ref.shape, F32)

    def body(ki, carry):
        kb = k_ref[0, 0, pl.ds(pl.multiple_of(ki * t_kv, t_kv), t_kv), :]
        s = _dot_nt(kb, q)
        m_old = m_ref[...]
        m_new = jnp.maximum(m_old, jnp.max(s, axis=0, keepdims=True))
        p = jnp.exp(s - m_new)
        alpha = jnp.exp(m_old - m_new)
        l_ref[...] = alpha * l_ref[...] + jnp.sum(p, axis=0, keepdims=True)
        acc_ref[...] = alpha * acc_ref[...] + _dot(vt_ref[0, ki], p.astype(BF16))
        m_ref[...] = m_new
        return carry

    lax.fori_loop(0, n_kv, body, 0)
    o_ref[0] = (acc_ref[...] * (1.0 / l_ref[...])).astype(BF16)


def _flash(q, k, vt):
    B, H, S, _ = q.shape
    t_kv = vt.shape[-1]
    n_kv = S // t_kv
    tq = min(T_Q, S)
    return pl.pallas_call(
        functools.partial(_flash_kernel, n_kv=n_kv, t_kv=t_kv),
        grid=(B, H, S // tq),
        in_specs=[
            pl.BlockSpec((1, 1, tq, LANES), lambda b, h, i: (b, h, i, 0)),
            pl.BlockSpec((1, 1, S, LANES), lambda b, h, i: (b, h, 0, 0)),
            pl.BlockSpec((1, n_kv, V_DIM, t_kv), lambda b, h, i: (b, 0, h, 0)),
        ],
        out_specs=pl.BlockSpec((1, V_DIM, tq), lambda b, h, i: (b, h, i)),
        out_shape=jax.ShapeDtypeStruct((B, H * V_DIM, S), BF16),
        scratch_shapes=[pltpu.VMEM((1, tq), F32), pltpu.VMEM((1, tq), F32), pltpu.VMEM((V_DIM, tq), F32)],
        compiler_params=_cparams(("parallel", "parallel", "parallel")),
        name="flash",
    )(q, k, vt)


def _router(x1, gm_ref, rw_ref, rb_ref, tri_ref, carry_ref, xn_ref, ri_ref, rf_ref, cnt_ref):
    T = x1.shape[0]

    @pl.when(pl.program_id(0) == 0)
    def _():
        carry_ref[...] = jnp.zeros(carry_ref.shape, F32)

    xn = _rms(x1, gm_ref[...])
    xn_ref[...] = xn
    x_hi = xn.astype(BF16)
    x_lo = (xn - x_hi.astype(F32)).astype(BF16)
    rw = rw_ref[...]
    l_hi = _dot(x_hi, rw)
    l_lo = _dot(x_lo, rw[:, :LANES])
    logits = l_hi[:, :LANES] + l_hi[:, LANES:] + l_lo + rb_ref[...]
    lt = logits.T

    row = lax.broadcasted_iota(I32, (SUBLANES, T), 0).astype(F32)
    neg = -jnp.inf
    lg = jnp.where(row < N_GROUPS, lt[GROUP_COL0:GROUP_COL0 + SUBLANES], neg)
    mg = jnp.max(lg, axis=0, keepdims=True)
    gidx = jnp.min(jnp.where(lg == mg, row, float(SUBLANES)), axis=0, keepdims=True)
    pg_top = 1.0 / jnp.sum(jnp.exp(lg - mg), axis=0, keepdims=True)

    def grp(g):
        return lt[EXPERT_COL0 + g * EXPERTS_PER_GROUP:EXPERT_COL0 + (g + 1) * EXPERTS_PER_GROUP]

    le = jnp.where(gidx == 0.0, grp(0), jnp.where(gidx == 1.0, grp(1), jnp.where(gidx == 2.0, grp(2), grp(3))))
    m1 = jnp.max(le, axis=0, keepdims=True)
    i1 = jnp.min(jnp.where(le == m1, row, float(SUBLANES)), axis=0, keepdims=True)
    le2 = jnp.where(row == i1, neg, le)
    m2 = jnp.max(le2, axis=0, keepdims=True)
    i2 = jnp.min(jnp.where(le2 == m2, row, float(SUBLANES)), axis=0, keepdims=True)
    e21 = jnp.exp(m2 - m1)
    inv = 1.0 / (1.0 + e21)
    g1 = pg_top * inv
    g2 = pg_top * e21 * inv
    e1 = gidx * EXPERTS_PER_GROUP + i1
    e2 = gidx * EXPERTS_PER_GROUP + i2

    erow = lax.broadcasted_iota(I32, (N_EXPERTS, T), 0).astype(F32)
    oh1 = erow == e1
    oh2 = erow == e2
    oh = jnp.where(oh1, 1.0, jnp.where(oh2, 1.0, 0.0))
    before = _dot(oh.astype(BF16), tri_ref[...]) + carry_ref[...]
    r1 = jnp.sum(jnp.where(oh1, before, 0.0), axis=0, keepdims=True)
    r2 = jnp.sum(jnp.where(oh2, before, 0.0), axis=0, keepdims=True)
    carry = carry_ref[...] + jnp.sum(oh, axis=1, keepdims=True)
    carry_ref[...] = carry
    cnt_ref[...] = carry[:, :LANES].astype(I32)

    ri = jnp.where(row == 0.0, e1, jnp.where(row == 1.0, e2, jnp.where(row == 2.0, r1, jnp.where(row == 3.0, r2, 0.0))))
    ri_ref[...] = ri.astype(I32)
    rf_ref[...] = jnp.where(row == 0.0, g1, jnp.where(row == 1.0, g2, 0.0))


def _router_specs(T, n_tok):
    const = lambda shape: pl.BlockSpec(shape, lambda i: (0,) * len(shape))
    in_specs = [const((1, D_MODEL)), const((D_MODEL, 2 * LANES)), const((1, LANES)), const((T, T))]
    out_specs = [
        pl.BlockSpec((T, D_MODEL), lambda i: (i, 0)),
        pl.BlockSpec((SUBLANES, T), lambda i: (0, i)),
        pl.BlockSpec((SUBLANES, T), lambda i: (0, i)),
        const((N_EXPERTS, LANES)),
    ]
    out_shape = [
        jax.ShapeDtypeStruct((n_tok, D_MODEL), F32),
        jax.ShapeDtypeStruct((SUBLANES, n_tok), I32),
        jax.ShapeDtypeStruct((SUBLANES, n_tok), F32),
        jax.ShapeDtypeStruct((N_EXPERTS, LANES), I32),
    ]
    return in_specs, out_specs, out_shape


def _attn_out_kernel(h_ref, ot_ref, wo_ref, gm_ref, rw_ref, rb_ref, tri_ref,
                     h1_ref, xn_ref, ri_ref, rf_ref, cnt_ref, carry_ref):
    x1 = h_ref[...] + _dot_tn(ot_ref[0], wo_ref[...])
    h1_ref[...] = x1
    _router(x1, gm_ref, rw_ref, rb_ref, tri_ref, carry_ref, xn_ref, ri_ref, rf_ref, cnt_ref)


def _attn_out(h, ot, w, lw):
    n_tok = h.shape[0]
    S = ot.shape[-1]
    T = T_TOK
    nS = S // T
    const = lambda shape: pl.BlockSpec(shape, lambda i: (0,) * len(shape))
    r_in, r_out, r_shape = _router_specs(T, n_tok)
    return pl.pallas_call(
        _attn_out_kernel,
        grid=(n_tok // T,),
        in_specs=[
            pl.BlockSpec((T, D_MODEL), lambda i: (i, 0)),
            pl.BlockSpec((1, N_HEADS * V_DIM, T), lambda i: (i // nS, 0, i % nS)),
            const((N_HEADS * V_DIM, D_MODEL)),
        ] + r_in,
        out_specs=[pl.BlockSpec((T, D_MODEL), lambda i: (i, 0))] + r_out,
        out_shape=[jax.ShapeDtypeStruct((n_tok, D_MODEL), F32)] + r_shape,
        scratch_shapes=[pltpu.VMEM((N_EXPERTS, T), F32)],
        compiler_params=_cparams(("arbitrary",)),
        name="attn_out_router",
    )(h, ot, w["w_o"], lw["moe_norm"], lw["rw"], lw["rb"], w["tri"])


def _conv_kernel(h_ref, hp_ref, hn_ref, gn_ref, win_ref, cw_ref, wout_ref, gm_ref, rw_ref, rb_ref, tri_ref,
                 h1_ref, xn_ref, ri_ref, rf_ref, cnt_ref, carry_ref, *, tiles_per_seq):
    T = h_ref.shape[0]
    i = pl.program_id(0)
    x = h_ref[...]
    gn = gn_ref[...]
    bcu = _dot(_rms(x, gn).astype(BF16), win_ref[...])
    b = bcu[:, :D_MODEL]
    cu = bcu[:, D_MODEL:2 * D_MODEL] * bcu[:, 2 * D_MODEL:]
    halo = jnp.concatenate([hp_ref[...], hn_ref[...]], axis=0)
    hcu = _dot(_rms(halo, gn).astype(BF16), win_ref[:, D_MODEL:])
    hcu = hcu[:, :D_MODEL] * hcu[:, D_MODEL:]
    first = (i % tiles_per_seq) == 0
    last = (i % tiles_per_seq) == tiles_per_seq - 1
    cu_before = jnp.where(first, 0.0, hcu[SUBLANES - 1:SUBLANES])
    cu_after = jnp.where(last, 0.0, hcu[SUBLANES:SUBLANES + 1])
    row = lax.broadcasted_iota(I32, (T, 1), 0)
    prev = jnp.where(row == 0, cu_before, pltpu.roll(cu, 1, 0))
    nxt = jnp.where(row == T - 1, cu_after, pltpu.roll(cu, T - 1, 0))
    cw = cw_ref[...]
    y = cw[0:1] * prev + cw[1:2] * cu + cw[2:3] * nxt
    x1 = x + _dot((b * y).astype(BF16), wout_ref[...])
    h1_ref[...] = x1
    _router(x1, gm_ref, rw_ref, rb_ref, tri_ref, carry_ref, xn_ref, ri_ref, rf_ref, cnt_ref)


def _conv(h, S, w, lw):
    n_tok = h.shape[0]
    T = T_TOK
    nS = S // T
    rows8 = T // SUBLANES
    n8 = n_tok // SUBLANES
    const = lambda shape: pl.BlockSpec(shape, lambda i: (0,) * len(shape))
    r_in, r_out, r_shape = _router_specs(T, n_tok)
    return pl.pallas_call(
        functools.partial(_conv_kernel, tiles_per_seq=nS),
        grid=(n_tok // T,),
        in_specs=[
            pl.BlockSpec((T, D_MODEL), lambda i: (i, 0)),
            pl.BlockSpec((SUBLANES, D_MODEL), lambda i: (jnp.maximum(i * rows8 - 1, 0), 0)),
            pl.BlockSpec((SUBLANES, D_MODEL), lambda i: (jnp.minimum((i + 1) * rows8, n8 - 1), 0)),
            const((1, D_MODEL)), const((D_MODEL, 3 * D_MODEL)), const((SUBLANES, D_MODEL)),
            const((D_MODEL, D_MODEL)),
        ] + r_in,
        out_specs=[pl.BlockSpec((T, D_MODEL), lambda i: (i, 0))] + r_out,
        out_shape=[jax.ShapeDtypeStruct((n_tok, D_MODEL), F32)] + r_shape,
        scratch_shapes=[pltpu.VMEM((N_EXPERTS, T), F32)],
        compiler_params=_cparams(("arbitrary",)),
        name="conv_router",
    )(h, h, h, w["conv_norm"], w["conv_w_in"], w["conv_w"], w["conv_w_out"],
      lw["moe_norm"], lw["rw"], lw["rb"], w["tri"])


def _plan_kernel(cnt_ref, ri_ref, dest_ref, info_ref, *, n_blocks):
    ri = ri_ref[...]
    eid = ri[0:2]
    offset = jnp.zeros(eid.shape, I32)
    blocks_before = jnp.int32(0)
    for e in range(N_EXPERTS):
        nb = (cnt_ref[e, 0] + (EXPERT_BLOCK - 1)) // EXPERT_BLOCK
        offset = jnp.where(eid == e, blocks_before * EXPERT_BLOCK, offset)

        def fill(j, c, e=e, base=blocks_before):
            info_ref[base + j] = e
            return c

        lax.fori_loop(0, nb, fill, 0)
        blocks_before = blocks_before + nb

    def fill_tail(j, c):
        info_ref[j] = N_EXPERTS - 1
        return c

    lax.fori_loop(blocks_before, n_blocks, fill_tail, 0)
    info_ref[n_blocks] = blocks_before
    dest_ref[...] = jnp.zeros(dest_ref.shape, I32)
    dest_ref[0:2, :] = ri[2:4] + offset


def _plan(cnt, ri, n_blocks):
    n_tok = ri.shape[1]
    return pl.pallas_call(
        functools.partial(_plan_kernel, n_blocks=n_blocks),
        in_specs=[pl.BlockSpec(memory_space=pltpu.SMEM), pl.BlockSpec(memory_space=pltpu.VMEM)],
        out_specs=[pl.BlockSpec(memory_space=pltpu.VMEM), pl.BlockSpec(memory_space=pltpu.SMEM)],
        out_shape=[jax.ShapeDtypeStruct((SUBLANES, n_tok), I32),
                   jax.ShapeDtypeStruct((n_blocks + 1,), I32)],
        compiler_params=pltpu.CompilerParams(vmem_limit_bytes=VMEM_LIMIT),
        name="plan",
    )(cnt, ri)


def _row_copy(src_ref, s, dst_ref, d, sem):
    return pltpu.make_async_copy(src_ref.at[pl.ds(s, 1)], dst_ref.at[pl.ds(d, 1)], sem)


def _dispatch_kernel(d0_ref, d1_ref, x_ref, xb_in_ref, xb_ref, sem):
    del xb_in_ref
    T = x_ref.shape[0]
    base = pl.program_id(0) * T

    def start(t, c):
        _row_copy(x_ref, t, xb_ref, d0_ref[base + t], sem.at[0]).start()
        _row_copy(x_ref, t, xb_ref, d1_ref[base + t], sem.at[1]).start()
        return c

    lax.fori_loop(0, T, start, 0, unroll=8)

    def wait(t, c):
        _row_copy(x_ref, t, xb_ref, d0_ref[base + t], sem.at[0]).wait()
        _row_copy(x_ref, t, xb_ref, d1_ref[base + t], sem.at[1]).wait()
        return c

    lax.fori_loop(0, T, wait, 0, unroll=8)


def _dispatch(xn, d0, d1, n_rows):
    n_tok = xn.shape[0]
    T = T_TOK
    xb0 = jnp.zeros((n_rows, D_MODEL), F32)
    return pl.pallas_call(
        _dispatch_kernel,
        grid_spec=pltpu.PrefetchScalarGridSpec(
            num_scalar_prefetch=2,
            grid=(n_tok // T,),
            in_specs=[pl.BlockSpec((T, D_MODEL), lambda i, d0, d1: (i, 0)),
                      pl.BlockSpec(memory_space=pl.ANY)],
            out_specs=pl.BlockSpec(memory_space=pl.ANY),
            scratch_shapes=[pltpu.SemaphoreType.DMA((2,))],
        ),
        out_shape=jax.ShapeDtypeStruct((n_rows, D_MODEL), F32),
        input_output_aliases={3: 0},
        compiler_params=_cparams(("arbitrary",)),
        name="dispatch",
    )(d0, d1, xn, xb0)


def _expert_kernel(info_ref, xb_ref, wg_ref, wu_ref, wd_ref, yb_ref, *, n_blocks):
    i = pl.program_id(0)

    @pl.when(i < info_ref[n_blocks])
    def _():
        x = xb_ref[...].astype(BF16)
        g = _dot(x, wg_ref[0])
        u = _dot(x, wu_ref[0])
        hmid = (g * jax.nn.sigmoid(g) * u).astype(BF16)
        yb_ref[...] = _dot(hmid, wd_ref[0])

    @pl.when(i >= info_ref[n_blocks])
    def _():
        yb_ref[...] = jnp.zeros(yb_ref.shape, F32)


def _experts(info, xb, lw):
    n_rows = xb.shape[0]
    n_blocks = n_rows // EXPERT_BLOCK
    return pl.pallas_call(
        functools.partial(_expert_kernel, n_blocks=n_blocks),
        grid_spec=pltpu.PrefetchScalarGridSpec(
            num_scalar_prefetch=1,
            grid=(n_blocks,),
            in_specs=[
                pl.BlockSpec((EXPERT_BLOCK, D_MODEL), lambda i, info: (i, 0)),
                pl.BlockSpec((1, D_MODEL, D_EXPERT), lambda i, info: (info[i], 0, 0)),
                pl.BlockSpec((1, D_MODEL, D_EXPERT), lambda i, info: (info[i], 0, 0)),
                pl.BlockSpec((1, D_EXPERT, D_MODEL), lambda i, info: (info[i], 0, 0)),
            ],
            out_specs=pl.BlockSpec((EXPERT_BLOCK, D_MODEL), lambda i, info: (i, 0)),
        ),
        out_shape=jax.ShapeDtypeStruct((n_rows, D_MODEL), F32),
        compiler_params=_cparams(("arbitrary",)),
        name="experts",
    )(info, xb, lw["w_gate"], lw["w_up"], lw["w_down"])


def _combine_kernel(d0_ref, d1_ref, h_ref, rf_ref, p_ref, yb_ref, gn_ref, wg_ref, wp_ref, o_ref,
                    y0_ref, y1_ref, sem):
    T = h_ref.shape[0]
    base = pl.program_id(0) * T

    def start(t, c):
        _row_copy(yb_ref, d0_ref[base + t], y0_ref, t, sem.at[0]).start()
        _row_copy(yb_ref, d1_ref[base + t], y1_ref, t, sem.at[1]).start()
        return c

    lax.fori_loop(0, T, start, 0, unroll=8)

    proj = _dot(p_ref[...].astype(BF16), wp_ref[...])
    gates = jnp.concatenate([rf_ref[...], jnp.zeros((LANES - SUBLANES, T), F32)], axis=0).T

    def wait(t, c):
        _row_copy(yb_ref, d0_ref[base + t], y0_ref, t, sem.at[0]).wait()
        _row_copy(yb_ref, d1_ref[base + t], y1_ref, t, sem.at[1]).wait()
        return c

    lax.fori_loop(0, T, wait, 0, unroll=8)

    h2 = h_ref[...] + gates[:, 0:1] * y0_ref[...] + gates[:, 1:2] * y1_ref[...]
    gate = jax.nn.sigmoid(_dot(_rms(h2, gn_ref[...]).astype(BF16), wg_ref[...]))
    o_ref[...] = h2 + gate * proj


def _combine(h, rf, p, yb, d0, d1, lw):
    n_tok = h.shape[0]
    T = T_TOK
    const = lambda shape: pl.BlockSpec(shape, lambda i, d0, d1: (0,) * len(shape))
    return pl.pallas_call(
        _combine_kernel,
        grid_spec=pltpu.PrefetchScalarGridSpec(
            num_scalar_prefetch=2,
            grid=(n_tok // T,),
            in_specs=[
                pl.BlockSpec((T, D_MODEL), lambda i, d0, d1: (i, 0)),
                pl.BlockSpec((SUBLANES, T), lambda i, d0, d1: (0, i)),
                pl.BlockSpec((T, PLE_DIM), lambda i, d0, d1: (i, 0)),
                pl.BlockSpec(memory_space=pl.ANY),
                const((1, D_MODEL)), const((D_MODEL, D_MODEL)), const((PLE_DIM, D_MODEL)),
            ],
            out_specs=pl.BlockSpec((T, D_MODEL), lambda i, d0, d1: (i, 0)),
            scratch_shapes=[pltpu.VMEM((T, D_MODEL), F32), pltpu.VMEM((T, D_MODEL), F32),
                            pltpu.SemaphoreType.DMA((2,))],
        ),
        out_shape=jax.ShapeDtypeStruct((n_tok, D_MODEL), F32),
        compiler_params=_cparams(("arbitrary",)),
        name="combine_ple",
    )(d0, d1, h, rf, p, yb, lw["ple_norm"], lw["ple_w_gate"], lw["ple_w_proj"])


def _pad_lanes(x, lo, width=LANES):
    n = x.shape[-1]
    pad = [(0, 0)] * (x.ndim - 1) + [(lo, width - lo - n)]
    return jnp.pad(x, pad)


def _rotate_half_cols(x):
    half = QK_ROPE // 2
    return jnp.concatenate([-x[..., half:], x[..., :half]], axis=-1)


def _swap_halves(g):
    half = QK_ROPE // 2
    return jnp.concatenate([g[..., half:], g[..., :half]], axis=-1)


def _prepare(attn_norm, attn_w_in, attn_q_a_norm, attn_kv_a_norm, attn_w_qb, attn_w_kvb, attn_q_gain,
             attn_k_gain, attn_w_o, conv_norm, conv_w_in, conv_w, conv_w_out, moe_norm, router_group_w,
             router_group_b, router_expert_w, router_expert_b, expert_w_gate, expert_w_up, expert_w_down,
             ple_norm, ple_w_gate, ple_w_proj):
    w = {}
    w_in = attn_w_in[0]
    rope_cols = w_in[:, Q_LORA + KV_LORA:]
    w["w_in_ext"] = jnp.concatenate(
        [w_in[:, :Q_LORA + KV_LORA], _pad_lanes(rope_cols, QK_NOPE), _pad_lanes(_rotate_half_cols(rope_cols), QK_NOPE)],
        axis=1).astype(BF16)
    w["attn_norm"] = attn_norm[0][None]
    w["g_qa"] = attn_q_a_norm[0][None]
    w["g_kva"] = attn_kv_a_norm[0][None]
    wqb = attn_w_qb[0].reshape(Q_LORA, N_HEADS, QK_DIM)
    w["wq"] = _pad_lanes(wqb, 0).reshape(Q_LORA, HEAD_PAD).astype(BF16)
    w["wq_sw"] = _pad_lanes(_rotate_half_cols(wqb[..., QK_NOPE:]), QK_NOPE).reshape(Q_LORA, HEAD_PAD).astype(BF16)
    wkvb = attn_w_kvb[0].reshape(KV_LORA, N_HEADS, QK_NOPE + V_DIM)
    w["wk"] = _pad_lanes(wkvb[..., :QK_NOPE], 0).reshape(KV_LORA, HEAD_PAD).astype(BF16)
    w["wvt"] = wkvb[..., QK_NOPE:].reshape(KV_LORA, N_HEADS * V_DIM).T.astype(BF16)
    scale = 1.0 / float(QK_DIM) ** 0.5
    gqs = attn_q_gain[0] * scale
    gk = attn_k_gain[0]
    zero = jnp.zeros((SUBLANES - 2, LANES), F32)
    w["gq"] = jnp.concatenate([_pad_lanes(gqs, 0)[None], _pad_lanes(_swap_halves(gqs[QK_NOPE:]), QK_NOPE)[None], zero])
    w["gk"] = jnp.concatenate([_pad_lanes(gk[:QK_NOPE], 0)[None], _pad_lanes(gk[QK_NOPE:], QK_NOPE)[None],
                               _pad_lanes(_swap_halves(gk[QK_NOPE:]), QK_NOPE)[None], zero[1:]])
    w["w_o"] = attn_w_o[0].astype(BF16)
    w["conv_norm"] = conv_norm[0][None]
    w["conv_w_in"] = conv_w_in[0].astype(BF16)
    w["conv_w"] = jnp.pad(conv_w[0], ((0, SUBLANES - conv_w.shape[1]), (0, 0)))
    w["conv_w_out"] = conv_w_out[0].astype(BF16)
    tri = lax.broadcasted_iota(I32, (T_TOK, T_TOK), 0) < lax.broadcasted_iota(I32, (T_TOK, T_TOK), 1)
    w["tri"] = tri.astype(BF16)
    layers = []
    for i in range(moe_norm.shape[0]):
        rw = jnp.zeros((D_MODEL, LANES), F32)
        rw = rw.at[:, GROUP_COL0:GROUP_COL0 + N_GROUPS].set(router_group_w[i])
        rw = rw.at[:, EXPERT_COL0:EXPERT_COL0 + N_EXPERTS].set(router_expert_w[i])
        rw_hi = rw.astype(BF16)
        rw_lo = (rw - rw_hi.astype(F32)).astype(BF16)
        rb = jnp.zeros((1, LANES), F32)
        rb = rb.at[0, GROUP_COL0:GROUP_COL0 + N_GROUPS].set(router_group_b[i])
        rb = rb.at[0, EXPERT_COL0:EXPERT_COL0 + N_EXPERTS].set(router_expert_b[i])
        layers.append(dict(
            moe_norm=moe_norm[i][None], rw=jnp.concatenate([rw_hi, rw_lo], axis=1), rb=rb,
            w_gate=expert_w_gate[i].astype(BF16), w_up=expert_w_up[i].astype(BF16),
            w_down=expert_w_down[i].astype(BF16),
            ple_norm=ple_norm[i][None], ple_w_gate=ple_w_gate[i].astype(BF16),
            ple_w_proj=ple_w_proj[i].astype(BF16)))
    return w, layers


def _rope_tiles(S):
    pos = jnp.arange(S, dtype=F32)
    inv = ROPE_THETA ** (-jnp.arange(0, QK_ROPE, 2, dtype=F32) / QK_ROPE)
    ang = pos[:, None] * inv[None, :]
    cos, sin = jnp.cos(ang), jnp.sin(ang)
    cos_t = jnp.concatenate([jnp.ones((S, QK_NOPE), F32), cos, cos, jnp.zeros((S, LANES - QK_DIM), F32)], axis=1)
    sin_t = jnp.concatenate([jnp.zeros((S, QK_NOPE), F32), sin, sin, jnp.zeros((S, LANES - QK_DIM), F32)], axis=1)
    return cos_t, sin_t


def _moe_ple(h1, xn, ri, rf, cnt, p, lw):
    n_tok = h1.shape[0]
    n_assign = 2 * n_tok
    n_rows = -(-n_assign // EXPERT_BLOCK) * EXPERT_BLOCK + N_EXPERTS * EXPERT_BLOCK
    dest, info = _plan(cnt, ri, n_rows // EXPERT_BLOCK)
    d0, d1 = dest[0], dest[1]
    xb = _dispatch(xn, d0, d1, n_rows)
    yb = _experts(info, xb, lw)
    return _combine(h1, rf, p, yb, d0, d1, lw)


def _trunk(x, p, w, layers):
    B, S, _ = x.shape
    n_tok = B * S
    cos_t, sin_t = _rope_tiles(S)
    h = x.reshape(n_tok, D_MODEL)
    p = p.reshape(p.shape[0], n_tok, PLE_DIM)
    q, k, vt = _attn_in(x, w, cos_t, sin_t)
    ot = _flash(q, k, vt)
    h1, xn, ri, rf, cnt = _attn_out(h, ot, w, layers[0])
    h = _moe_ple(h1, xn, ri, rf, cnt, p[0], layers[0])
    h1, xn, ri, rf, cnt = _conv(h, S, w, layers[1])
    h = _moe_ple(h1, xn, ri, rf, cnt, p[1], layers[1])
    return h.reshape(B, S, D_MODEL)


def kernel(x_prompt, x_sample, p_prompt, p_sample, attn_norm, attn_w_in, attn_q_a_norm, attn_kv_a_norm, attn_w_qb, attn_w_kvb, attn_q_gain, attn_k_gain, attn_w_o, conv_norm, conv_w_in, conv_w, conv_w_out, moe_norm, router_group_w, router_group_b, router_expert_w, router_expert_b, expert_w_gate, expert_w_up, expert_w_down, ple_norm, ple_w_gate, ple_w_proj):
    assert x_prompt.shape[-1] == D_MODEL and moe_norm.shape[0] == 2
    assert attn_w_in.shape == (1, D_MODEL, Q_LORA + KV_LORA + QK_ROPE)
    assert attn_w_qb.shape == (1, Q_LORA, N_HEADS * QK_DIM)
    assert attn_w_kvb.shape == (1, KV_LORA, N_HEADS * (QK_NOPE + V_DIM))
    assert expert_w_gate.shape[1:] == (N_EXPERTS, D_MODEL, D_EXPERT)
    w, layers = _prepare(attn_norm, attn_w_in, attn_q_a_norm, attn_kv_a_norm, attn_w_qb, attn_w_kvb,
                         attn_q_gain, attn_k_gain, attn_w_o, conv_norm, conv_w_in, conv_w, conv_w_out,
                         moe_norm, router_group_w, router_group_b, router_expert_w, router_expert_b,
                         expert_w_gate, expert_w_up, expert_w_down, ple_norm, ple_w_gate, ple_w_proj)
    return (_trunk(x_prompt, p_prompt, w, layers), _trunk(x_sample, p_sample, w, layers))
```

```python
import functools

import jax
import jax.numpy as jnp
from jax import lax
from jax.experimental import pallas as pl
from jax.experimental.pallas import tpu as pltpu

F32, BF16, I32 = jnp.float32, jnp.bfloat16, jnp.int32

D_MODEL = 1024
N_HEADS = 16
QK_NOPE, QK_ROPE, V_DIM = 64, 32, 64
QK_DIM = QK_NOPE + QK_ROPE
Q_LORA, KV_LORA = 384, 256
N_GROUPS, EXPERTS_PER_GROUP = 4, 8
N_EXPERTS = N_GROUPS * EXPERTS_PER_GROUP
D_EXPERT = 512
PLE_DIM = 256
ROPE_THETA = 10000.0
RMS_EPS = 1e-6
EXPERT_BLOCK = 128

LANES = 128
SUBLANES = 8
VMEM_LIMIT = 56 * 1024 * 1024

T_ATTN_IN = 256
T_Q = 512
T_TOK = 256
KV_UNROLL = 8
BLOCKS_PER_STEP = 4
LOG2_E = 1.4426950408889634

BF16_ROWS = 16
V_EXT = V_DIM + BF16_ROWS
W_IN_EXT = Q_LORA + KV_LORA + 2 * LANES
HEAD_PAD = N_HEADS * LANES
GROUP_COL0, EXPERT_COL0 = 0, SUBLANES


def _cparams(sem):
    return pltpu.CompilerParams(dimension_semantics=sem, vmem_limit_bytes=VMEM_LIMIT)


def _rms(x, g):
    return x * lax.rsqrt(jnp.mean(x * x, axis=-1, keepdims=True) + RMS_EPS) * g


def _dot(a, b):
    return jnp.dot(a, b, preferred_element_type=F32)


def _dot_nt(a, b):
    return lax.dot_general(a, b, (((1,), (1,)), ((), ())), preferred_element_type=F32)


def _dot_tn(a, b):
    return lax.dot_general(a, b, (((0,), (0,)), ((), ())), preferred_element_type=F32)


def _attn_in_kernel(x_ref, gn_ref, win_ref, gqa_ref, gkva_ref, wq_ref, wqs_ref, wk_ref, wvt_ref,
                    gq_ref, gk_ref, cos_ref, sin_ref, q_ref, k_ref, vt_ref):
    x = x_ref[0]
    xn = _rms(x, gn_ref[...]).astype(BF16)
    a = _dot(xn, win_ref[...])
    cq = _rms(a[:, :Q_LORA], gqa_ref[...]).astype(BF16)
    ckv = _rms(a[:, Q_LORA:Q_LORA + KV_LORA], gkva_ref[...]).astype(BF16)
    kr = a[:, Q_LORA + KV_LORA:Q_LORA + KV_LORA + LANES]
    kr_sw = a[:, Q_LORA + KV_LORA + LANES:]
    cosv, sinv = cos_ref[...], sin_ref[...]
    gq, gk = gq_ref[...], gk_ref[...]

    q = _dot(cq, wq_ref[...])
    q_sw = _dot(cq, wqs_ref[...])
    kn = _dot(ckv, wk_ref[...])
    vt = _dot_nt(wvt_ref[...], ckv)
    vrow = lax.broadcasted_iota(I32, vt.shape, 0) % V_EXT
    vt_ref[0, 0] = jnp.where(vrow == V_DIM, 1.0, vt).astype(BF16)

    ssq_rope = jnp.sum(kr * kr, axis=-1, keepdims=True)
    k_rope = kr * (gk[1:2] * cosv) + kr_sw * (gk[2:3] * sinv)
    q_cos, q_sin = gq[0:1] * cosv, gq[1:2] * sinv
    inv_d = 1.0 / QK_DIM
    for h in range(N_HEADS):
        sl = slice(h * LANES, (h + 1) * LANES)
        qh = q[:, sl]
        rq = lax.rsqrt(jnp.sum(qh * qh, axis=-1, keepdims=True) * inv_d + RMS_EPS)
        q_ref[0, h] = ((qh * q_cos + q_sw[:, sl] * q_sin) * rq).astype(BF16)
        kh = kn[:, sl]
        rk = lax.rsqrt((jnp.sum(kh * kh, axis=-1, keepdims=True) + ssq_rope) * inv_d + RMS_EPS)
        k_ref[0, h] = ((kh * gk[0:1] + k_rope) * rk).astype(BF16)


def _attn_in(x, w, cos_t, sin_t):
    B, S, _ = x.shape
    T = T_ATTN_IN
    nS = S // T
    const = lambda shape: pl.BlockSpec(shape, lambda b, j: (0,) * len(shape))
    return pl.pallas_call(
        _attn_in_kernel,
        grid=(B, nS),
        in_specs=[
            pl.BlockSpec((1, T, D_MODEL), lambda b, j: (b, j, 0)),
            const((1, D_MODEL)), const((D_MODEL, W_IN_EXT)), const((1, Q_LORA)), const((1, KV_LORA)),
            const((Q_LORA, HEAD_PAD)), const((Q_LORA, HEAD_PAD)), const((KV_LORA, HEAD_PAD)),
            const((N_HEADS * V_EXT, KV_LORA)), const((SUBLANES, LANES)), const((SUBLANES, LANES)),
            pl.BlockSpec((T, LANES), lambda b, j: (j, 0)),
            pl.BlockSpec((T, LANES), lambda b, j: (j, 0)),
        ],
        out_specs=[
            pl.BlockSpec((1, N_HEADS, T, LANES), lambda b, j: (b, 0, j, 0)),
            pl.BlockSpec((1, N_HEADS, T, LANES), lambda b, j: (b, 0, j, 0)),
            pl.BlockSpec((1, 1, N_HEADS * V_EXT, T), lambda b, j: (b, j, 0, 0)),
        ],
        out_shape=[
            jax.ShapeDtypeStruct((B, N_HEADS, S, LANES), BF16),
            jax.ShapeDtypeStruct((B, N_HEADS, S, LANES), BF16),
            jax.ShapeDtypeStruct((B, nS, N_HEADS * V_EXT, T), BF16),
        ],
        compiler_params=_cparams(("parallel", "parallel")),
        name="attn_in",
    )(x, w["attn_norm"], w["w_in_ext"], w["g_qa"], w["g_kva"], w["wq"], w["wq_sw"], w["wk"], w["wvt"],
      w["gq"], w["gk"], cos_t, sin_t)


def _flash_kernel(q_ref, k_ref, vt_ref, o_ref, m_ref, acc_ref, s_ref, cm_ref, *, n_kv, t_kv, unroll):
    q = q_ref[0, 0]
    m_ref[...] = jnp.full(m_ref.shape, -jnp.inf, F32)
    acc_ref[...] = jnp.zeros(acc_ref.shape, F32)

    def scores(j):
        kb = k_ref[0, 0, pl.ds(pl.multiple_of(j * t_kv, t_kv), t_kv), :]
        return _dot_nt(kb, q)

    s0 = scores(0)
    s_ref[0] = s0
    cm_ref[...] = jnp.max(s0, axis=0, keepdims=True)
    s_ref[1] = scores(min(1, n_kv - 1))

    def body(it, carry):
        s, s_next, cm = s_ref[0], s_ref[1], cm_ref[...]
        m, acc = m_ref[...], acc_ref[...]
        for u in range(unroll):
            j = it * unroll + u
            s_next2 = scores(jnp.minimum(j + 2, n_kv - 1))
            cm_next = jnp.max(s_next, axis=0, keepdims=True)
            m_new = jnp.maximum(m, cm)
            p = jnp.exp2(s - m_new)
            alpha = jnp.exp2(m - m_new)
            acc = alpha * acc + _dot(vt_ref[0, j], p.astype(BF16))
            m, s, cm, s_next = m_new, s_next, cm_next, s_next2
        s_ref[0], s_ref[1], cm_ref[...] = s, s_next, cm
        m_ref[...], acc_ref[...] = m, acc
        return carry

    lax.fori_loop(0, n_kv // unroll, body, 0)
    o_ref[0] = (acc_ref[0:V_DIM] * (1.0 / acc_ref[V_DIM:V_DIM + 1])).astype(BF16)


def _flash(q, k, vt):
    B, H, S, _ = q.shape
    t_kv = vt.shape[-1]
    n_kv = S // t_kv
    tq = min(T_Q, S)
    unroll = KV_UNROLL if n_kv % KV_UNROLL == 0 else 1
    return pl.pallas_call(
        functools.partial(_flash_kernel, n_kv=n_kv, t_kv=t_kv, unroll=unroll),
        grid=(B, H, S // tq),
        in_specs=[
            pl.BlockSpec((1, 1, tq, LANES), lambda b, h, i: (b, h, i, 0)),
            pl.BlockSpec((1, 1, S, LANES), lambda b, h, i: (b, h, 0, 0)),
            pl.BlockSpec((1, n_kv, V_EXT, t_kv), lambda b, h, i: (b, 0, h, 0)),
        ],
        out_specs=pl.BlockSpec((1, V_DIM, tq), lambda b, h, i: (b, h, i)),
        out_shape=jax.ShapeDtypeStruct((B, H * V_DIM, S), BF16),
        scratch_shapes=[pltpu.VMEM((1, tq), F32), pltpu.VMEM((V_EXT, tq), F32),
                        pltpu.VMEM((2, t_kv, tq), F32), pltpu.VMEM((1, tq), F32)],
        compiler_params=_cparams(("parallel", "parallel", "parallel")),
        name="flash",
    )(q, k, vt)


def _router(x1, gm_ref, rw_ref, rb_ref, tri_ref, carry_ref, xn_ref, ri_ref, rf_ref, cnt_ref):
    T = x1.shape[0]

    @pl.when(pl.program_id(0) == 0)
    def _():
        carry_ref[...] = jnp.zeros(carry_ref.shape, F32)

    xn = _rms(x1, gm_ref[...])
    xn_ref[...] = xn
    x_hi = xn.astype(BF16)
    x_lo = (xn - x_hi.astype(F32)).astype(BF16)
    rw = rw_ref[...]
    l_hi = _dot(x_hi, rw)
    l_lo = _dot(x_lo, rw[:, :LANES])
    logits = l_hi[:, :LANES] + l_hi[:, LANES:] + l_lo + rb_ref[...]
    lt = logits.T

    row = lax.broadcasted_iota(I32, (SUBLANES, T), 0).astype(F32)
    neg = -jnp.inf
    lg = jnp.where(row < N_GROUPS, lt[GROUP_COL0:GROUP_COL0 + SUBLANES], neg)
    mg = jnp.max(lg, axis=0, keepdims=True)
    gidx = jnp.min(jnp.where(lg == mg, row, float(SUBLANES)), axis=0, keepdims=True)
    pg_top = 1.0 / jnp.sum(jnp.exp(lg - mg), axis=0, keepdims=True)

    def grp(g):
        return lt[EXPERT_COL0 + g * EXPERTS_PER_GROUP:EXPERT_COL0 + (g + 1) * EXPERTS_PER_GROUP]

    le = jnp.where(gidx == 0.0, grp(0), jnp.where(gidx == 1.0, grp(1), jnp.where(gidx == 2.0, grp(2), grp(3))))
    m1 = jnp.max(le, axis=0, keepdims=True)
    i1 = jnp.min(jnp.where(le == m1, row, float(SUBLANES)), axis=0, keepdims=True)
    le2 = jnp.where(row == i1, neg, le)
    m2 = jnp.max(le2, axis=0, keepdims=True)
    i2 = jnp.min(jnp.where(le2 == m2, row, float(SUBLANES)), axis=0, keepdims=True)
    e21 = jnp.exp(m2 - m1)
    inv = 1.0 / (1.0 + e21)
    g1 = pg_top * inv
    g2 = pg_top * e21 * inv
    e1 = gidx * EXPERTS_PER_GROUP + i1
    e2 = gidx * EXPERTS_PER_GROUP + i2

    erow = lax.broadcasted_iota(I32, (N_EXPERTS, T), 0).astype(F32)
    oh1 = erow == e1
    oh2 = erow == e2
    oh = jnp.where(oh1, 1.0, jnp.where(oh2, 1.0, 0.0))
    before = _dot(oh.astype(BF16), tri_ref[...]) + carry_ref[...]
    r1 = jnp.sum(jnp.where(oh1, before, 0.0), axis=0, keepdims=True)
    r2 = jnp.sum(jnp.where(oh2, before, 0.0), axis=0, keepdims=True)
    carry = carry_ref[...] + jnp.sum(oh, axis=1, keepdims=True)
    carry_ref[...] = carry
    cnt_ref[...] = carry[:, :LANES].astype(I32)

    ri = jnp.where(row == 0.0, e1, jnp.where(row == 1.0, e2, jnp.where(row == 2.0, r1, jnp.where(row == 3.0, r2, 0.0))))
    ri_ref[...] = ri.astype(I32)
    rf_ref[...] = jnp.where(row == 0.0, g1, jnp.where(row == 1.0, g2, 0.0))


def _router_specs(T, n_tok):
    const = lambda shape: pl.BlockSpec(shape, lambda i: (0,) * len(shape))
    in_specs = [const((1, D_MODEL)), const((D_MODEL, 2 * LANES)), const((1, LANES)), const((T, T))]
    out_specs = [
        pl.BlockSpec((T, D_MODEL), lambda i: (i, 0)),
        pl.BlockSpec((SUBLANES, T), lambda i: (0, i)),
        pl.BlockSpec((SUBLANES, T), lambda i: (0, i)),
        const((N_EXPERTS, LANES)),
    ]
    out_shape = [
        jax.ShapeDtypeStruct((n_tok, D_MODEL), F32),
        jax.ShapeDtypeStruct((SUBLANES, n_tok), I32),
        jax.ShapeDtypeStruct((SUBLANES, n_tok), F32),
        jax.ShapeDtypeStruct((N_EXPERTS, LANES), I32),
    ]
    return in_specs, out_specs, out_shape


def _attn_out_kernel(h_ref, ot_ref, wo_ref, gm_ref, rw_ref, rb_ref, tri_ref,
                     h1_ref, xn_ref, ri_ref, rf_ref, cnt_ref, carry_ref):
    x1 = h_ref[...] + _dot_tn(ot_ref[0], wo_ref[...])
    h1_ref[...] = x1
    _router(x1, gm_ref, rw_ref, rb_ref, tri_ref, carry_ref, xn_ref, ri_ref, rf_ref, cnt_ref)


def _attn_out(h, ot, w, lw):
    n_tok = h.shape[0]
    S = ot.shape[-1]
    T = T_TOK
    nS = S // T
    const = lambda shape: pl.BlockSpec(shape, lambda i: (0,) * len(shape))
    r_in, r_out, r_shape = _router_specs(T, n_tok)
    return pl.pallas_call(
        _attn_out_kernel,
        grid=(n_tok // T,),
        in_specs=[
            pl.BlockSpec((T, D_MODEL), lambda i: (i, 0)),
            pl.BlockSpec((1, N_HEADS * V_DIM, T), lambda i: (i // nS, 0, i % nS)),
            const((N_HEADS * V_DIM, D_MODEL)),
        ] + r_in,
        out_specs=[pl.BlockSpec((T, D_MODEL), lambda i: (i, 0))] + r_out,
        out_shape=[jax.ShapeDtypeStruct((n_tok, D_MODEL), F32)] + r_shape,
        scratch_shapes=[pltpu.VMEM((N_EXPERTS, T), F32)],
        compiler_params=_cparams(("arbitrary",)),
        name="attn_out_router",
    )(h, ot, w["w_o"], lw["moe_norm"], lw["rw"], lw["rb"], w["tri"])


def _conv_kernel(h_ref, hp_ref, hn_ref, gn_ref, win_ref, cw_ref, wout_ref, gm_ref, rw_ref, rb_ref, tri_ref,
                 h1_ref, xn_ref, ri_ref, rf_ref, cnt_ref, carry_ref, *, tiles_per_seq):
    T = h_ref.shape[0]
    i = pl.program_id(0)
    x = h_ref[...]
    gn = gn_ref[...]
    bcu = _dot(_rms(x, gn).astype(BF16), win_ref[...])
    b = bcu[:, :D_MODEL]
    cu = bcu[:, D_MODEL:2 * D_MODEL] * bcu[:, 2 * D_MODEL:]
    halo = jnp.concatenate([hp_ref[...], hn_ref[...]], axis=0)
    hcu = _dot(_rms(halo, gn).astype(BF16), win_ref[:, D_MODEL:])
    hcu = hcu[:, :D_MODEL] * hcu[:, D_MODEL:]
    first = (i % tiles_per_seq) == 0
    last = (i % tiles_per_seq) == tiles_per_seq - 1
    cu_before = jnp.where(first, 0.0, hcu[SUBLANES - 1:SUBLANES])
    cu_after = jnp.where(last, 0.0, hcu[SUBLANES:SUBLANES + 1])
    row = lax.broadcasted_iota(I32, (T, 1), 0)
    prev = jnp.where(row == 0, cu_before, pltpu.roll(cu, 1, 0))
    nxt = jnp.where(row == T - 1, cu_after, pltpu.roll(cu, T - 1, 0))
    cw = cw_ref[...]
    y = cw[0:1] * prev + cw[1:2] * cu + cw[2:3] * nxt
    x1 = x + _dot((b * y).astype(BF16), wout_ref[...])
    h1_ref[...] = x1
    _router(x1, gm_ref, rw_ref, rb_ref, tri_ref, carry_ref, xn_ref, ri_ref, rf_ref, cnt_ref)


def _conv(h, S, w, lw):
    n_tok = h.shape[0]
    T = T_TOK
    nS = S // T
    rows8 = T // SUBLANES
    n8 = n_tok // SUBLANES
    const = lambda shape: pl.BlockSpec(shape, lambda i: (0,) * len(shape))
    r_in, r_out, r_shape = _router_specs(T, n_tok)
    return pl.pallas_call(
        functools.partial(_conv_kernel, tiles_per_seq=nS),
        grid=(n_tok // T,),
        in_specs=[
            pl.BlockSpec((T, D_MODEL), lambda i: (i, 0)),
            pl.BlockSpec((SUBLANES, D_MODEL), lambda i: (jnp.maximum(i * rows8 - 1, 0), 0)),
            pl.BlockSpec((SUBLANES, D_MODEL), lambda i: (jnp.minimum((i + 1) * rows8, n8 - 1), 0)),
            const((1, D_MODEL)), const((D_MODEL, 3 * D_MODEL)), const((SUBLANES, D_MODEL)),
            const((D_MODEL, D_MODEL)),
        ] + r_in,
        out_specs=[pl.BlockSpec((T, D_MODEL), lambda i: (i, 0))] + r_out,
        out_shape=[jax.ShapeDtypeStruct((n_tok, D_MODEL), F32)] + r_shape,
        scratch_shapes=[pltpu.VMEM((N_EXPERTS, T), F32)],
        compiler_params=_cparams(("arbitrary",)),
        name="conv_router",
    )(h, h, h, w["conv_norm"], w["conv_w_in"], w["conv_w"], w["conv_w_out"],
      lw["moe_norm"], lw["rw"], lw["rb"], w["tri"])


def _plan_kernel(cnt_ref, ri_ref, dest_ref, info_ref, *, n_blocks):
    ri = ri_ref[...]
    eid = ri[0:2]
    offset = jnp.zeros(eid.shape, I32)
    blocks_before = jnp.int32(0)
    for e in range(N_EXPERTS):
        nb = (cnt_ref[e, 0] + (EXPERT_BLOCK - 1)) // EXPERT_BLOCK
        offset = jnp.where(eid == e, blocks_before * EXPERT_BLOCK, offset)

        def fill(j, c, e=e, base=blocks_before):
            info_ref[base + j] = e
            return c

        lax.fori_loop(0, nb, fill, 0)
        blocks_before = blocks_before + nb

    def fill_tail(j, c):
        info_ref[j] = N_EXPERTS - 1
        return c

    lax.fori_loop(blocks_before, n_blocks, fill_tail, 0)
    info_ref[n_blocks] = blocks_before
    dest_ref[...] = jnp.zeros(dest_ref.shape, I32)
    dest_ref[0:2, :] = ri[2:4] + offset


def _plan(cnt, ri, n_blocks):
    n_tok = ri.shape[1]
    return pl.pallas_call(
        functools.partial(_plan_kernel, n_blocks=n_blocks),
        in_specs=[pl.BlockSpec(memory_space=pltpu.SMEM), pl.BlockSpec(memory_space=pltpu.VMEM)],
        out_specs=[pl.BlockSpec(memory_space=pltpu.VMEM), pl.BlockSpec(memory_space=pltpu.SMEM)],
        out_shape=[jax.ShapeDtypeStruct((SUBLANES, n_tok), I32),
                   jax.ShapeDtypeStruct((n_blocks + 1,), I32)],
        compiler_params=pltpu.CompilerParams(vmem_limit_bytes=VMEM_LIMIT),
        name="plan",
    )(cnt, ri)


def _row_copy(src_ref, s, dst_ref, d, sem):
    return pltpu.make_async_copy(src_ref.at[pl.ds(s, 1)], dst_ref.at[pl.ds(d, 1)], sem)


def _dispatch_kernel(d0_ref, d1_ref, x_ref, xb_in_ref, xb_ref, sem):
    del xb_in_ref
    T = x_ref.shape[0]
    base = pl.program_id(0) * T

    def start(t, c):
        _row_copy(x_ref, t, xb_ref, d0_ref[base + t], sem.at[0]).start(priority=0)
        _row_copy(x_ref, t, xb_ref, d1_ref[base + t], sem.at[1]).start(priority=1)
        return c

    lax.fori_loop(0, T, start, 0, unroll=8)

    def wait(t, c):
        _row_copy(x_ref, t, xb_ref, d0_ref[base + t], sem.at[0]).wait()
        _row_copy(x_ref, t, xb_ref, d1_ref[base + t], sem.at[1]).wait()
        return c

    lax.fori_loop(0, T, wait, 0, unroll=8)


def _dispatch(xn, d0, d1, n_rows):
    n_tok = xn.shape[0]
    T = T_TOK
    xb0 = jnp.zeros((n_rows, D_MODEL), F32)
    return pl.pallas_call(
        _dispatch_kernel,
        grid_spec=pltpu.PrefetchScalarGridSpec(
            num_scalar_prefetch=2,
            grid=(n_tok // T,),
            in_specs=[pl.BlockSpec((T, D_MODEL), lambda i, d0, d1: (i, 0)),
                      pl.BlockSpec(memory_space=pl.ANY)],
            out_specs=pl.BlockSpec(memory_space=pl.ANY),
            scratch_shapes=[pltpu.SemaphoreType.DMA((2,))],
        ),
        out_shape=jax.ShapeDtypeStruct((n_rows, D_MODEL), F32),
        input_output_aliases={3: 0},
        compiler_params=_cparams(("arbitrary",)),
        name="dispatch",
    )(d0, d1, xn, xb0)


def _expert_kernel(info_ref, xb_ref, *refs, n_blocks):
    yb_ref = refs[-1]
    i = pl.program_id(0)
    for u in range(BLOCKS_PER_STEP):
        wg_ref, wu_ref, wd_ref = refs[3 * u:3 * u + 3]
        blk = i * BLOCKS_PER_STEP + u
        rows = pl.ds(u * EXPERT_BLOCK, EXPERT_BLOCK)

        @pl.when(blk < info_ref[n_blocks])
        def _():
            x = xb_ref[rows, :].astype(BF16)
            g = _dot(x, wg_ref[0])
            up = _dot(x, wu_ref[0])
            hmid = (g * jax.nn.sigmoid(g) * up).astype(BF16)
            yb_ref[rows, :] = _dot(hmid, wd_ref[0])

        @pl.when(blk >= info_ref[n_blocks])
        def _():
            yb_ref[rows, :] = jnp.zeros((EXPERT_BLOCK, D_MODEL), F32)


def _experts(info, xb, lw):
    n_rows = xb.shape[0]
    n_blocks = n_rows // EXPERT_BLOCK
    assert n_blocks % BLOCKS_PER_STEP == 0
    step_rows = BLOCKS_PER_STEP * EXPERT_BLOCK
    w_specs, w_args = [], []
    for u in range(BLOCKS_PER_STEP):
        pick = lambda i, info, u=u: (info[i * BLOCKS_PER_STEP + u], 0, 0)
        w_specs += [pl.BlockSpec((1, D_MODEL, D_EXPERT), pick), pl.BlockSpec((1, D_MODEL, D_EXPERT), pick),
                    pl.BlockSpec((1, D_EXPERT, D_MODEL), pick)]
        w_args += [lw["w_gate"], lw["w_up"], lw["w_down"]]
    return pl.pallas_call(
        functools.partial(_expert_kernel, n_blocks=n_blocks),
        grid_spec=pltpu.PrefetchScalarGridSpec(
            num_scalar_prefetch=1,
            grid=(n_blocks // BLOCKS_PER_STEP,),
            in_specs=[pl.BlockSpec((step_rows, D_MODEL), lambda i, info: (i, 0))] + w_specs,
            out_specs=pl.BlockSpec((step_rows, D_MODEL), lambda i, info: (i, 0)),
        ),
        out_shape=jax.ShapeDtypeStruct((n_rows, D_MODEL), F32),
        compiler_params=_cparams(("arbitrary",)),
        name="experts",
    )(info, xb, *w_args)


def _combine_kernel(d0_ref, d1_ref, h_ref, rf_ref, p_ref, yb_ref, gn_ref, wg_ref, wp_ref, o_ref,
                    y0_ref, y1_ref, sem):
    T = h_ref.shape[0]
    base = pl.program_id(0) * T

    def start(t, c):
        _row_copy(yb_ref, d0_ref[base + t], y0_ref, t, sem.at[0]).start(priority=0)
        _row_copy(yb_ref, d1_ref[base + t], y1_ref, t, sem.at[1]).start(priority=1)
        return c

    lax.fori_loop(0, T, start, 0, unroll=8)

    proj = _dot(p_ref[...].astype(BF16), wp_ref[...])
    gates = jnp.concatenate([rf_ref[...], jnp.zeros((LANES - SUBLANES, T), F32)], axis=0).T

    def wait(t, c):
        _row_copy(yb_ref, d0_ref[base + t], y0_ref, t, sem.at[0]).wait()
        _row_copy(yb_ref, d1_ref[base + t], y1_ref, t, sem.at[1]).wait()
        return c

    lax.fori_loop(0, T, wait, 0, unroll=8)

    h2 = h_ref[...] + gates[:, 0:1] * y0_ref[...] + gates[:, 1:2] * y1_ref[...]
    gate = jax.nn.sigmoid(_dot(_rms(h2, gn_ref[...]).astype(BF16), wg_ref[...]))
    o_ref[...] = h2 + gate * proj


def _combine(h, rf, p, yb, d0, d1, lw):
    n_tok = h.shape[0]
    T = T_TOK
    const = lambda shape: pl.BlockSpec(shape, lambda i, d0, d1: (0,) * len(shape))
    return pl.pallas_call(
        _combine_kernel,
        grid_spec=pltpu.PrefetchScalarGridSpec(
            num_scalar_prefetch=2,
            grid=(n_tok // T,),
            in_specs=[
                pl.BlockSpec((T, D_MODEL), lambda i, d0, d1: (i, 0)),
                pl.BlockSpec((SUBLANES, T), lambda i, d0, d1: (0, i)),
                pl.BlockSpec((T, PLE_DIM), lambda i, d0, d1: (i, 0)),
                pl.BlockSpec(memory_space=pl.ANY),
                const((1, D_MODEL)), const((D_MODEL, D_MODEL)), const((PLE_DIM, D_MODEL)),
            ],
            out_specs=pl.BlockSpec((T, D_MODEL), lambda i, d0, d1: (i, 0)),
            scratch_shapes=[pltpu.VMEM((T, D_MODEL), F32), pltpu.VMEM((T, D_MODEL), F32),
                            pltpu.SemaphoreType.DMA((2,))],
        ),
        out_shape=jax.ShapeDtypeStruct((n_tok, D_MODEL), F32),
        compiler_params=_cparams(("arbitrary",)),
        name="combine_ple",
    )(d0, d1, h, rf, p, yb, lw["ple_norm"], lw["ple_w_gate"], lw["ple_w_proj"])


def _pad_lanes(x, lo, width=LANES):
    n = x.shape[-1]
    pad = [(0, 0)] * (x.ndim - 1) + [(lo, width - lo - n)]
    return jnp.pad(x, pad)


def _rotate_half_cols(x):
    half = QK_ROPE // 2
    return jnp.concatenate([-x[..., half:], x[..., :half]], axis=-1)


def _swap_halves(g):
    half = QK_ROPE // 2
    return jnp.concatenate([g[..., half:], g[..., :half]], axis=-1)


def _prepare(attn_norm, attn_w_in, attn_q_a_norm, attn_kv_a_norm, attn_w_qb, attn_w_kvb, attn_q_gain,
             attn_k_gain, attn_w_o, conv_norm, conv_w_in, conv_w, conv_w_out, moe_norm, router_group_w,
             router_group_b, router_expert_w, router_expert_b, expert_w_gate, expert_w_up, expert_w_down,
             ple_norm, ple_w_gate, ple_w_proj):
    w = {}
    w_in = attn_w_in[0]
    rope_cols = w_in[:, Q_LORA + KV_LORA:]
    w["w_in_ext"] = jnp.concatenate(
        [w_in[:, :Q_LORA + KV_LORA], _pad_lanes(rope_cols, QK_NOPE), _pad_lanes(_rotate_half_cols(rope_cols), QK_NOPE)],
        axis=1).astype(BF16)
    w["attn_norm"] = attn_norm[0][None]
    w["g_qa"] = attn_q_a_norm[0][None]
    w["g_kva"] = attn_kv_a_norm[0][None]
    wqb = attn_w_qb[0].reshape(Q_LORA, N_HEADS, QK_DIM)
    w["wq"] = _pad_lanes(wqb, 0).reshape(Q_LORA, HEAD_PAD).astype(BF16)
    w["wq_sw"] = _pad_lanes(_rotate_half_cols(wqb[..., QK_NOPE:]), QK_NOPE).reshape(Q_LORA, HEAD_PAD).astype(BF16)
    wkvb = attn_w_kvb[0].reshape(KV_LORA, N_HEADS, QK_NOPE + V_DIM)
    w["wk"] = _pad_lanes(wkvb[..., :QK_NOPE], 0).reshape(KV_LORA, HEAD_PAD).astype(BF16)
    w["wvt"] = _pad_lanes(wkvb[..., QK_NOPE:], 0, V_EXT).reshape(KV_LORA, N_HEADS * V_EXT).T.astype(BF16)
    scale = LOG2_E / float(QK_DIM) ** 0.5
    gqs = attn_q_gain[0] * scale
    gk = attn_k_gain[0]
    zero = jnp.zeros((SUBLANES - 2, LANES), F32)
    w["gq"] = jnp.concatenate([_pad_lanes(gqs, 0)[None], _pad_lanes(_swap_halves(gqs[QK_NOPE:]), QK_NOPE)[None], zero])
    w["gk"] = jnp.concatenate([_pad_lanes(gk[:QK_NOPE], 0)[None], _pad_lanes(gk[QK_NOPE:], QK_NOPE)[None],
                               _pad_lanes(_swap_halves(gk[QK_NOPE:]), QK_NOPE)[None], zero[1:]])
    w["w_o"] = attn_w_o[0].astype(BF16)
    w["conv_norm"] = conv_norm[0][None]
    w["conv_w_in"] = conv_w_in[0].astype(BF16)
    w["conv_w"] = jnp.pad(conv_w[0], ((0, SUBLANES - conv_w.shape[1]), (0, 0)))
    w["conv_w_out"] = conv_w_out[0].astype(BF16)
    tri = lax.broadcasted_iota(I32, (T_TOK, T_TOK), 0) < lax.broadcasted_iota(I32, (T_TOK, T_TOK), 1)
    w["tri"] = tri.astype(BF16)
    layers = []
    for i in range(moe_norm.shape[0]):
        rw = jnp.zeros((D_MODEL, LANES), F32)
        rw = rw.at[:, GROUP_COL0:GROUP_COL0 + N_GROUPS].set(router_group_w[i])
        rw = rw.at[:, EXPERT_COL0:EXPERT_COL0 + N_EXPERTS].set(router_expert_w[i])
        rw_hi = rw.astype(BF16)
        rw_lo = (rw - rw_hi.astype(F32)).astype(BF16)
        rb = jnp.zeros((1, LANES), F32)
        rb = rb.at[0, GROUP_COL0:GROUP_COL0 + N_GROUPS].set(router_group_b[i])
        rb = rb.at[0, EXPERT_COL0:EXPERT_COL0 + N_EXPERTS].set(router_expert_b[i])
        layers.append(dict(
            moe_norm=moe_norm[i][None], rw=jnp.concatenate([rw_hi, rw_lo], axis=1), rb=rb,
            w_gate=expert_w_gate[i].astype(BF16), w_up=expert_w_up[i].astype(BF16),
            w_down=expert_w_down[i].astype(BF16),
            ple_norm=ple_norm[i][None], ple_w_gate=ple_w_gate[i].astype(BF16),
            ple_w_proj=ple_w_proj[i].astype(BF16)))
    return w, layers


def _rope_tiles(S):
    pos = jnp.arange(S, dtype=F32)
    inv = ROPE_THETA ** (-jnp.arange(0, QK_ROPE, 2, dtype=F32) / QK_ROPE)
    ang = pos[:, None] * inv[None, :]
    cos, sin = jnp.cos(ang), jnp.sin(ang)
    cos_t = jnp.concatenate([jnp.ones((S, QK_NOPE), F32), cos, cos, jnp.zeros((S, LANES - QK_DIM), F32)], axis=1)
    sin_t = jnp.concatenate([jnp.zeros((S, QK_NOPE), F32), sin, sin, jnp.zeros((S, LANES - QK_DIM), F32)], axis=1)
    return cos_t, sin_t


def _moe_ple(h1, xn, ri, rf, cnt, p, lw):
    n_tok = h1.shape[0]
    n_assign = 2 * n_tok
    n_rows = -(-n_assign // EXPERT_BLOCK) * EXPERT_BLOCK + N_EXPERTS * EXPERT_BLOCK
    dest, info = _plan(cnt, ri, n_rows // EXPERT_BLOCK)
    d0, d1 = dest[0], dest[1]
    xb = _dispatch(xn, d0, d1, n_rows)
    yb = _experts(info, xb, lw)
    return _combine(h1, rf, p, yb, d0, d1, lw)


def _trunk(x, p, w, layers):
    B, S, _ = x.shape
    n_tok = B * S
    cos_t, sin_t = _rope_tiles(S)
    h = x.reshape(n_tok, D_MODEL)
    p = p.reshape(p.shape[0], n_tok, PLE_DIM)
    q, k, vt = _attn_in(x, w, cos_t, sin_t)
    ot = _flash(q, k, vt)
    h1, xn, ri, rf, cnt = _attn_out(h, ot, w, layers[0])
    h = _moe_ple(h1, xn, ri, rf, cnt, p[0], layers[0])
    h1, xn, ri, rf, cnt = _conv(h, S, w, layers[1])
    h = _moe_ple(h1, xn, ri, rf, cnt, p[1], layers[1])
    return h.reshape(B, S, D_MODEL)


def kernel(x_prompt, x_sample, p_prompt, p_sample, attn_norm, attn_w_in, attn_q_a_norm, attn_kv_a_norm, attn_w_qb, attn_w_kvb, attn_q_gain, attn_k_gain, attn_w_o, conv_norm, conv_w_in, conv_w, conv_w_out, moe_norm, router_group_w, router_group_b, router_expert_w, router_expert_b, expert_w_gate, expert_w_up, expert_w_down, ple_norm, ple_w_gate, ple_w_proj):
    assert x_prompt.shape[-1] == D_MODEL and moe_norm.shape[0] == 2
    assert attn_w_in.shape == (1, D_MODEL, Q_LORA + KV_LORA + QK_ROPE)
    assert attn_w_qb.shape == (1, Q_LORA, N_HEADS * QK_DIM)
    assert attn_w_kvb.shape == (1, KV_LORA, N_HEADS * (QK_NOPE + V_DIM))
    assert expert_w_gate.shape[1:] == (N_EXPERTS, D_MODEL, D_EXPERT)
    w, layers = _prepare(attn_norm, attn_w_in, attn_q_a_norm, attn_kv_a_norm, attn_w_qb, attn_w_kvb,
                         attn_q_gain, attn_k_gain, attn_w_o, conv_norm, conv_w_in, conv_w, conv_w_out,
                         moe_norm, router_group_w, router_group_b, router_expert_w, router_expert_b,
                         expert_w_gate, expert_w_up, expert_w_down, ple_norm, ple_w_gate, ple_w_proj)
    return (_trunk(x_prompt, p_prompt, w, layers), _trunk(x_sample, p_sample, w, layers))
```

```python
import functools

import jax
import jax.numpy as jnp
from jax import lax
from jax.experimental import pallas as pl
from jax.experimental.pallas import tpu as pltpu

F32, BF16, I32 = jnp.float32, jnp.bfloat16, jnp.int32

D_MODEL = 1024
N_HEADS = 16
QK_NOPE, QK_ROPE, V_DIM = 64, 32, 64
QK_DIM = QK_NOPE + QK_ROPE
Q_LORA, KV_LORA = 384, 256
N_GROUPS, EXPERTS_PER_GROUP = 4, 8
N_EXPERTS = N_GROUPS * EXPERTS_PER_GROUP
D_EXPERT = 512
PLE_DIM = 256
ROPE_THETA = 10000.0
RMS_EPS = 1e-6
EXPERT_BLOCK = 128

LANES = 128
SUBLANES = 8
VMEM_LIMIT = 56 * 1024 * 1024

T_ATTN_IN = 256
T_Q = 512
T_TOK = 256
KV_UNROLL = 8
BLOCKS_PER_STEP = 4
LOG2_E = 1.4426950408889634
BOUND_MARGIN = 1.02
MIN_DENOMINATOR = 2.0 ** -60

BF16_ROWS = 16
V_EXT = V_DIM + BF16_ROWS
W_IN_EXT = Q_LORA + KV_LORA + 2 * LANES
HEAD_PAD = N_HEADS * LANES
GROUP_COL0, EXPERT_COL0 = 0, SUBLANES


def _cparams(sem):
    return pltpu.CompilerParams(dimension_semantics=sem, vmem_limit_bytes=VMEM_LIMIT)


def _rms(x, g):
    return x * lax.rsqrt(jnp.mean(x * x, axis=-1, keepdims=True) + RMS_EPS) * g


def _dot(a, b):
    return jnp.dot(a, b, preferred_element_type=F32)


def _dot_nt(a, b):
    return lax.dot_general(a, b, (((1,), (1,)), ((), ())), preferred_element_type=F32)


def _dot_tn(a, b):
    return lax.dot_general(a, b, (((0,), (0,)), ((), ())), preferred_element_type=F32)


def _attn_in_kernel(x_ref, gn_ref, win_ref, gqa_ref, gkva_ref, wq_ref, wqs_ref, wk_ref, wvt_ref,
                    gq_ref, gk_ref, cos_ref, sin_ref, q_ref, k_ref, vt_ref):
    x = x_ref[0]
    xn = _rms(x, gn_ref[...]).astype(BF16)
    a = _dot(xn, win_ref[...])
    cq = _rms(a[:, :Q_LORA], gqa_ref[...]).astype(BF16)
    ckv = _rms(a[:, Q_LORA:Q_LORA + KV_LORA], gkva_ref[...]).astype(BF16)
    kr = a[:, Q_LORA + KV_LORA:Q_LORA + KV_LORA + LANES]
    kr_sw = a[:, Q_LORA + KV_LORA + LANES:]
    cosv, sinv = cos_ref[...], sin_ref[...]
    gq, gk = gq_ref[...], gk_ref[...]

    q = _dot(cq, wq_ref[...])
    q_sw = _dot(cq, wqs_ref[...])
    kn = _dot(ckv, wk_ref[...])
    vt = _dot_nt(wvt_ref[...], ckv)
    vrow = lax.broadcasted_iota(I32, vt.shape, 0) % V_EXT
    vt_ref[0, 0] = jnp.where(vrow == V_DIM, 1.0, vt).astype(BF16)

    ssq_rope = jnp.sum(kr * kr, axis=-1, keepdims=True)
    k_rope = kr * (gk[1:2] * cosv) + kr_sw * (gk[2:3] * sinv)
    q_cos, q_sin = gq[0:1] * cosv, gq[1:2] * sinv
    inv_d = 1.0 / QK_DIM
    bound_lane = lax.broadcasted_iota(I32, (x.shape[0], LANES), 1) == QK_DIM
    for h in range(N_HEADS):
        sl = slice(h * LANES, (h + 1) * LANES)
        qh = q[:, sl]
        rq = lax.rsqrt(jnp.sum(qh * qh, axis=-1, keepdims=True) * inv_d + RMS_EPS)
        qt = (qh * q_cos + q_sw[:, sl] * q_sin) * rq
        bound = jnp.sqrt(jnp.sum(qt * qt, axis=-1, keepdims=True)) * gq[2:3]
        q_ref[0, h] = jnp.where(bound_lane, -bound, qt).astype(BF16)
        kh = kn[:, sl]
        rk = lax.rsqrt((jnp.sum(kh * kh, axis=-1, keepdims=True) + ssq_rope) * inv_d + RMS_EPS)
        k_ref[0, h] = jnp.where(bound_lane, 1.0, (kh * gk[0:1] + k_rope) * rk).astype(BF16)


def _attn_in(x, w, cos_t, sin_t):
    B, S, _ = x.shape
    T = T_ATTN_IN
    nS = S // T
    const = lambda shape: pl.BlockSpec(shape, lambda b, j: (0,) * len(shape))
    return pl.pallas_call(
        _attn_in_kernel,
        grid=(B, nS),
        in_specs=[
            pl.BlockSpec((1, T, D_MODEL), lambda b, j: (b, j, 0)),
            const((1, D_MODEL)), const((D_MODEL, W_IN_EXT)), const((1, Q_LORA)), const((1, KV_LORA)),
            const((Q_LORA, HEAD_PAD)), const((Q_LORA, HEAD_PAD)), const((KV_LORA, HEAD_PAD)),
            const((N_HEADS * V_EXT, KV_LORA)), const((SUBLANES, LANES)), const((SUBLANES, LANES)),
            pl.BlockSpec((T, LANES), lambda b, j: (j, 0)),
            pl.BlockSpec((T, LANES), lambda b, j: (j, 0)),
        ],
        out_specs=[
            pl.BlockSpec((1, N_HEADS, T, LANES), lambda b, j: (b, 0, j, 0)),
            pl.BlockSpec((1, N_HEADS, T, LANES), lambda b, j: (b, 0, j, 0)),
            pl.BlockSpec((1, 1, N_HEADS * V_EXT, T), lambda b, j: (b, j, 0, 0)),
        ],
        out_shape=[
            jax.ShapeDtypeStruct((B, N_HEADS, S, LANES), BF16),
            jax.ShapeDtypeStruct((B, N_HEADS, S, LANES), BF16),
            jax.ShapeDtypeStruct((B, nS, N_HEADS * V_EXT, T), BF16),
        ],
        compiler_params=_cparams(("parallel", "parallel")),
        name="attn_in",
    )(x, w["attn_norm"], w["w_in_ext"], w["g_qa"], w["g_kva"], w["wq"], w["wq_sw"], w["wk"], w["wvt"],
      w["gq"], w["gk"], cos_t, sin_t)


def _flash_kernel(q_ref, k_ref, vt_ref, o_ref, m_ref, acc_ref, s_ref, cm_ref, *, n_kv, t_kv, unroll):
    q = q_ref[0, 0]

    def scores(j):
        kb = k_ref[0, 0, pl.ds(pl.multiple_of(j * t_kv, t_kv), t_kv), :]
        return _dot_nt(kb, q)

    def finish():
        o_ref[0] = (acc_ref[0:V_DIM] * (1.0 / acc_ref[V_DIM:V_DIM + 1])).astype(BF16)

    s_ref[0] = scores(0)
    s_ref[1] = scores(min(1, n_kv - 1))
    acc_ref[...] = jnp.zeros(acc_ref.shape, F32)

    def fast_body(it, carry):
        s, s_next, acc = s_ref[0], s_ref[1], acc_ref[...]
        for u in range(unroll):
            j = it * unroll + u
            s_next2 = scores(jnp.minimum(j + 2, n_kv - 1))
            acc = acc + _dot(vt_ref[0, j], jnp.exp2(s).astype(BF16))
            s, s_next = s_next, s_next2
        s_ref[0], s_ref[1], acc_ref[...] = s, s_next, acc
        return carry

    lax.fori_loop(0, n_kv // unroll, fast_body, 0)
    safe = jnp.min(acc_ref[V_DIM:V_DIM + 1]) >= MIN_DENOMINATOR

    @pl.when(safe)
    def _():
        finish()

    @pl.when(jnp.logical_not(safe))
    def _():
        m_ref[...] = jnp.full(m_ref.shape, -jnp.inf, F32)
        acc_ref[...] = jnp.zeros(acc_ref.shape, F32)
        s0 = scores(0)
        s_ref[0] = s0
        cm_ref[...] = jnp.max(s0, axis=0, keepdims=True)
        s_ref[1] = scores(min(1, n_kv - 1))

        def body(it, carry):
            s, s_next, cm = s_ref[0], s_ref[1], cm_ref[...]
            m, acc = m_ref[...], acc_ref[...]
            for u in range(unroll):
                j = it * unroll + u
                s_next2 = scores(jnp.minimum(j + 2, n_kv - 1))
                cm_next = jnp.max(s_next, axis=0, keepdims=True)
                m_new = jnp.maximum(m, cm)
                p = jnp.exp2(s - m_new)
                alpha = jnp.exp2(m - m_new)
                acc = alpha * acc + _dot(vt_ref[0, j], p.astype(BF16))
                m, s, cm, s_next = m_new, s_next, cm_next, s_next2
            s_ref[0], s_ref[1], cm_ref[...] = s, s_next, cm
            m_ref[...], acc_ref[...] = m, acc
            return carry

        lax.fori_loop(0, n_kv // unroll, body, 0)
        finish()


def _flash(q, k, vt):
    B, H, S, _ = q.shape
    t_kv = vt.shape[-1]
    n_kv = S // t_kv
    tq = min(T_Q, S)
    unroll = KV_UNROLL if n_kv % KV_UNROLL == 0 else 1
    return pl.pallas_call(
        functools.partial(_flash_kernel, n_kv=n_kv, t_kv=t_kv, unroll=unroll),
        grid=(B, H, S // tq),
        in_specs=[
            pl.BlockSpec((1, 1, tq, LANES), lambda b, h, i: (b, h, i, 0)),
            pl.BlockSpec((1, 1, S, LANES), lambda b, h, i: (b, h, 0, 0)),
            pl.BlockSpec((1, n_kv, V_EXT, t_kv), lambda b, h, i: (b, 0, h, 0)),
        ],
        out_specs=pl.BlockSpec((1, V_DIM, tq), lambda b, h, i: (b, h, i)),
        out_shape=jax.ShapeDtypeStruct((B, H * V_DIM, S), BF16),
        scratch_shapes=[pltpu.VMEM((1, tq), F32), pltpu.VMEM((V_EXT, tq), F32),
                        pltpu.VMEM((2, t_kv, tq), F32), pltpu.VMEM((1, tq), F32)],
        compiler_params=_cparams(("parallel", "parallel", "parallel")),
        name="flash",
    )(q, k, vt)


def _router(x1, gm_ref, rw_ref, rb_ref, tri_ref, carry_ref, xn_ref, ri_ref, rf_ref, cnt_ref):
    T = x1.shape[0]

    @pl.when(pl.program_id(0) == 0)
    def _():
        carry_ref[...] = jnp.zeros(carry_ref.shape, F32)

    xn = _rms(x1, gm_ref[...])
    xn_ref[...] = xn
    x_hi = xn.astype(BF16)
    x_lo = (xn - x_hi.astype(F32)).astype(BF16)
    rw = rw_ref[...]
    l_hi = _dot(x_hi, rw)
    l_lo = _dot(x_lo, rw[:, :LANES])
    logits = l_hi[:, :LANES] + l_hi[:, LANES:] + l_lo + rb_ref[...]
    lt = logits.T

    row = lax.broadcasted_iota(I32, (SUBLANES, T), 0).astype(F32)
    neg = -jnp.inf
    lg = jnp.where(row < N_GROUPS, lt[GROUP_COL0:GROUP_COL0 + SUBLANES], neg)
    mg = jnp.max(lg, axis=0, keepdims=True)
    gidx = jnp.min(jnp.where(lg == mg, row, float(SUBLANES)), axis=0, keepdims=True)
    pg_top = 1.0 / jnp.sum(jnp.exp(lg - mg), axis=0, keepdims=True)

    def grp(g):
        return lt[EXPERT_COL0 + g * EXPERTS_PER_GROUP:EXPERT_COL0 + (g + 1) * EXPERTS_PER_GROUP]

    le = jnp.where(gidx == 0.0, grp(0), jnp.where(gidx == 1.0, grp(1), jnp.where(gidx == 2.0, grp(2), grp(3))))
    m1 = jnp.max(le, axis=0, keepdims=True)
    i1 = jnp.min(jnp.where(le == m1, row, float(SUBLANES)), axis=0, keepdims=True)
    le2 = jnp.where(row == i1, neg, le)
    m2 = jnp.max(le2, axis=0, keepdims=True)
    i2 = jnp.min(jnp.where(le2 == m2, row, float(SUBLANES)), axis=0, keepdims=True)
    e21 = jnp.exp(m2 - m1)
    inv = 1.0 / (1.0 + e21)
    g1 = pg_top * inv
    g2 = pg_top * e21 * inv
    e1 = gidx * EXPERTS_PER_GROUP + i1
    e2 = gidx * EXPERTS_PER_GROUP + i2

    erow = lax.broadcasted_iota(I32, (N_EXPERTS, T), 0).astype(F32)
    oh1 = erow == e1
    oh2 = erow == e2
    oh = jnp.where(oh1, 1.0, jnp.where(oh2, 1.0, 0.0))
    before = _dot(oh.astype(BF16), tri_ref[...]) + carry_ref[...]
    r1 = jnp.sum(jnp.where(oh1, before, 0.0), axis=0, keepdims=True)
    r2 = jnp.sum(jnp.where(oh2, before, 0.0), axis=0, keepdims=True)
    carry = carry_ref[...] + jnp.sum(oh, axis=1, keepdims=True)
    carry_ref[...] = carry
    cnt_ref[...] = carry[:, :LANES].astype(I32)

    ri = jnp.where(row == 0.0, e1, jnp.where(row == 1.0, e2, jnp.where(row == 2.0, r1, jnp.where(row == 3.0, r2, 0.0))))
    ri_ref[...] = ri.astype(I32)
    rf_ref[...] = jnp.where(row == 0.0, g1, jnp.where(row == 1.0, g2, 0.0))


def _router_specs(T, n_tok):
    const = lambda shape: pl.BlockSpec(shape, lambda i: (0,) * len(shape))
    in_specs = [const((1, D_MODEL)), const((D_MODEL, 2 * LANES)), const((1, LANES)), const((T, T))]
    out_specs = [
        pl.BlockSpec((T, D_MODEL), lambda i: (i, 0)),
        pl.BlockSpec((SUBLANES, T), lambda i: (0, i)),
        pl.BlockSpec((SUBLANES, T), lambda i: (0, i)),
        const((N_EXPERTS, LANES)),
    ]
    out_shape = [
        jax.ShapeDtypeStruct((n_tok, D_MODEL), F32),
        jax.ShapeDtypeStruct((SUBLANES, n_tok), I32),
        jax.ShapeDtypeStruct((SUBLANES, n_tok), F32),
        jax.ShapeDtypeStruct((N_EXPERTS, LANES), I32),
    ]
    return in_specs, out_specs, out_shape


def _attn_out_kernel(h_ref, ot_ref, wo_ref, gm_ref, rw_ref, rb_ref, tri_ref,
                     h1_ref, xn_ref, ri_ref, rf_ref, cnt_ref, carry_ref):
    x1 = h_ref[...] + _dot_tn(ot_ref[0], wo_ref[...])
    h1_ref[...] = x1
    _router(x1, gm_ref, rw_ref, rb_ref, tri_ref, carry_ref, xn_ref, ri_ref, rf_ref, cnt_ref)


def _attn_out(h, ot, w, lw):
    n_tok = h.shape[0]
    S = ot.shape[-1]
    T = T_TOK
    nS = S // T
    const = lambda shape: pl.BlockSpec(shape, lambda i: (0,) * len(shape))
    r_in, r_out, r_shape = _router_specs(T, n_tok)
    return pl.pallas_call(
        _attn_out_kernel,
        grid=(n_tok // T,),
        in_specs=[
            pl.BlockSpec((T, D_MODEL), lambda i: (i, 0)),
            pl.BlockSpec((1, N_HEADS * V_DIM, T), lambda i: (i // nS, 0, i % nS)),
            const((N_HEADS * V_DIM, D_MODEL)),
        ] + r_in,
        out_specs=[pl.BlockSpec((T, D_MODEL), lambda i: (i, 0))] + r_out,
        out_shape=[jax.ShapeDtypeStruct((n_tok, D_MODEL), F32)] + r_shape,
        scratch_shapes=[pltpu.VMEM((N_EXPERTS, T), F32)],
        compiler_params=_cparams(("arbitrary",)),
        name="attn_out_router",
    )(h, ot, w["w_o"], lw["moe_norm"], lw["rw"], lw["rb"], w["tri"])


def _conv_kernel(h_ref, hp_ref, hn_ref, gn_ref, win_ref, cw_ref, wout_ref, gm_ref, rw_ref, rb_ref, tri_ref,
                 h1_ref, xn_ref, ri_ref, rf_ref, cnt_ref, carry_ref, *, tiles_per_seq):
    T = h_ref.shape[0]
    i = pl.program_id(0)
    x = h_ref[...]
    gn = gn_ref[...]
    bcu = _dot(_rms(x, gn).astype(BF16), win_ref[...])
    b = bcu[:, :D_MODEL]
    cu = bcu[:, D_MODEL:2 * D_MODEL] * bcu[:, 2 * D_MODEL:]
    halo = jnp.concatenate([hp_ref[...], hn_ref[...]], axis=0)
    hcu = _dot(_rms(halo, gn).astype(BF16), win_ref[:, D_MODEL:])
    hcu = hcu[:, :D_MODEL] * hcu[:, D_MODEL:]
    first = (i % tiles_per_seq) == 0
    last = (i % tiles_per_seq) == tiles_per_seq - 1
    cu_before = jnp.where(first, 0.0, hcu[SUBLANES - 1:SUBLANES])
    cu_after = jnp.where(last, 0.0, hcu[SUBLANES:SUBLANES + 1])
    row = lax.broadcasted_iota(I32, (T, 1), 0)
    prev = jnp.where(row == 0, cu_before, pltpu.roll(cu, 1, 0))
    nxt = jnp.where(row == T - 1, cu_after, pltpu.roll(cu, T - 1, 0))
    cw = cw_ref[...]
    y = cw[0:1] * prev + cw[1:2] * cu + cw[2:3] * nxt
    x1 = x + _dot((b * y).astype(BF16), wout_ref[...])
    h1_ref[...] = x1
    _router(x1, gm_ref, rw_ref, rb_ref, tri_ref, carry_ref, xn_ref, ri_ref, rf_ref, cnt_ref)


def _conv(h, S, w, lw):
    n_tok = h.shape[0]
    T = T_TOK
    nS = S // T
    rows8 = T // SUBLANES
    n8 = n_tok // SUBLANES
    const = lambda shape: pl.BlockSpec(shape, lambda i: (0,) * len(shape))
    r_in, r_out, r_shape = _router_specs(T, n_tok)
    return pl.pallas_call(
        functools.partial(_conv_kernel, tiles_per_seq=nS),
        grid=(n_tok // T,),
        in_specs=[
            pl.BlockSpec((T, D_MODEL), lambda i: (i, 0)),
            pl.BlockSpec((SUBLANES, D_MODEL), lambda i: (jnp.maximum(i * rows8 - 1, 0), 0)),
            pl.BlockSpec((SUBLANES, D_MODEL), lambda i: (jnp.minimum((i + 1) * rows8, n8 - 1), 0)),
            const((1, D_MODEL)), const((D_MODEL, 3 * D_MODEL)), const((SUBLANES, D_MODEL)),
            const((D_MODEL, D_MODEL)),
        ] + r_in,
        out_specs=[pl.BlockSpec((T, D_MODEL), lambda i: (i, 0))] + r_out,
        out_shape=[jax.ShapeDtypeStruct((n_tok, D_MODEL), F32)] + r_shape,
        scratch_shapes=[pltpu.VMEM((N_EXPERTS, T), F32)],
        compiler_params=_cparams(("arbitrary",)),
        name="conv_router",
    )(h, h, h, w["conv_norm"], w["conv_w_in"], w["conv_w"], w["conv_w_out"],
      lw["moe_norm"], lw["rw"], lw["rb"], w["tri"])


def _plan_kernel(cnt_ref, ri_ref, dest_ref, info_ref, *, n_blocks):
    ri = ri_ref[...]
    eid = ri[0:2]
    offset = jnp.zeros(eid.shape, I32)
    blocks_before = jnp.int32(0)
    for e in range(N_EXPERTS):
        nb = (cnt_ref[e, 0] + (EXPERT_BLOCK - 1)) // EXPERT_BLOCK
        offset = jnp.where(eid == e, blocks_before * EXPERT_BLOCK, offset)

        def fill(j, c, e=e, base=blocks_before):
            info_ref[base + j] = e
            return c

        lax.fori_loop(0, nb, fill, 0)
        blocks_before = blocks_before + nb

    def fill_tail(j, c):
        info_ref[j] = N_EXPERTS - 1
        return c

    lax.fori_loop(blocks_before, n_blocks, fill_tail, 0)
    info_ref[n_blocks] = blocks_before
    dest_ref[...] = jnp.zeros(dest_ref.shape, I32)
    dest_ref[0:2, :] = ri[2:4] + offset


def _plan(cnt, ri, n_blocks):
    n_tok = ri.shape[1]
    return pl.pallas_call(
        functools.partial(_plan_kernel, n_blocks=n_blocks),
        in_specs=[pl.BlockSpec(memory_space=pltpu.SMEM), pl.BlockSpec(memory_space=pltpu.VMEM)],
        out_specs=[pl.BlockSpec(memory_space=pltpu.VMEM), pl.BlockSpec(memory_space=pltpu.SMEM)],
        out_shape=[jax.ShapeDtypeStruct((SUBLANES, n_tok), I32),
                   jax.ShapeDtypeStruct((n_blocks + 1,), I32)],
        compiler_params=pltpu.CompilerParams(vmem_limit_bytes=VMEM_LIMIT),
        name="plan",
    )(cnt, ri)


def _row_copy(src_ref, s, dst_ref, d, sem):
    return pltpu.make_async_copy(src_ref.at[pl.ds(s, 1)], dst_ref.at[pl.ds(d, 1)], sem)


def _dispatch_kernel(d0_ref, d1_ref, x_ref, xb_in_ref, xb_ref, sem):
    del xb_in_ref
    T = x_ref.shape[0]
    base = pl.program_id(0) * T

    def start(t, c):
        _row_copy(x_ref, t, xb_ref, d0_ref[base + t], sem.at[0]).start(priority=0)
        _row_copy(x_ref, t, xb_ref, d1_ref[base + t], sem.at[1]).start(priority=1)
        return c

    lax.fori_loop(0, T, start, 0, unroll=8)

    def wait(t, c):
        _row_copy(x_ref, t, xb_ref, d0_ref[base + t], sem.at[0]).wait()
        _row_copy(x_ref, t, xb_ref, d1_ref[base + t], sem.at[1]).wait()
        return c

    lax.fori_loop(0, T, wait, 0, unroll=8)


def _dispatch(xn, d0, d1, n_rows):
    n_tok = xn.shape[0]
    T = T_TOK
    xb0 = jnp.zeros((n_rows, D_MODEL), F32)
    return pl.pallas_call(
        _dispatch_kernel,
        grid_spec=pltpu.PrefetchScalarGridSpec(
            num_scalar_prefetch=2,
            grid=(n_tok // T,),
            in_specs=[pl.BlockSpec((T, D_MODEL), lambda i, d0, d1: (i, 0)),
                      pl.BlockSpec(memory_space=pl.ANY)],
            out_specs=pl.BlockSpec(memory_space=pl.ANY),
            scratch_shapes=[pltpu.SemaphoreType.DMA((2,))],
        ),
        out_shape=jax.ShapeDtypeStruct((n_rows, D_MODEL), F32),
        input_output_aliases={3: 0},
        compiler_params=_cparams(("arbitrary",)),
        name="dispatch",
    )(d0, d1, xn, xb0)


def _expert_kernel(info_ref, xb_ref, *refs, n_blocks):
    yb_ref = refs[-1]
    i = pl.program_id(0)
    for u in range(BLOCKS_PER_STEP):
        wg_ref, wu_ref, wd_ref = refs[3 * u:3 * u + 3]
        blk = i * BLOCKS_PER_STEP + u
        rows = pl.ds(u * EXPERT_BLOCK, EXPERT_BLOCK)

        @pl.when(blk < info_ref[n_blocks])
        def _():
            x = xb_ref[rows, :].astype(BF16)
            g = _dot(x, wg_ref[0])
            up = _dot(x, wu_ref[0])
            hmid = (g * jax.nn.sigmoid(g) * up).astype(BF16)
            yb_ref[rows, :] = _dot(hmid, wd_ref[0])

        @pl.when(blk >= info_ref[n_blocks])
        def _():
            yb_ref[rows, :] = jnp.zeros((EXPERT_BLOCK, D_MODEL), F32)


def _experts(info, xb, lw):
    n_rows = xb.shape[0]
    n_blocks = n_rows // EXPERT_BLOCK
    assert n_blocks % BLOCKS_PER_STEP == 0
    step_rows = BLOCKS_PER_STEP * EXPERT_BLOCK
    w_specs, w_args = [], []
    for u in range(BLOCKS_PER_STEP):
        pick = lambda i, info, u=u: (info[i * BLOCKS_PER_STEP + u], 0, 0)
        w_specs += [pl.BlockSpec((1, D_MODEL, D_EXPERT), pick), pl.BlockSpec((1, D_MODEL, D_EXPERT), pick),
                    pl.BlockSpec((1, D_EXPERT, D_MODEL), pick)]
        w_args += [lw["w_gate"], lw["w_up"], lw["w_down"]]
    return pl.pallas_call(
        functools.partial(_expert_kernel, n_blocks=n_blocks),
        grid_spec=pltpu.PrefetchScalarGridSpec(
            num_scalar_prefetch=1,
            grid=(n_blocks // BLOCKS_PER_STEP,),
            in_specs=[pl.BlockSpec((step_rows, D_MODEL), lambda i, info: (i, 0))] + w_specs,
            out_specs=pl.BlockSpec((step_rows, D_MODEL), lambda i, info: (i, 0)),
        ),
        out_shape=jax.ShapeDtypeStruct((n_rows, D_MODEL), F32),
        compiler_params=_cparams(("arbitrary",)),
        name="experts",
    )(info, xb, *w_args)


def _combine_kernel(d0_ref, d1_ref, h_ref, rf_ref, p_ref, yb_ref, gn_ref, wg_ref, wp_ref, o_ref,
                    y0_ref, y1_ref, sem):
    T = h_ref.shape[0]
    base = pl.program_id(0) * T

    def start(t, c):
        _row_copy(yb_ref, d0_ref[base + t], y0_ref, t, sem.at[0]).start(priority=0)
        _row_copy(yb_ref, d1_ref[base + t], y1_ref, t, sem.at[1]).start(priority=1)
        return c

    lax.fori_loop(0, T, start, 0, unroll=8)

    proj = _dot(p_ref[...].astype(BF16), wp_ref[...])
    gates = jnp.concatenate([rf_ref[...], jnp.zeros((LANES - SUBLANES, T), F32)], axis=0).T

    def wait(t, c):
        _row_copy(yb_ref, d0_ref[base + t], y0_ref, t, sem.at[0]).wait()
        _row_copy(yb_ref, d1_ref[base + t], y1_ref, t, sem.at[1]).wait()
        return c

    lax.fori_loop(0, T, wait, 0, unroll=8)

    h2 = h_ref[...] + gates[:, 0:1] * y0_ref[...] + gates[:, 1:2] * y1_ref[...]
    gate = jax.nn.sigmoid(_dot(_rms(h2, gn_ref[...]).astype(BF16), wg_ref[...]))
    o_ref[...] = h2 + gate * proj


def _combine(h, rf, p, yb, d0, d1, lw):
    n_tok = h.shape[0]
    T = T_TOK
    const = lambda shape: pl.BlockSpec(shape, lambda i, d0, d1: (0,) * len(shape))
    return pl.pallas_call(
        _combine_kernel,
        grid_spec=pltpu.PrefetchScalarGridSpec(
            num_scalar_prefetch=2,
            grid=(n_tok // T,),
            in_specs=[
                pl.BlockSpec((T, D_MODEL), lambda i, d0, d1: (i, 0)),
                pl.BlockSpec((SUBLANES, T), lambda i, d0, d1: (0, i)),
                pl.BlockSpec((T, PLE_DIM), lambda i, d0, d1: (i, 0)),
                pl.BlockSpec(memory_space=pl.ANY),
                const((1, D_MODEL)), const((D_MODEL, D_MODEL)), const((PLE_DIM, D_MODEL)),
            ],
            out_specs=pl.BlockSpec((T, D_MODEL), lambda i, d0, d1: (i, 0)),
            scratch_shapes=[pltpu.VMEM((T, D_MODEL), F32), pltpu.VMEM((T, D_MODEL), F32),
                            pltpu.SemaphoreType.DMA((2,))],
        ),
        out_shape=jax.ShapeDtypeStruct((n_tok, D_MODEL), F32),
        compiler_params=_cparams(("arbitrary",)),
        name="combine_ple",
    )(d0, d1, h, rf, p, yb, lw["ple_norm"], lw["ple_w_gate"], lw["ple_w_proj"])


def _pad_lanes(x, lo, width=LANES):
    n = x.shape[-1]
    pad = [(0, 0)] * (x.ndim - 1) + [(lo, width - lo - n)]
    return jnp.pad(x, pad)


def _rotate_half_cols(x):
    half = QK_ROPE // 2
    return jnp.concatenate([-x[..., half:], x[..., :half]], axis=-1)


def _swap_halves(g):
    half = QK_ROPE // 2
    return jnp.concatenate([g[..., half:], g[..., :half]], axis=-1)


def _prepare(attn_norm, attn_w_in, attn_q_a_norm, attn_kv_a_norm, attn_w_qb, attn_w_kvb, attn_q_gain,
             attn_k_gain, attn_w_o, conv_norm, conv_w_in, conv_w, conv_w_out, moe_norm, router_group_w,
             router_group_b, router_expert_w, router_expert_b, expert_w_gate, expert_w_up, expert_w_down,
             ple_norm, ple_w_gate, ple_w_proj):
    w = {}
    w_in = attn_w_in[0]
    rope_cols = w_in[:, Q_LORA + KV_LORA:]
    w["w_in_ext"] = jnp.concatenate(
        [w_in[:, :Q_LORA + KV_LORA], _pad_lanes(rope_cols, QK_NOPE), _pad_lanes(_rotate_half_cols(rope_cols), QK_NOPE)],
        axis=1).astype(BF16)
    w["attn_norm"] = attn_norm[0][None]
    w["g_qa"] = attn_q_a_norm[0][None]
    w["g_kva"] = attn_kv_a_norm[0][None]
    wqb = attn_w_qb[0].reshape(Q_LORA, N_HEADS, QK_DIM)
    w["wq"] = _pad_lanes(wqb, 0).reshape(Q_LORA, HEAD_PAD).astype(BF16)
    w["wq_sw"] = _pad_lanes(_rotate_half_cols(wqb[..., QK_NOPE:]), QK_NOPE).reshape(Q_LORA, HEAD_PAD).astype(BF16)
    wkvb = attn_w_kvb[0].reshape(KV_LORA, N_HEADS, QK_NOPE + V_DIM)
    w["wk"] = _pad_lanes(wkvb[..., :QK_NOPE], 0).reshape(KV_LORA, HEAD_PAD).astype(BF16)
    w["wvt"] = _pad_lanes(wkvb[..., QK_NOPE:], 0, V_EXT).reshape(KV_LORA, N_HEADS * V_EXT).T.astype(BF16)
    scale = LOG2_E / float(QK_DIM) ** 0.5
    gqs = attn_q_gain[0] * scale
    gk = attn_k_gain[0]
    zero = jnp.zeros((SUBLANES - 2, LANES), F32)
    key_norm_bound = jnp.full((1, LANES), BOUND_MARGIN * float(QK_DIM) ** 0.5, F32) * jnp.max(jnp.abs(gk))
    w["gq"] = jnp.concatenate([_pad_lanes(gqs, 0)[None], _pad_lanes(_swap_halves(gqs[QK_NOPE:]), QK_NOPE)[None],
                               key_norm_bound, zero[1:]])
    w["gk"] = jnp.concatenate([_pad_lanes(gk[:QK_NOPE], 0)[None], _pad_lanes(gk[QK_NOPE:], QK_NOPE)[None],
                               _pad_lanes(_swap_halves(gk[QK_NOPE:]), QK_NOPE)[None], zero[1:]])
    w["w_o"] = attn_w_o[0].astype(BF16)
    w["conv_norm"] = conv_norm[0][None]
    w["conv_w_in"] = conv_w_in[0].astype(BF16)
    w["conv_w"] = jnp.pad(conv_w[0], ((0, SUBLANES - conv_w.shape[1]), (0, 0)))
    w["conv_w_out"] = conv_w_out[0].astype(BF16)
    tri = lax.broadcasted_iota(I32, (T_TOK, T_TOK), 0) < lax.broadcasted_iota(I32, (T_TOK, T_TOK), 1)
    w["tri"] = tri.astype(BF16)
    layers = []
    for i in range(moe_norm.shape[0]):
        rw = jnp.zeros((D_MODEL, LANES), F32)
        rw = rw.at[:, GROUP_COL0:GROUP_COL0 + N_GROUPS].set(router_group_w[i])
        rw = rw.at[:, EXPERT_COL0:EXPERT_COL0 + N_EXPERTS].set(router_expert_w[i])
        rw_hi = rw.astype(BF16)
        rw_lo = (rw - rw_hi.astype(F32)).astype(BF16)
        rb = jnp.zeros((1, LANES), F32)
        rb = rb.at[0, GROUP_COL0:GROUP_COL0 + N_GROUPS].set(router_group_b[i])
        rb = rb.at[0, EXPERT_COL0:EXPERT_COL0 + N_EXPERTS].set(router_expert_b[i])
        layers.append(dict(
            moe_norm=moe_norm[i][None], rw=jnp.concatenate([rw_hi, rw_lo], axis=1), rb=rb,
            w_gate=expert_w_gate[i].astype(BF16), w_up=expert_w_up[i].astype(BF16),
            w_down=expert_w_down[i].astype(BF16),
            ple_norm=ple_norm[i][None], ple_w_gate=ple_w_gate[i].astype(BF16),
            ple_w_proj=ple_w_proj[i].astype(BF16)))
    return w, layers


def _rope_tiles(S):
    pos = jnp.arange(S, dtype=F32)
    inv = ROPE_THETA ** (-jnp.arange(0, QK_ROPE, 2, dtype=F32) / QK_ROPE)
    ang = pos[:, None] * inv[None, :]
    cos, sin = jnp.cos(ang), jnp.sin(ang)
    cos_t = jnp.concatenate([jnp.ones((S, QK_NOPE), F32), cos, cos, jnp.zeros((S, LANES - QK_DIM), F32)], axis=1)
    sin_t = jnp.concatenate([jnp.zeros((S, QK_NOPE), F32), sin, sin, jnp.zeros((S, LANES - QK_DIM), F32)], axis=1)
    return cos_t, sin_t


def _moe_ple(h1, xn, ri, rf, cnt, p, lw):
    n_tok = h1.shape[0]
    n_assign = 2 * n_tok
    n_rows = -(-n_assign // EXPERT_BLOCK) * EXPERT_BLOCK + N_EXPERTS * EXPERT_BLOCK
    dest, info = _plan(cnt, ri, n_rows // EXPERT_BLOCK)
    d0, d1 = dest[0], dest[1]
    xb = _dispatch(xn, d0, d1, n_rows)
    yb = _experts(info, xb, lw)
    return _combine(h1, rf, p, yb, d0, d1, lw)


def _trunk(x, p, w, layers):
    B, S, _ = x.shape
    n_tok = B * S
    cos_t, sin_t = _rope_tiles(S)
    h = x.reshape(n_tok, D_MODEL)
    p = p.reshape(p.shape[0], n_tok, PLE_DIM)
    q, k, vt = _attn_in(x, w, cos_t, sin_t)
    ot = _flash(q, k, vt)
    h1, xn, ri, rf, cnt = _attn_out(h, ot, w, layers[0])
    h = _moe_ple(h1, xn, ri, rf, cnt, p[0], layers[0])
    h1, xn, ri, rf, cnt = _conv(h, S, w, layers[1])
    h = _moe_ple(h1, xn, ri, rf, cnt, p[1], layers[1])
    return h.reshape(B, S, D_MODEL)


def kernel(x_prompt, x_sample, p_prompt, p_sample, attn_norm, attn_w_in, attn_q_a_norm, attn_kv_a_norm, attn_w_qb, attn_w_kvb, attn_q_gain, attn_k_gain, attn_w_o, conv_norm, conv_w_in, conv_w, conv_w_out, moe_norm, router_group_w, router_group_b, router_expert_w, router_expert_b, expert_w_gate, expert_w_up, expert_w_down, ple_norm, ple_w_gate, ple_w_proj):
    assert x_prompt.shape[-1] == D_MODEL and moe_norm.shape[0] == 2
    assert attn_w_in.shape == (1, D_MODEL, Q_LORA + KV_LORA + QK_ROPE)
    assert attn_w_qb.shape == (1, Q_LORA, N_HEADS * QK_DIM)
    assert attn_w_kvb.shape == (1, KV_LORA, N_HEADS * (QK_NOPE + V_DIM))
    assert expert_w_gate.shape[1:] == (N_EXPERTS, D_MODEL, D_EXPERT)
    w, layers = _prepare(attn_norm, attn_w_in, attn_q_a_norm, attn_kv_a_norm, attn_w_qb, attn_w_kvb,
                         attn_q_gain, attn_k_gain, attn_w_o, conv_norm, conv_w_in, conv_w, conv_w_out,
                         moe_norm, router_group_w, router_group_b, router_expert_w, router_expert_b,
                         expert_w_gate, expert_w_up, expert_w_down, ple_norm, ple_w_gate, ple_w_proj)
    return (_trunk(x_prompt, p_prompt, w, layers), _trunk(x_sample, p_sample, w, layers))
```

```python
import functools

import jax
import jax.numpy as jnp
from jax import lax
from jax.experimental import pallas as pl
from jax.experimental.pallas import tpu as pltpu

F32, BF16, I32 = jnp.float32, jnp.bfloat16, jnp.int32

D_MODEL = 1024
N_HEADS = 16
QK_NOPE, QK_ROPE, V_DIM = 64, 32, 64
QK_DIM = QK_NOPE + QK_ROPE
Q_LORA, KV_LORA = 384, 256
N_GROUPS, EXPERTS_PER_GROUP = 4, 8
N_EXPERTS = N_GROUPS * EXPERTS_PER_GROUP
D_EXPERT = 512
PLE_DIM = 256
ROPE_THETA = 10000.0
RMS_EPS = 1e-6
EXPERT_BLOCK = 128

LANES = 128
SUBLANES = 8
VMEM_LIMIT = 56 * 1024 * 1024

T_ATTN_IN = 256
T_Q = 512
T_TOK = 256
KV_MERGE = 2
KV_UNROLL = 8
BLOCKS_PER_STEP = 4
LOG2_E = 1.4426950408889634
BOUND_MARGIN = 1.02
MIN_DENOMINATOR = 2.0 ** -60

BF16_ROWS = 16
V_EXT = V_DIM + BF16_ROWS
W_IN_EXT = Q_LORA + KV_LORA + 2 * LANES
HEAD_PAD = N_HEADS * LANES
GROUP_COL0, EXPERT_COL0 = 0, SUBLANES


def _cparams(sem):
    return pltpu.CompilerParams(dimension_semantics=sem, vmem_limit_bytes=VMEM_LIMIT)


def _rms(x, g):
    return x * lax.rsqrt(jnp.mean(x * x, axis=-1, keepdims=True) + RMS_EPS) * g


def _dot(a, b):
    return jnp.dot(a, b, preferred_element_type=F32)


def _dot_nt(a, b):
    return lax.dot_general(a, b, (((1,), (1,)), ((), ())), preferred_element_type=F32)


def _dot_tn(a, b):
    return lax.dot_general(a, b, (((0,), (0,)), ((), ())), preferred_element_type=F32)


def _attn_in_kernel(x_ref, gn_ref, win_ref, gqa_ref, gkva_ref, wq_ref, wqs_ref, wk_ref, wvt_ref,
                    gq_ref, gk_ref, cos_ref, sin_ref, q_ref, k_ref, vt_ref):
    x = x_ref[0]
    xn = _rms(x, gn_ref[...]).astype(BF16)
    a = _dot(xn, win_ref[...])
    cq = _rms(a[:, :Q_LORA], gqa_ref[...]).astype(BF16)
    ckv = _rms(a[:, Q_LORA:Q_LORA + KV_LORA], gkva_ref[...]).astype(BF16)
    kr = a[:, Q_LORA + KV_LORA:Q_LORA + KV_LORA + LANES]
    kr_sw = a[:, Q_LORA + KV_LORA + LANES:]
    cosv, sinv = cos_ref[...], sin_ref[...]
    gq, gk = gq_ref[...], gk_ref[...]

    q = _dot(cq, wq_ref[...])
    q_sw = _dot(cq, wqs_ref[...])
    kn = _dot(ckv, wk_ref[...])
    vt = _dot_nt(wvt_ref[...], ckv)
    vrow = lax.broadcasted_iota(I32, vt.shape, 0) % V_EXT
    vt_ref[0, 0] = jnp.where(vrow == V_DIM, 1.0, vt).astype(BF16)

    ssq_rope = jnp.sum(kr * kr, axis=-1, keepdims=True)
    k_rope = kr * (gk[1:2] * cosv) + kr_sw * (gk[2:3] * sinv)
    q_cos, q_sin = gq[0:1] * cosv, gq[1:2] * sinv
    inv_d = 1.0 / QK_DIM
    bound_lane = lax.broadcasted_iota(I32, (x.shape[0], LANES), 1) == QK_DIM
    for h in range(N_HEADS):
        sl = slice(h * LANES, (h + 1) * LANES)
        qh = q[:, sl]
        rq = lax.rsqrt(jnp.sum(qh * qh, axis=-1, keepdims=True) * inv_d + RMS_EPS)
        qt = (qh * q_cos + q_sw[:, sl] * q_sin) * rq
        bound = jnp.sqrt(jnp.sum(qt * qt, axis=-1, keepdims=True)) * gq[2:3]
        q_ref[0, h] = jnp.where(bound_lane, -bound, qt).astype(BF16)
        kh = kn[:, sl]
        rk = lax.rsqrt((jnp.sum(kh * kh, axis=-1, keepdims=True) + ssq_rope) * inv_d + RMS_EPS)
        k_ref[0, h] = jnp.where(bound_lane, 1.0, (kh * gk[0:1] + k_rope) * rk).astype(BF16)


def _attn_in(x, w, cos_t, sin_t):
    B, S, _ = x.shape
    T = T_ATTN_IN
    nS = S // T
    const = lambda shape: pl.BlockSpec(shape, lambda b, j: (0,) * len(shape))
    return pl.pallas_call(
        _attn_in_kernel,
        grid=(B, nS),
        in_specs=[
            pl.BlockSpec((1, T, D_MODEL), lambda b, j: (b, j, 0)),
            const((1, D_MODEL)), const((D_MODEL, W_IN_EXT)), const((1, Q_LORA)), const((1, KV_LORA)),
            const((Q_LORA, HEAD_PAD)), const((Q_LORA, HEAD_PAD)), const((KV_LORA, HEAD_PAD)),
            const((N_HEADS * V_EXT, KV_LORA)), const((SUBLANES, LANES)), const((SUBLANES, LANES)),
            pl.BlockSpec((T, LANES), lambda b, j: (j, 0)),
            pl.BlockSpec((T, LANES), lambda b, j: (j, 0)),
        ],
        out_specs=[
            pl.BlockSpec((1, N_HEADS, T, LANES), lambda b, j: (b, 0, j, 0)),
            pl.BlockSpec((1, N_HEADS, T, LANES), lambda b, j: (b, 0, j, 0)),
            pl.BlockSpec((1, 1, N_HEADS * V_EXT, T), lambda b, j: (b, j, 0, 0)),
        ],
        out_shape=[
            jax.ShapeDtypeStruct((B, N_HEADS, S, LANES), BF16),
            jax.ShapeDtypeStruct((B, N_HEADS, S, LANES), BF16),
            jax.ShapeDtypeStruct((B, nS, N_HEADS * V_EXT, T), BF16),
        ],
        compiler_params=_cparams(("parallel", "parallel")),
        name="attn_in",
    )(x, w["attn_norm"], w["w_in_ext"], w["g_qa"], w["g_kva"], w["wq"], w["wq_sw"], w["wk"], w["wvt"],
      w["gq"], w["gk"], cos_t, sin_t)


def _flash_kernel(q_ref, k_ref, vt_ref, o_ref, m_ref, acc_ref, s_ref, cm_ref, *, n_kv, t_kv, merge, unroll):
    q = q_ref[0, 0]
    t_blk = merge * t_kv
    n_blk = n_kv // merge

    def scores(j):
        kb = k_ref[0, 0, pl.ds(pl.multiple_of(j * t_blk, t_blk), t_blk), :]
        return _dot_nt(kb, q)

    def pv(j, p):
        out = _dot(vt_ref[0, j * merge], p[0:t_kv])
        for i in range(1, merge):
            out = out + _dot(vt_ref[0, j * merge + i], p[i * t_kv:(i + 1) * t_kv])
        return out

    def finish():
        o_ref[0] = (acc_ref[0:V_DIM] * (1.0 / acc_ref[V_DIM:V_DIM + 1])).astype(BF16)

    s_ref[0] = scores(0)
    s_ref[1] = scores(min(1, n_blk - 1))
    acc_ref[...] = jnp.zeros(acc_ref.shape, F32)

    def fast_body(it, carry):
        s, s_next, acc = s_ref[0], s_ref[1], acc_ref[...]
        for u in range(unroll):
            j = it * unroll + u
            s_next2 = scores(jnp.minimum(j + 2, n_blk - 1))
            acc = acc + pv(j, jnp.exp2(s).astype(BF16))
            s, s_next = s_next, s_next2
        s_ref[0], s_ref[1], acc_ref[...] = s, s_next, acc
        return carry

    lax.fori_loop(0, n_blk // unroll, fast_body, 0)
    safe = jnp.min(acc_ref[V_DIM:V_DIM + 1]) >= MIN_DENOMINATOR

    @pl.when(safe)
    def _():
        finish()

    @pl.when(jnp.logical_not(safe))
    def _():
        m_ref[...] = jnp.full(m_ref.shape, -jnp.inf, F32)
        acc_ref[...] = jnp.zeros(acc_ref.shape, F32)
        s0 = scores(0)
        s_ref[0] = s0
        cm_ref[...] = jnp.max(s0, axis=0, keepdims=True)
        s_ref[1] = scores(min(1, n_blk - 1))

        def body(it, carry):
            s, s_next, cm = s_ref[0], s_ref[1], cm_ref[...]
            m, acc = m_ref[...], acc_ref[...]
            for u in range(unroll):
                j = it * unroll + u
                s_next2 = scores(jnp.minimum(j + 2, n_blk - 1))
                cm_next = jnp.max(s_next, axis=0, keepdims=True)
                m_new = jnp.maximum(m, cm)
                p = jnp.exp2(s - m_new)
                alpha = jnp.exp2(m - m_new)
                acc = alpha * acc + pv(j, p.astype(BF16))
                m, s, cm, s_next = m_new, s_next, cm_next, s_next2
            s_ref[0], s_ref[1], cm_ref[...] = s, s_next, cm
            m_ref[...], acc_ref[...] = m, acc
            return carry

        lax.fori_loop(0, n_blk // unroll, body, 0)
        finish()


def _flash(q, k, vt):
    B, H, S, _ = q.shape
    t_kv = vt.shape[-1]
    n_kv = S // t_kv
    tq = min(T_Q, S)
    merge = KV_MERGE if n_kv % KV_MERGE == 0 else 1
    n_blk = n_kv // merge
    unroll = max(u for u in (KV_UNROLL, 4, 2, 1) if n_blk % u == 0)
    return pl.pallas_call(
        functools.partial(_flash_kernel, n_kv=n_kv, t_kv=t_kv, merge=merge, unroll=unroll),
        grid=(B, H, S // tq),
        in_specs=[
            pl.BlockSpec((1, 1, tq, LANES), lambda b, h, i: (b, h, i, 0)),
            pl.BlockSpec((1, 1, S, LANES), lambda b, h, i: (b, h, 0, 0)),
            pl.BlockSpec((1, n_kv, V_EXT, t_kv), lambda b, h, i: (b, 0, h, 0)),
        ],
        out_specs=pl.BlockSpec((1, V_DIM, tq), lambda b, h, i: (b, h, i)),
        out_shape=jax.ShapeDtypeStruct((B, H * V_DIM, S), BF16),
        scratch_shapes=[pltpu.VMEM((1, tq), F32), pltpu.VMEM((V_EXT, tq), F32),
                        pltpu.VMEM((2, merge * t_kv, tq), F32), pltpu.VMEM((1, tq), F32)],
        compiler_params=_cparams(("parallel", "parallel", "parallel")),
        name="flash",
    )(q, k, vt)


def _router(x1, gm_ref, rw_ref, rb_ref, tri_ref, carry_ref, xn_ref, ri_ref, rf_ref, cnt_ref):
    T = x1.shape[0]

    @pl.when(pl.program_id(0) == 0)
    def _():
        carry_ref[...] = jnp.zeros(carry_ref.shape, F32)

    xn = _rms(x1, gm_ref[...])
    xn_ref[...] = xn
    x_hi = xn.astype(BF16)
    x_lo = (xn - x_hi.astype(F32)).astype(BF16)
    rw = rw_ref[...]
    l_hi = _dot(x_hi, rw)
    l_lo = _dot(x_lo, rw[:, :LANES])
    logits = l_hi[:, :LANES] + l_hi[:, LANES:] + l_lo + rb_ref[...]
    lt = logits.T

    row = lax.broadcasted_iota(I32, (SUBLANES, T), 0).astype(F32)
    neg = -jnp.inf
    lg = jnp.where(row < N_GROUPS, lt[GROUP_COL0:GROUP_COL0 + SUBLANES], neg)
    mg = jnp.max(lg, axis=0, keepdims=True)
    gidx = jnp.min(jnp.where(lg == mg, row, float(SUBLANES)), axis=0, keepdims=True)
    pg_top = 1.0 / jnp.sum(jnp.exp(lg - mg), axis=0, keepdims=True)

    def grp(g):
        return lt[EXPERT_COL0 + g * EXPERTS_PER_GROUP:EXPERT_COL0 + (g + 1) * EXPERTS_PER_GROUP]

    le = jnp.where(gidx == 0.0, grp(0), jnp.where(gidx == 1.0, grp(1), jnp.where(gidx == 2.0, grp(2), grp(3))))
    m1 = jnp.max(le, axis=0, keepdims=True)
    i1 = jnp.min(jnp.where(le == m1, row, float(SUBLANES)), axis=0, keepdims=True)
    le2 = jnp.where(row == i1, neg, le)
    m2 = jnp.max(le2, axis=0, keepdims=True)
    i2 = jnp.min(jnp.where(le2 == m2, row, float(SUBLANES)), axis=0, keepdims=True)
    e21 = jnp.exp(m2 - m1)
    inv = 1.0 / (1.0 + e21)
    g1 = pg_top * inv
    g2 = pg_top * e21 * inv
    e1 = gidx * EXPERTS_PER_GROUP + i1
    e2 = gidx * EXPERTS_PER_GROUP + i2

    erow = lax.broadcasted_iota(I32, (N_EXPERTS, T), 0).astype(F32)
    oh1 = erow == e1
    oh2 = erow == e2
    oh = jnp.where(oh1, 1.0, jnp.where(oh2, 1.0, 0.0))
    before = _dot(oh.astype(BF16), tri_ref[...]) + carry_ref[...]
    r1 = jnp.sum(jnp.where(oh1, before, 0.0), axis=0, keepdims=True)
    r2 = jnp.sum(jnp.where(oh2, before, 0.0), axis=0, keepdims=True)
    carry = carry_ref[...] + jnp.sum(oh, axis=1, keepdims=True)
    carry_ref[...] = carry
    cnt_ref[...] = carry[:, :LANES].astype(I32)

    ri = jnp.where(row == 0.0, e1, jnp.where(row == 1.0, e2, jnp.where(row == 2.0, r1, jnp.where(row == 3.0, r2, 0.0))))
    ri_ref[...] = ri.astype(I32)
    rf_ref[...] = jnp.where(row == 0.0, g1, jnp.where(row == 1.0, g2, 0.0))


def _router_specs(T, n_tok):
    const = lambda shape: pl.BlockSpec(shape, lambda i: (0,) * len(shape))
    in_specs = [const((1, D_MODEL)), const((D_MODEL, 2 * LANES)), const((1, LANES)), const((T, T))]
    out_specs = [
        pl.BlockSpec((T, D_MODEL), lambda i: (i, 0)),
        pl.BlockSpec((SUBLANES, T), lambda i: (0, i)),
        pl.BlockSpec((SUBLANES, T), lambda i: (0, i)),
        const((N_EXPERTS, LANES)),
    ]
    out_shape = [
        jax.ShapeDtypeStruct((n_tok, D_MODEL), F32),
        jax.ShapeDtypeStruct((SUBLANES, n_tok), I32),
        jax.ShapeDtypeStruct((SUBLANES, n_tok), F32),
        jax.ShapeDtypeStruct((N_EXPERTS, LANES), I32),
    ]
    return in_specs, out_specs, out_shape


def _attn_out_kernel(h_ref, ot_ref, wo_ref, gm_ref, rw_ref, rb_ref, tri_ref,
                     h1_ref, xn_ref, ri_ref, rf_ref, cnt_ref, carry_ref):
    x1 = h_ref[...] + _dot_tn(ot_ref[0], wo_ref[...])
    h1_ref[...] = x1
    _router(x1, gm_ref, rw_ref, rb_ref, tri_ref, carry_ref, xn_ref, ri_ref, rf_ref, cnt_ref)


def _attn_out(h, ot, w, lw):
    n_tok = h.shape[0]
    S = ot.shape[-1]
    T = T_TOK
    nS = S // T
    const = lambda shape: pl.BlockSpec(shape, lambda i: (0,) * len(shape))
    r_in, r_out, r_shape = _router_specs(T, n_tok)
    return pl.pallas_call(
        _attn_out_kernel,
        grid=(n_tok // T,),
        in_specs=[
            pl.BlockSpec((T, D_MODEL), lambda i: (i, 0)),
            pl.BlockSpec((1, N_HEADS * V_DIM, T), lambda i: (i // nS, 0, i % nS)),
            const((N_HEADS * V_DIM, D_MODEL)),
        ] + r_in,
        out_specs=[pl.BlockSpec((T, D_MODEL), lambda i: (i, 0))] + r_out,
        out_shape=[jax.ShapeDtypeStruct((n_tok, D_MODEL), F32)] + r_shape,
        scratch_shapes=[pltpu.VMEM((N_EXPERTS, T), F32)],
        compiler_params=_cparams(("arbitrary",)),
        name="attn_out_router",
    )(h, ot, w["w_o"], lw["moe_norm"], lw["rw"], lw["rb"], w["tri"])


def _conv_kernel(h_ref, hp_ref, hn_ref, gn_ref, win_ref, cw_ref, wout_ref, gm_ref, rw_ref, rb_ref, tri_ref,
                 h1_ref, xn_ref, ri_ref, rf_ref, cnt_ref, carry_ref, *, tiles_per_seq):
    T = h_ref.shape[0]
    i = pl.program_id(0)
    x = h_ref[...]
    gn = gn_ref[...]
    bcu = _dot(_rms(x, gn).astype(BF16), win_ref[...])
    b = bcu[:, :D_MODEL]
    cu = bcu[:, D_MODEL:2 * D_MODEL] * bcu[:, 2 * D_MODEL:]
    halo = jnp.concatenate([hp_ref[...], hn_ref[...]], axis=0)
    hcu = _dot(_rms(halo, gn).astype(BF16), win_ref[:, D_MODEL:])
    hcu = hcu[:, :D_MODEL] * hcu[:, D_MODEL:]
    first = (i % tiles_per_seq) == 0
    last = (i % tiles_per_seq) == tiles_per_seq - 1
    cu_before = jnp.where(first, 0.0, hcu[SUBLANES - 1:SUBLANES])
    cu_after = jnp.where(last, 0.0, hcu[SUBLANES:SUBLANES + 1])
    row = lax.broadcasted_iota(I32, (T, 1), 0)
    prev = jnp.where(row == 0, cu_before, pltpu.roll(cu, 1, 0))
    nxt = jnp.where(row == T - 1, cu_after, pltpu.roll(cu, T - 1, 0))
    cw = cw_ref[...]
    y = cw[0:1] * prev + cw[1:2] * cu + cw[2:3] * nxt
    x1 = x + _dot((b * y).astype(BF16), wout_ref[...])
    h1_ref[...] = x1
    _router(x1, gm_ref, rw_ref, rb_ref, tri_ref, carry_ref, xn_ref, ri_ref, rf_ref, cnt_ref)


def _conv(h, S, w, lw):
    n_tok = h.shape[0]
    T = T_TOK
    nS = S // T
    rows8 = T // SUBLANES
    n8 = n_tok // SUBLANES
    const = lambda shape: pl.BlockSpec(shape, lambda i: (0,) * len(shape))
    r_in, r_out, r_shape = _router_specs(T, n_tok)
    return pl.pallas_call(
        functools.partial(_conv_kernel, tiles_per_seq=nS),
        grid=(n_tok // T,),
        in_specs=[
            pl.BlockSpec((T, D_MODEL), lambda i: (i, 0)),
            pl.BlockSpec((SUBLANES, D_MODEL), lambda i: (jnp.maximum(i * rows8 - 1, 0), 0)),
            pl.BlockSpec((SUBLANES, D_MODEL), lambda i: (jnp.minimum((i + 1) * rows8, n8 - 1), 0)),
            const((1, D_MODEL)), const((D_MODEL, 3 * D_MODEL)), const((SUBLANES, D_MODEL)),
            const((D_MODEL, D_MODEL)),
        ] + r_in,
        out_specs=[pl.BlockSpec((T, D_MODEL), lambda i: (i, 0))] + r_out,
        out_shape=[jax.ShapeDtypeStruct((n_tok, D_MODEL), F32)] + r_shape,
        scratch_shapes=[pltpu.VMEM((N_EXPERTS, T), F32)],
        compiler_params=_cparams(("arbitrary",)),
        name="conv_router",
    )(h, h, h, w["conv_norm"], w["conv_w_in"], w["conv_w"], w["conv_w_out"],
      lw["moe_norm"], lw["rw"], lw["rb"], w["tri"])


def _plan_kernel(cnt_ref, ri_ref, dest_ref, info_ref, *, n_blocks):
    ri = ri_ref[...]
    eid = ri[0:2]
    offset = jnp.zeros(eid.shape, I32)
    blocks_before = jnp.int32(0)
    for e in range(N_EXPERTS):
        nb = (cnt_ref[e, 0] + (EXPERT_BLOCK - 1)) // EXPERT_BLOCK
        offset = jnp.where(eid == e, blocks_before * EXPERT_BLOCK, offset)

        def fill(j, c, e=e, base=blocks_before):
            info_ref[base + j] = e
            return c

        lax.fori_loop(0, nb, fill, 0)
        blocks_before = blocks_before + nb

    def fill_tail(j, c):
        info_ref[j] = N_EXPERTS - 1
        return c

    lax.fori_loop(blocks_before, n_blocks, fill_tail, 0)
    info_ref[n_blocks] = blocks_before
    dest_ref[...] = jnp.zeros(dest_ref.shape, I32)
    dest_ref[0:2, :] = ri[2:4] + offset


def _plan(cnt, ri, n_blocks):
    n_tok = ri.shape[1]
    return pl.pallas_call(
        functools.partial(_plan_kernel, n_blocks=n_blocks),
        in_specs=[pl.BlockSpec(memory_space=pltpu.SMEM), pl.BlockSpec(memory_space=pltpu.VMEM)],
        out_specs=[pl.BlockSpec(memory_space=pltpu.VMEM), pl.BlockSpec(memory_space=pltpu.SMEM)],
        out_shape=[jax.ShapeDtypeStruct((SUBLANES, n_tok), I32),
                   jax.ShapeDtypeStruct((n_blocks + 1,), I32)],
        compiler_params=pltpu.CompilerParams(vmem_limit_bytes=VMEM_LIMIT),
        name="plan",
    )(cnt, ri)


def _row_copy(src_ref, s, dst_ref, d, sem):
    return pltpu.make_async_copy(src_ref.at[pl.ds(s, 1)], dst_ref.at[pl.ds(d, 1)], sem)


def _dispatch_kernel(d0_ref, d1_ref, x_ref, xb_in_ref, xb_ref, sem):
    del xb_in_ref
    T = x_ref.shape[0]
    base = pl.program_id(0) * T

    def start(t, c):
        _row_copy(x_ref, t, xb_ref, d0_ref[base + t], sem.at[0]).start(priority=0)
        _row_copy(x_ref, t, xb_ref, d1_ref[base + t], sem.at[1]).start(priority=1)
        return c

    lax.fori_loop(0, T, start, 0, unroll=8)

    def wait(t, c):
        _row_copy(x_ref, t, xb_ref, d0_ref[base + t], sem.at[0]).wait()
        _row_copy(x_ref, t, xb_ref, d1_ref[base + t], sem.at[1]).wait()
        return c

    lax.fori_loop(0, T, wait, 0, unroll=8)


def _dispatch(xn, d0, d1, n_rows):
    n_tok = xn.shape[0]
    T = T_TOK
    xb0 = jnp.zeros((n_rows, D_MODEL), F32)
    return pl.pallas_call(
        _dispatch_kernel,
        grid_spec=pltpu.PrefetchScalarGridSpec(
            num_scalar_prefetch=2,
            grid=(n_tok // T,),
            in_specs=[pl.BlockSpec((T, D_MODEL), lambda i, d0, d1: (i, 0)),
                      pl.BlockSpec(memory_space=pl.ANY)],
            out_specs=pl.BlockSpec(memory_space=pl.ANY),
            scratch_shapes=[pltpu.SemaphoreType.DMA((2,))],
        ),
        out_shape=jax.ShapeDtypeStruct((n_rows, D_MODEL), F32),
        input_output_aliases={3: 0},
        compiler_params=_cparams(("arbitrary",)),
        name="dispatch",
    )(d0, d1, xn, xb0)


def _expert_kernel(info_ref, xb_ref, *refs, n_blocks):
    yb_ref = refs[-1]
    i = pl.program_id(0)
    for u in range(BLOCKS_PER_STEP):
        wg_ref, wu_ref, wd_ref = refs[3 * u:3 * u + 3]
        blk = i * BLOCKS_PER_STEP + u
        rows = pl.ds(u * EXPERT_BLOCK, EXPERT_BLOCK)

        @pl.when(blk < info_ref[n_blocks])
        def _():
            x = xb_ref[rows, :].astype(BF16)
            g = _dot(x, wg_ref[0])
            up = _dot(x, wu_ref[0])
            hmid = (g * jax.nn.sigmoid(g) * up).astype(BF16)
            yb_ref[rows, :] = _dot(hmid, wd_ref[0])

        @pl.when(blk >= info_ref[n_blocks])
        def _():
            yb_ref[rows, :] = jnp.zeros((EXPERT_BLOCK, D_MODEL), F32)


def _experts(info, xb, lw):
    n_rows = xb.shape[0]
    n_blocks = n_rows // EXPERT_BLOCK
    assert n_blocks % BLOCKS_PER_STEP == 0
    step_rows = BLOCKS_PER_STEP * EXPERT_BLOCK
    w_specs, w_args = [], []
    for u in range(BLOCKS_PER_STEP):
        pick = lambda i, info, u=u: (info[i * BLOCKS_PER_STEP + u], 0, 0)
        w_specs += [pl.BlockSpec((1, D_MODEL, D_EXPERT), pick), pl.BlockSpec((1, D_MODEL, D_EXPERT), pick),
                    pl.BlockSpec((1, D_EXPERT, D_MODEL), pick)]
        w_args += [lw["w_gate"], lw["w_up"], lw["w_down"]]
    return pl.pallas_call(
        functools.partial(_expert_kernel, n_blocks=n_blocks),
        grid_spec=pltpu.PrefetchScalarGridSpec(
            num_scalar_prefetch=1,
            grid=(n_blocks // BLOCKS_PER_STEP,),
            in_specs=[pl.BlockSpec((step_rows, D_MODEL), lambda i, info: (i, 0))] + w_specs,
            out_specs=pl.BlockSpec((step_rows, D_MODEL), lambda i, info: (i, 0)),
        ),
        out_shape=jax.ShapeDtypeStruct((n_rows, D_MODEL), F32),
        compiler_params=_cparams(("arbitrary",)),
        name="experts",
    )(info, xb, *w_args)


def _combine_kernel(d0_ref, d1_ref, h_ref, rf_ref, p_ref, yb_ref, gn_ref, wg_ref, wp_ref, o_ref,
                    y0_ref, y1_ref, sem):
    T = h_ref.shape[0]
    base = pl.program_id(0) * T

    def start(t, c):
        _row_copy(yb_ref, d0_ref[base + t], y0_ref, t, sem.at[0]).start(priority=0)
        _row_copy(yb_ref, d1_ref[base + t], y1_ref, t, sem.at[1]).start(priority=1)
        return c

    lax.fori_loop(0, T, start, 0, unroll=8)

    proj = _dot(p_ref[...].astype(BF16), wp_ref[...])
    gates = jnp.concatenate([rf_ref[...], jnp.zeros((LANES - SUBLANES, T), F32)], axis=0).T

    def wait(t, c):
        _row_copy(yb_ref, d0_ref[base + t], y0_ref, t, sem.at[0]).wait()
        _row_copy(yb_ref, d1_ref[base + t], y1_ref, t, sem.at[1]).wait()
        return c

    lax.fori_loop(0, T, wait, 0, unroll=8)

    h2 = h_ref[...] + gates[:, 0:1] * y0_ref[...] + gates[:, 1:2] * y1_ref[...]
    gate = jax.nn.sigmoid(_dot(_rms(h2, gn_ref[...]).astype(BF16), wg_ref[...]))
    o_ref[...] = h2 + gate * proj


def _combine(h, rf, p, yb, d0, d1, lw):
    n_tok = h.shape[0]
    T = T_TOK
    const = lambda shape: pl.BlockSpec(shape, lambda i, d0, d1: (0,) * len(shape))
    return pl.pallas_call(
        _combine_kernel,
        grid_spec=pltpu.PrefetchScalarGridSpec(
            num_scalar_prefetch=2,
            grid=(n_tok // T,),
            in_specs=[
                pl.BlockSpec((T, D_MODEL), lambda i, d0, d1: (i, 0)),
                pl.BlockSpec((SUBLANES, T), lambda i, d0, d1: (0, i)),
                pl.BlockSpec((T, PLE_DIM), lambda i, d0, d1: (i, 0)),
                pl.BlockSpec(memory_space=pl.ANY),
                const((1, D_MODEL)), const((D_MODEL, D_MODEL)), const((PLE_DIM, D_MODEL)),
            ],
            out_specs=pl.BlockSpec((T, D_MODEL), lambda i, d0, d1: (i, 0)),
            scratch_shapes=[pltpu.VMEM((T, D_MODEL), F32), pltpu.VMEM((T, D_MODEL), F32),
                            pltpu.SemaphoreType.DMA((2,))],
        ),
        out_shape=jax.ShapeDtypeStruct((n_tok, D_MODEL), F32),
        compiler_params=_cparams(("arbitrary",)),
        name="combine_ple",
    )(d0, d1, h, rf, p, yb, lw["ple_norm"], lw["ple_w_gate"], lw["ple_w_proj"])


def _pad_lanes(x, lo, width=LANES):
    n = x.shape[-1]
    pad = [(0, 0)] * (x.ndim - 1) + [(lo, width - lo - n)]
    return jnp.pad(x, pad)


def _rotate_half_cols(x):
    half = QK_ROPE // 2
    return jnp.concatenate([-x[..., half:], x[..., :half]], axis=-1)


def _swap_halves(g):
    half = QK_ROPE // 2
    return jnp.concatenate([g[..., half:], g[..., :half]], axis=-1)


def _prepare(attn_norm, attn_w_in, attn_q_a_norm, attn_kv_a_norm, attn_w_qb, attn_w_kvb, attn_q_gain,
             attn_k_gain, attn_w_o, conv_norm, conv_w_in, conv_w, conv_w_out, moe_norm, router_group_w,
             router_group_b, router_expert_w, router_expert_b, expert_w_gate, expert_w_up, expert_w_down,
             ple_norm, ple_w_gate, ple_w_proj):
    w = {}
    w_in = attn_w_in[0]
    rope_cols = w_in[:, Q_LORA + KV_LORA:]
    w["w_in_ext"] = jnp.concatenate(
        [w_in[:, :Q_LORA + KV_LORA], _pad_lanes(rope_cols, QK_NOPE), _pad_lanes(_rotate_half_cols(rope_cols), QK_NOPE)],
        axis=1).astype(BF16)
    w["attn_norm"] = attn_norm[0][None]
    w["g_qa"] = attn_q_a_norm[0][None]
    w["g_kva"] = attn_kv_a_norm[0][None]
    wqb = attn_w_qb[0].reshape(Q_LORA, N_HEADS, QK_DIM)
    w["wq"] = _pad_lanes(wqb, 0).reshape(Q_LORA, HEAD_PAD).astype(BF16)
    w["wq_sw"] = _pad_lanes(_rotate_half_cols(wqb[..., QK_NOPE:]), QK_NOPE).reshape(Q_LORA, HEAD_PAD).astype(BF16)
    wkvb = attn_w_kvb[0].reshape(KV_LORA, N_HEADS, QK_NOPE + V_DIM)
    w["wk"] = _pad_lanes(wkvb[..., :QK_NOPE], 0).reshape(KV_LORA, HEAD_PAD).astype(BF16)
    w["wvt"] = _pad_lanes(wkvb[..., QK_NOPE:], 0, V_EXT).reshape(KV_LORA, N_HEADS * V_EXT).T.astype(BF16)
    scale = LOG2_E / float(QK_DIM) ** 0.5
    gqs = attn_q_gain[0] * scale
    gk = attn_k_gain[0]
    zero = jnp.zeros((SUBLANES - 2, LANES), F32)
    key_norm_bound = jnp.full((1, LANES), BOUND_MARGIN * float(QK_DIM) ** 0.5, F32) * jnp.max(jnp.abs(gk))
    w["gq"] = jnp.concatenate([_pad_lanes(gqs, 0)[None], _pad_lanes(_swap_halves(gqs[QK_NOPE:]), QK_NOPE)[None],
                               key_norm_bound, zero[1:]])
    w["gk"] = jnp.concatenate([_pad_lanes(gk[:QK_NOPE], 0)[None], _pad_lanes(gk[QK_NOPE:], QK_NOPE)[None],
                               _pad_lanes(_swap_halves(gk[QK_NOPE:]), QK_NOPE)[None], zero[1:]])
    w["w_o"] = attn_w_o[0].astype(BF16)
    w["conv_norm"] = conv_norm[0][None]
    w["conv_w_in"] = conv_w_in[0].astype(BF16)
    w["conv_w"] = jnp.pad(conv_w[0], ((0, SUBLANES - conv_w.shape[1]), (0, 0)))
    w["conv_w_out"] = conv_w_out[0].astype(BF16)
    tri = lax.broadcasted_iota(I32, (T_TOK, T_TOK), 0) < lax.broadcasted_iota(I32, (T_TOK, T_TOK), 1)
    w["tri"] = tri.astype(BF16)
    layers = []
    for i in range(moe_norm.shape[0]):
        rw = jnp.zeros((D_MODEL, LANES), F32)
        rw = rw.at[:, GROUP_COL0:GROUP_COL0 + N_GROUPS].set(router_group_w[i])
        rw = rw.at[:, EXPERT_COL0:EXPERT_COL0 + N_EXPERTS].set(router_expert_w[i])
        rw_hi = rw.astype(BF16)
        rw_lo = (rw - rw_hi.astype(F32)).astype(BF16)
        rb = jnp.zeros((1, LANES), F32)
        rb = rb.at[0, GROUP_COL0:GROUP_COL0 + N_GROUPS].set(router_group_b[i])
        rb = rb.at[0, EXPERT_COL0:EXPERT_COL0 + N_EXPERTS].set(router_expert_b[i])
        layers.append(dict(
            moe_norm=moe_norm[i][None], rw=jnp.concatenate([rw_hi, rw_lo], axis=1), rb=rb,
            w_gate=expert_w_gate[i].astype(BF16), w_up=expert_w_up[i].astype(BF16),
            w_down=expert_w_down[i].astype(BF16),
            ple_norm=ple_norm[i][None], ple_w_gate=ple_w_gate[i].astype(BF16),
            ple_w_proj=ple_w_proj[i].astype(BF16)))
    return w, layers


def _rope_tiles(S):
    pos = jnp.arange(S, dtype=F32)
    inv = ROPE_THETA ** (-jnp.arange(0, QK_ROPE, 2, dtype=F32) / QK_ROPE)
    ang = pos[:, None] * inv[None, :]
    cos, sin = jnp.cos(ang), jnp.sin(ang)
    cos_t = jnp.concatenate([jnp.ones((S, QK_NOPE), F32), cos, cos, jnp.zeros((S, LANES - QK_DIM), F32)], axis=1)
    sin_t = jnp.concatenate([jnp.zeros((S, QK_NOPE), F32), sin, sin, jnp.zeros((S, LANES - QK_DIM), F32)], axis=1)
    return cos_t, sin_t


def _moe_ple(h1, xn, ri, rf, cnt, p, lw):
    n_tok = h1.shape[0]
    n_assign = 2 * n_tok
    n_rows = -(-n_assign // EXPERT_BLOCK) * EXPERT_BLOCK + N_EXPERTS * EXPERT_BLOCK
    dest, info = _plan(cnt, ri, n_rows // EXPERT_BLOCK)
    d0, d1 = dest[0], dest[1]
    xb = _dispatch(xn, d0, d1, n_rows)
    yb = _experts(info, xb, lw)
    return _combine(h1, rf, p, yb, d0, d1, lw)


def _trunk(x, p, w, layers):
    B, S, _ = x.shape
    n_tok = B * S
    cos_t, sin_t = _rope_tiles(S)
    h = x.reshape(n_tok, D_MODEL)
    p = p.reshape(p.shape[0], n_tok, PLE_DIM)
    q, k, vt = _attn_in(x, w, cos_t, sin_t)
    ot = _flash(q, k, vt)
    h1, xn, ri, rf, cnt = _attn_out(h, ot, w, layers[0])
    h = _moe_ple(h1, xn, ri, rf, cnt, p[0], layers[0])
    h1, xn, ri, rf, cnt = _conv(h, S, w, layers[1])
    h = _moe_ple(h1, xn, ri, rf, cnt, p[1], layers[1])
    return h.reshape(B, S, D_MODEL)


def kernel(x_prompt, x_sample, p_prompt, p_sample, attn_norm, attn_w_in, attn_q_a_norm, attn_kv_a_norm, attn_w_qb, attn_w_kvb, attn_q_gain, attn_k_gain, attn_w_o, conv_norm, conv_w_in, conv_w, conv_w_out, moe_norm, router_group_w, router_group_b, router_expert_w, router_expert_b, expert_w_gate, expert_w_up, expert_w_down, ple_norm, ple_w_gate, ple_w_proj):
    assert x_prompt.shape[-1] == D_MODEL and moe_norm.shape[0] == 2
    assert attn_w_in.shape == (1, D_MODEL, Q_LORA + KV_LORA + QK_ROPE)
    assert attn_w_qb.shape == (1, Q_LORA, N_HEADS * QK_DIM)
    assert attn_w_kvb.shape == (1, KV_LORA, N_HEADS * (QK_NOPE + V_DIM))
    assert expert_w_gate.shape[1:] == (N_EXPERTS, D_MODEL, D_EXPERT)
    w, layers = _prepare(attn_norm, attn_w_in, attn_q_a_norm, attn_kv_a_norm, attn_w_qb, attn_w_kvb,
                         attn_q_gain, attn_k_gain, attn_w_o, conv_norm, conv_w_in, conv_w, conv_w_out,
                         moe_norm, router_group_w, router_group_b, router_expert_w, router_expert_b,
                         expert_w_gate, expert_w_up, expert_w_down, ple_norm, ple_w_gate, ple_w_proj)
    return (_trunk(x_prompt, p_prompt, w, layers), _trunk(x_sample, p_sample, w, layers))
```

```python
import functools

import jax
import jax.numpy as jnp
from jax import lax
from jax.experimental import pallas as pl
from jax.experimental.pallas import tpu as pltpu

F32, BF16, I32 = jnp.float32, jnp.bfloat16, jnp.int32

D_MODEL = 1024
N_HEADS = 16
QK_NOPE, QK_ROPE, V_DIM = 64, 32, 64
QK_DIM = QK_NOPE + QK_ROPE
Q_LORA, KV_LORA = 384, 256
N_GROUPS, EXPERTS_PER_GROUP = 4, 8
N_EXPERTS = N_GROUPS * EXPERTS_PER_GROUP
D_EXPERT = 512
PLE_DIM = 256
ROPE_THETA = 10000.0
RMS_EPS = 1e-6
EXPERT_BLOCK = 128

LANES = 128
SUBLANES = 8
VMEM_LIMIT = 56 * 1024 * 1024
ROW_TILE = D_MODEL // LANES

T_ATTN_IN = 256
T_Q = 512
T_TOK = 256
KV_MERGE = 2
KV_UNROLL = 8
BLOCKS_PER_STEP = 4
LOG2_E = 1.4426950408889634
BOUND_MARGIN = 1.02
MIN_DENOMINATOR = 2.0 ** -60

W_IN_EXT = Q_LORA + KV_LORA + 2 * LANES
HEAD_PAD = N_HEADS * LANES
GROUP_COL0, EXPERT_COL0 = 0, SUBLANES


def _cparams(sem):
    return pltpu.CompilerParams(dimension_semantics=sem, vmem_limit_bytes=VMEM_LIMIT)


def _rms(x, g):
    return x * lax.rsqrt(jnp.mean(x * x, axis=-1, keepdims=True) + RMS_EPS) * g


def _dot(a, b):
    return jnp.dot(a, b, preferred_element_type=F32)


def _dot_nt(a, b):
    return lax.dot_general(a, b, (((1,), (1,)), ((), ())), preferred_element_type=F32)


def _dot_tn(a, b):
    return lax.dot_general(a, b, (((0,), (0,)), ((), ())), preferred_element_type=F32)


def _attn_in_kernel(x_ref, gn_ref, win_ref, gqa_ref, gkva_ref, wq_ref, wqs_ref, wk_ref, wvt_ref,
                    gq_ref, gk_ref, cos_ref, sin_ref, q_ref, k_ref, vt_ref):
    x = x_ref[0]
    xn = _rms(x, gn_ref[...]).astype(BF16)
    a = _dot(xn, win_ref[...])
    cq = _rms(a[:, :Q_LORA], gqa_ref[...]).astype(BF16)
    ckv = _rms(a[:, Q_LORA:Q_LORA + KV_LORA], gkva_ref[...]).astype(BF16)
    kr = a[:, Q_LORA + KV_LORA:Q_LORA + KV_LORA + LANES]
    kr_sw = a[:, Q_LORA + KV_LORA + LANES:]
    cosv, sinv = cos_ref[...], sin_ref[...]
    gq, gk = gq_ref[...], gk_ref[...]

    q = _dot(cq, wq_ref[...])
    q_sw = _dot(cq, wqs_ref[...])
    kn = _dot(ckv, wk_ref[...])
    vt_ref[0, 0] = _dot_nt(wvt_ref[...], ckv).astype(BF16)

    ssq_rope = jnp.sum(kr * kr, axis=-1, keepdims=True)
    k_rope = kr * (gk[1:2] * cosv) + kr_sw * (gk[2:3] * sinv)
    q_cos, q_sin = gq[0:1] * cosv, gq[1:2] * sinv
    inv_d = 1.0 / QK_DIM
    bound_lane = lax.broadcasted_iota(I32, (x.shape[0], LANES), 1) == QK_DIM
    for h in range(N_HEADS):
        sl = slice(h * LANES, (h + 1) * LANES)
        qh = q[:, sl]
        rq = lax.rsqrt(jnp.sum(qh * qh, axis=-1, keepdims=True) * inv_d + RMS_EPS)
        qt = (qh * q_cos + q_sw[:, sl] * q_sin) * rq
        bound = jnp.sqrt(jnp.sum(qt * qt, axis=-1, keepdims=True)) * gq[2:3]
        q_ref[0, h] = jnp.where(bound_lane, -bound, qt).astype(BF16)
        kh = kn[:, sl]
        rk = lax.rsqrt((jnp.sum(kh * kh, axis=-1, keepdims=True) + ssq_rope) * inv_d + RMS_EPS)
        k_ref[0, h] = jnp.where(bound_lane, 1.0, (kh * gk[0:1] + k_rope) * rk).astype(BF16)


def _attn_in(x, w, cos_t, sin_t):
    B, S, _ = x.shape
    T = T_ATTN_IN
    nS = S // T
    const = lambda shape: pl.BlockSpec(shape, lambda b, j: (0,) * len(shape))
    return pl.pallas_call(
        _attn_in_kernel,
        grid=(B, nS),
        in_specs=[
            pl.BlockSpec((1, T, D_MODEL), lambda b, j: (b, j, 0)),
            const((1, D_MODEL)), const((D_MODEL, W_IN_EXT)), const((1, Q_LORA)), const((1, KV_LORA)),
            const((Q_LORA, HEAD_PAD)), const((Q_LORA, HEAD_PAD)), const((KV_LORA, HEAD_PAD)),
            const((N_HEADS * V_DIM, KV_LORA)), const((SUBLANES, LANES)), const((SUBLANES, LANES)),
            pl.BlockSpec((T, LANES), lambda b, j: (j, 0)),
            pl.BlockSpec((T, LANES), lambda b, j: (j, 0)),
        ],
        out_specs=[
            pl.BlockSpec((1, N_HEADS, T, LANES), lambda b, j: (b, 0, j, 0)),
            pl.BlockSpec((1, N_HEADS, T, LANES), lambda b, j: (b, 0, j, 0)),
            pl.BlockSpec((1, 1, N_HEADS * V_DIM, T), lambda b, j: (b, j, 0, 0)),
        ],
        out_shape=[
            jax.ShapeDtypeStruct((B, N_HEADS, S, LANES), BF16),
            jax.ShapeDtypeStruct((B, N_HEADS, S, LANES), BF16),
            jax.ShapeDtypeStruct((B, nS, N_HEADS * V_DIM, T), BF16),
        ],
        compiler_params=_cparams(("parallel", "parallel")),
        name="attn_in",
    )(x, w["attn_norm"], w["w_in_ext"], w["g_qa"], w["g_kva"], w["wq"], w["wq_sw"], w["wk"], w["wvt"],
      w["gq"], w["gk"], cos_t, sin_t)


def _flash_kernel(q_ref, k_ref, vt_ref, o_ref, m_ref, l_ref, acc_ref, s_ref, cm_ref, *, n_kv, t_kv, merge, unroll):
    q = q_ref[0, 0]
    t_blk = merge * t_kv
    n_blk = n_kv // merge

    def scores(j):
        kb = k_ref[0, 0, pl.ds(pl.multiple_of(j * t_blk, t_blk), t_blk), :]
        return _dot_nt(kb, q)

    def pv(j, p):
        out = _dot(vt_ref[0, j * merge], p[0:t_kv])
        for i in range(1, merge):
            out = out + _dot(vt_ref[0, j * merge + i], p[i * t_kv:(i + 1) * t_kv])
        return out

    def finish():
        o_ref[0] = (acc_ref[...] * (1.0 / l_ref[...])).astype(BF16)

    s_ref[0] = scores(0)
    s_ref[1] = scores(min(1, n_blk - 1))
    l_ref[...] = jnp.zeros(l_ref.shape, F32)
    acc_ref[...] = jnp.zeros(acc_ref.shape, F32)

    def fast_body(it, carry):
        s, s_next, l, acc = s_ref[0], s_ref[1], l_ref[...], acc_ref[...]
        for u in range(unroll):
            j = it * unroll + u
            s_next2 = scores(jnp.minimum(j + 2, n_blk - 1))
            p = jnp.exp2(s)
            l = l + jnp.sum(p, axis=0, keepdims=True)
            acc = acc + pv(j, p.astype(BF16))
            s, s_next = s_next, s_next2
        s_ref[0], s_ref[1], l_ref[...], acc_ref[...] = s, s_next, l, acc
        return carry

    lax.fori_loop(0, n_blk // unroll, fast_body, 0)
    safe = jnp.min(l_ref[...]) >= MIN_DENOMINATOR

    @pl.when(safe)
    def _():
        finish()

    @pl.when(jnp.logical_not(safe))
    def _():
        m_ref[...] = jnp.full(m_ref.shape, -jnp.inf, F32)
        l_ref[...] = jnp.zeros(l_ref.shape, F32)
        acc_ref[...] = jnp.zeros(acc_ref.shape, F32)
        s0 = scores(0)
        s_ref[0] = s0
        cm_ref[...] = jnp.max(s0, axis=0, keepdims=True)
        s_ref[1] = scores(min(1, n_blk - 1))

        def body(it, carry):
            s, s_next, cm = s_ref[0], s_ref[1], cm_ref[...]
            m, l, acc = m_ref[...], l_ref[...], acc_ref[...]
            for u in range(unroll):
                j = it * unroll + u
                s_next2 = scores(jnp.minimum(j + 2, n_blk - 1))
                cm_next = jnp.max(s_next, axis=0, keepdims=True)
                m_new = jnp.maximum(m, cm)
                p = jnp.exp2(s - m_new)
                alpha = jnp.exp2(m - m_new)
                l = alpha * l + jnp.sum(p, axis=0, keepdims=True)
                acc = alpha * acc + pv(j, p.astype(BF16))
                m, s, cm, s_next = m_new, s_next, cm_next, s_next2
            s_ref[0], s_ref[1], cm_ref[...] = s, s_next, cm
            m_ref[...], l_ref[...], acc_ref[...] = m, l, acc
            return carry

        lax.fori_loop(0, n_blk // unroll, body, 0)
        finish()


def _flash(q, k, vt):
    B, H, S, _ = q.shape
    t_kv = vt.shape[-1]
    n_kv = S // t_kv
    tq = min(T_Q, S)
    merge = KV_MERGE if n_kv % KV_MERGE == 0 else 1
    n_blk = n_kv // merge
    unroll = max(u for u in (KV_UNROLL, 4, 2, 1) if n_blk % u == 0)
    return pl.pallas_call(
        functools.partial(_flash_kernel, n_kv=n_kv, t_kv=t_kv, merge=merge, unroll=unroll),
        grid=(B, H, S // tq),
        in_specs=[
            pl.BlockSpec((1, 1, tq, LANES), lambda b, h, i: (b, h, i, 0)),
            pl.BlockSpec((1, 1, S, LANES), lambda b, h, i: (b, h, 0, 0)),
            pl.BlockSpec((1, n_kv, V_DIM, t_kv), lambda b, h, i: (b, 0, h, 0)),
        ],
        out_specs=pl.BlockSpec((1, V_DIM, tq), lambda b, h, i: (b, h, i)),
        out_shape=jax.ShapeDtypeStruct((B, H * V_DIM, S), BF16),
        scratch_shapes=[pltpu.VMEM((1, tq), F32), pltpu.VMEM((1, tq), F32), pltpu.VMEM((V_DIM, tq), F32),
                        pltpu.VMEM((2, merge * t_kv, tq), F32), pltpu.VMEM((1, tq), F32)],
        compiler_params=_cparams(("parallel", "parallel", "parallel")),
        name="flash",
    )(q, k, vt)


def _router(x1, gm_ref, rw_ref, rb_ref, tri_ref, carry_ref, xn_ref, ri_ref, rf_ref, cnt_ref):
    T = x1.shape[0]

    @pl.when(pl.program_id(0) == 0)
    def _():
        carry_ref[...] = jnp.zeros(carry_ref.shape, F32)

    xn = _rms(x1, gm_ref[...])
    _store_row_tiles(xn_ref, 0, xn)
    x_hi = xn.astype(BF16)
    x_lo = (xn - x_hi.astype(F32)).astype(BF16)
    rw = rw_ref[...]
    l_hi = _dot(x_hi, rw)
    l_lo = _dot(x_lo, rw[:, :LANES])
    logits = l_hi[:, :LANES] + l_hi[:, LANES:] + l_lo + rb_ref[...]
    lt = logits.T

    row = lax.broadcasted_iota(I32, (SUBLANES, T), 0).astype(F32)
    neg = -jnp.inf
    lg = jnp.where(row < N_GROUPS, lt[GROUP_COL0:GROUP_COL0 + SUBLANES], neg)
    mg = jnp.max(lg, axis=0, keepdims=True)
    gidx = jnp.min(jnp.where(lg == mg, row, float(SUBLANES)), axis=0, keepdims=True)
    pg_top = 1.0 / jnp.sum(jnp.exp(lg - mg), axis=0, keepdims=True)

    def grp(g):
        return lt[EXPERT_COL0 + g * EXPERTS_PER_GROUP:EXPERT_COL0 + (g + 1) * EXPERTS_PER_GROUP]

    le = jnp.where(gidx == 0.0, grp(0), jnp.where(gidx == 1.0, grp(1), jnp.where(gidx == 2.0, grp(2), grp(3))))
    m1 = jnp.max(le, axis=0, keepdims=True)
    i1 = jnp.min(jnp.where(le == m1, row, float(SUBLANES)), axis=0, keepdims=True)
    le2 = jnp.where(row == i1, neg, le)
    m2 = jnp.max(le2, axis=0, keepdims=True)
    i2 = jnp.min(jnp.where(le2 == m2, row, float(SUBLANES)), axis=0, keepdims=True)
    e21 = jnp.exp(m2 - m1)
    inv = 1.0 / (1.0 + e21)
    g1 = pg_top * inv
    g2 = pg_top * e21 * inv
    e1 = gidx * EXPERTS_PER_GROUP + i1
    e2 = gidx * EXPERTS_PER_GROUP + i2

    erow = lax.broadcasted_iota(I32, (N_EXPERTS, T), 0).astype(F32)
    oh1 = erow == e1
    oh2 = erow == e2
    oh = jnp.where(oh1, 1.0, jnp.where(oh2, 1.0, 0.0))
    before = _dot(oh.astype(BF16), tri_ref[...]) + carry_ref[...]
    r1 = jnp.sum(jnp.where(oh1, before, 0.0), axis=0, keepdims=True)
    r2 = jnp.sum(jnp.where(oh2, before, 0.0), axis=0, keepdims=True)
    carry = carry_ref[...] + jnp.sum(oh, axis=1, keepdims=True)
    carry_ref[...] = carry
    cnt_ref[...] = carry[:, :LANES].astype(I32)

    ri = jnp.where(row == 0.0, e1, jnp.where(row == 1.0, e2, jnp.where(row == 2.0, r1, jnp.where(row == 3.0, r2, 0.0))))
    ri_ref[...] = ri.astype(I32)
    rf_ref[...] = jnp.where(row == 0.0, g1, jnp.where(row == 1.0, g2, 0.0))


def _router_specs(T, n_tok):
    const = lambda shape: pl.BlockSpec(shape, lambda i: (0,) * len(shape))
    in_specs = [const((1, D_MODEL)), const((D_MODEL, 2 * LANES)), const((1, LANES)), const((T, T))]
    out_specs = [
        pl.BlockSpec((T * ROW_TILE, LANES), lambda i: (i, 0)),
        pl.BlockSpec((SUBLANES, T), lambda i: (0, i)),
        pl.BlockSpec((SUBLANES, T), lambda i: (0, i)),
        const((N_EXPERTS, LANES)),
    ]
    out_shape = [
        jax.ShapeDtypeStruct((n_tok * ROW_TILE, LANES), F32),
        jax.ShapeDtypeStruct((SUBLANES, n_tok), I32),
        jax.ShapeDtypeStruct((SUBLANES, n_tok), F32),
        jax.ShapeDtypeStruct((N_EXPERTS, LANES), I32),
    ]
    return in_specs, out_specs, out_shape


def _attn_out_kernel(h_ref, ot_ref, wo_ref, gm_ref, rw_ref, rb_ref, tri_ref,
                     h1_ref, xn_ref, ri_ref, rf_ref, cnt_ref, carry_ref):
    x1 = h_ref[...] + _dot_tn(ot_ref[0], wo_ref[...])
    h1_ref[...] = x1
    _router(x1, gm_ref, rw_ref, rb_ref, tri_ref, carry_ref, xn_ref, ri_ref, rf_ref, cnt_ref)


def _attn_out(h, ot, w, lw):
    n_tok = h.shape[0]
    S = ot.shape[-1]
    T = T_TOK
    nS = S // T
    const = lambda shape: pl.BlockSpec(shape, lambda i: (0,) * len(shape))
    r_in, r_out, r_shape = _router_specs(T, n_tok)
    return pl.pallas_call(
        _attn_out_kernel,
        grid=(n_tok // T,),
        in_specs=[
            pl.BlockSpec((T, D_MODEL), lambda i: (i, 0)),
            pl.BlockSpec((1, N_HEADS * V_DIM, T), lambda i: (i // nS, 0, i % nS)),
            const((N_HEADS * V_DIM, D_MODEL)),
        ] + r_in,
        out_specs=[pl.BlockSpec((T, D_MODEL), lambda i: (i, 0))] + r_out,
        out_shape=[jax.ShapeDtypeStruct((n_tok, D_MODEL), F32)] + r_shape,
        scratch_shapes=[pltpu.VMEM((N_EXPERTS, T), F32)],
        compiler_params=_cparams(("arbitrary",)),
        name="attn_out_router",
    )(h, ot, w["w_o"], lw["moe_norm"], lw["rw"], lw["rb"], w["tri"])


def _conv_kernel(h_ref, hp_ref, hn_ref, gn_ref, win_ref, cw_ref, wout_ref, gm_ref, rw_ref, rb_ref, tri_ref,
                 h1_ref, xn_ref, ri_ref, rf_ref, cnt_ref, carry_ref, *, tiles_per_seq):
    T = h_ref.shape[0]
    i = pl.program_id(0)
    x = h_ref[...]
    gn = gn_ref[...]
    bcu = _dot(_rms(x, gn).astype(BF16), win_ref[...])
    b = bcu[:, :D_MODEL]
    cu = bcu[:, D_MODEL:2 * D_MODEL] * bcu[:, 2 * D_MODEL:]
    halo = jnp.concatenate([hp_ref[...], hn_ref[...]], axis=0)
    hcu = _dot(_rms(halo, gn).astype(BF16), win_ref[:, D_MODEL:])
    hcu = hcu[:, :D_MODEL] * hcu[:, D_MODEL:]
    first = (i % tiles_per_seq) == 0
    last = (i % tiles_per_seq) == tiles_per_seq - 1
    cu_before = jnp.where(first, 0.0, hcu[SUBLANES - 1:SUBLANES])
    cu_after = jnp.where(last, 0.0, hcu[SUBLANES:SUBLANES + 1])
    row = lax.broadcasted_iota(I32, (T, 1), 0)
    prev = jnp.where(row == 0, cu_before, pltpu.roll(cu, 1, 0))
    nxt = jnp.where(row == T - 1, cu_after, pltpu.roll(cu, T - 1, 0))
    cw = cw_ref[...]
    y = cw[0:1] * prev + cw[1:2] * cu + cw[2:3] * nxt
    x1 = x + _dot((b * y).astype(BF16), wout_ref[...])
    h1_ref[...] = x1
    _router(x1, gm_ref, rw_ref, rb_ref, tri_ref, carry_ref, xn_ref, ri_ref, rf_ref, cnt_ref)


def _conv(h, S, w, lw):
    n_tok = h.shape[0]
    T = T_TOK
    nS = S // T
    rows8 = T // SUBLANES
    n8 = n_tok // SUBLANES
    const = lambda shape: pl.BlockSpec(shape, lambda i: (0,) * len(shape))
    r_in, r_out, r_shape = _router_specs(T, n_tok)
    return pl.pallas_call(
        functools.partial(_conv_kernel, tiles_per_seq=nS),
        grid=(n_tok // T,),
        in_specs=[
            pl.BlockSpec((T, D_MODEL), lambda i: (i, 0)),
            pl.BlockSpec((SUBLANES, D_MODEL), lambda i: (jnp.maximum(i * rows8 - 1, 0), 0)),
            pl.BlockSpec((SUBLANES, D_MODEL), lambda i: (jnp.minimum((i + 1) * rows8, n8 - 1), 0)),
            const((1, D_MODEL)), const((D_MODEL, 3 * D_MODEL)), const((SUBLANES, D_MODEL)),
            const((D_MODEL, D_MODEL)),
        ] + r_in,
        out_specs=[pl.BlockSpec((T, D_MODEL), lambda i: (i, 0))] + r_out,
        out_shape=[jax.ShapeDtypeStruct((n_tok, D_MODEL), F32)] + r_shape,
        scratch_shapes=[pltpu.VMEM((N_EXPERTS, T), F32)],
        compiler_params=_cparams(("arbitrary",)),
        name="conv_router",
    )(h, h, h, w["conv_norm"], w["conv_w_in"], w["conv_w"], w["conv_w_out"],
      lw["moe_norm"], lw["rw"], lw["rb"], w["tri"])


def _plan_kernel(cnt_ref, ri_ref, dest_ref, info_ref, *, n_blocks):
    ri = ri_ref[...]
    eid = ri[0:2]
    offset = jnp.zeros(eid.shape, I32)
    blocks_before = jnp.int32(0)
    for e in range(N_EXPERTS):
        nb = (cnt_ref[e, 0] + (EXPERT_BLOCK - 1)) // EXPERT_BLOCK
        offset = jnp.where(eid == e, blocks_before * EXPERT_BLOCK, offset)

        def fill(j, c, e=e, base=blocks_before):
            info_ref[base + j] = e
            return c

        lax.fori_loop(0, nb, fill, 0)
        blocks_before = blocks_before + nb

    def fill_tail(j, c):
        info_ref[j] = N_EXPERTS - 1
        return c

    lax.fori_loop(blocks_before, n_blocks, fill_tail, 0)
    info_ref[n_blocks] = blocks_before
    dest_ref[...] = jnp.zeros(dest_ref.shape, I32)
    dest_ref[0:2, :] = ri[2:4] + offset


def _plan(cnt, ri, n_blocks):
    n_tok = ri.shape[1]
    return pl.pallas_call(
        functools.partial(_plan_kernel, n_blocks=n_blocks),
        in_specs=[pl.BlockSpec(memory_space=pltpu.SMEM), pl.BlockSpec(memory_space=pltpu.VMEM)],
        out_specs=[pl.BlockSpec(memory_space=pltpu.VMEM), pl.BlockSpec(memory_space=pltpu.SMEM)],
        out_shape=[jax.ShapeDtypeStruct((SUBLANES, n_tok), I32),
                   jax.ShapeDtypeStruct((n_blocks + 1,), I32)],
        compiler_params=pltpu.CompilerParams(vmem_limit_bytes=VMEM_LIMIT),
        name="plan",
    )(cnt, ri)


def _store_row_tiles(ref, r0, x):
    for s in range(ROW_TILE):
        ref[pl.ds(r0 * ROW_TILE + s, x.shape[0], stride=ROW_TILE), :] = x[:, s * LANES:(s + 1) * LANES]


def _load_row_tiles(ref, r0, n):
    cols = [ref[pl.ds(r0 * ROW_TILE + s, n, stride=ROW_TILE), :] for s in range(ROW_TILE)]
    return jnp.concatenate(cols, axis=1)


def _row_copy(src_ref, s, dst_ref, d, sem):
    src = src_ref.at[pl.ds(pl.multiple_of(s * ROW_TILE, ROW_TILE), ROW_TILE)]
    dst = dst_ref.at[pl.ds(pl.multiple_of(d * ROW_TILE, ROW_TILE), ROW_TILE)]
    return pltpu.make_async_copy(src, dst, sem)


def _dispatch_kernel(d0_ref, d1_ref, x_ref, xb_in_ref, xb_ref, sem):
    del xb_in_ref
    T = x_ref.shape[0] // ROW_TILE
    base = pl.program_id(0) * T

    def start(t, c):
        _row_copy(x_ref, t, xb_ref, d0_ref[base + t], sem.at[0]).start(priority=0)
        _row_copy(x_ref, t, xb_ref, d1_ref[base + t], sem.at[1]).start(priority=1)
        return c

    lax.fori_loop(0, T, start, 0, unroll=8)

    def wait(t, c):
        _row_copy(x_ref, t, xb_ref, d0_ref[base + t], sem.at[0]).wait()
        _row_copy(x_ref, t, xb_ref, d1_ref[base + t], sem.at[1]).wait()
        return c

    lax.fori_loop(0, T, wait, 0, unroll=8)


def _dispatch(xn, d0, d1, n_rows):
    n_tok = xn.shape[0] // ROW_TILE
    T = T_TOK
    xb0 = jnp.zeros((n_rows * ROW_TILE, LANES), F32)
    return pl.pallas_call(
        _dispatch_kernel,
        grid_spec=pltpu.PrefetchScalarGridSpec(
            num_scalar_prefetch=2,
            grid=(n_tok // T,),
            in_specs=[pl.BlockSpec((T * ROW_TILE, LANES), lambda i, d0, d1: (i, 0)),
                      pl.BlockSpec(memory_space=pl.ANY)],
            out_specs=pl.BlockSpec(memory_space=pl.ANY),
            scratch_shapes=[pltpu.SemaphoreType.DMA((2,))],
        ),
        out_shape=jax.ShapeDtypeStruct((n_rows * ROW_TILE, LANES), F32),
        input_output_aliases={3: 0},
        compiler_params=_cparams(("arbitrary",)),
        name="dispatch",
    )(d0, d1, xn, xb0)


def _expert_kernel(info_ref, xb_ref, *refs, n_blocks):
    yb_ref = refs[-1]
    i = pl.program_id(0)
    for u in range(BLOCKS_PER_STEP):
        wg_ref, wu_ref, wd_ref = refs[3 * u:3 * u + 3]
        blk = i * BLOCKS_PER_STEP + u

        @pl.when(blk < info_ref[n_blocks])
        def _():
            x = _load_row_tiles(xb_ref, u * EXPERT_BLOCK, EXPERT_BLOCK).astype(BF16)
            g = _dot(x, wg_ref[0])
            up = _dot(x, wu_ref[0])
            hmid = (g * jax.nn.sigmoid(g) * up).astype(BF16)
            _store_row_tiles(yb_ref, u * EXPERT_BLOCK, _dot(hmid, wd_ref[0]))

        @pl.when(blk >= info_ref[n_blocks])
        def _():
            n = EXPERT_BLOCK * ROW_TILE
            yb_ref[pl.ds(u * n, n), :] = jnp.zeros((n, LANES), F32)


def _experts(info, xb, lw):
    n_rows = xb.shape[0] // ROW_TILE
    n_blocks = n_rows // EXPERT_BLOCK
    assert n_blocks % BLOCKS_PER_STEP == 0
    step_rows = BLOCKS_PER_STEP * EXPERT_BLOCK * ROW_TILE
    w_specs, w_args = [], []
    for u in range(BLOCKS_PER_STEP):
        pick = lambda i, info, u=u: (info[i * BLOCKS_PER_STEP + u], 0, 0)
        w_specs += [pl.BlockSpec((1, D_MODEL, D_EXPERT), pick), pl.BlockSpec((1, D_MODEL, D_EXPERT), pick),
                    pl.BlockSpec((1, D_EXPERT, D_MODEL), pick)]
        w_args += [lw["w_gate"], lw["w_up"], lw["w_down"]]
    return pl.pallas_call(
        functools.partial(_expert_kernel, n_blocks=n_blocks),
        grid_spec=pltpu.PrefetchScalarGridSpec(
            num_scalar_prefetch=1,
            grid=(n_blocks // BLOCKS_PER_STEP,),
            in_specs=[pl.BlockSpec((step_rows, LANES), lambda i, info: (i, 0))] + w_specs,
            out_specs=pl.BlockSpec((step_rows, LANES), lambda i, info: (i, 0)),
        ),
        out_shape=jax.ShapeDtypeStruct((n_rows * ROW_TILE, LANES), F32),
        compiler_params=_cparams(("arbitrary",)),
        name="experts",
    )(info, xb, *w_args)


def _combine_kernel(d0_ref, d1_ref, h_ref, rf_ref, p_ref, yb_ref, gn_ref, wg_ref, wp_ref, o_ref,
                    y0_ref, y1_ref, sem):
    T = h_ref.shape[0]
    base = pl.program_id(0) * T

    def start(t, c):
        _row_copy(yb_ref, d0_ref[base + t], y0_ref, t, sem.at[0]).start(priority=0)
        _row_copy(yb_ref, d1_ref[base + t], y1_ref, t, sem.at[1]).start(priority=1)
        return c

    lax.fori_loop(0, T, start, 0, unroll=8)

    proj = _dot(p_ref[...].astype(BF16), wp_ref[...])
    gates = jnp.concatenate([rf_ref[...], jnp.zeros((LANES - SUBLANES, T), F32)], axis=0).T

    def wait(t, c):
        _row_copy(yb_ref, d0_ref[base + t], y0_ref, t, sem.at[0]).wait()
        _row_copy(yb_ref, d1_ref[base + t], y1_ref, t, sem.at[1]).wait()
        return c

    lax.fori_loop(0, T, wait, 0, unroll=8)

    y0 = _load_row_tiles(y0_ref, 0, T)
    y1 = _load_row_tiles(y1_ref, 0, T)
    h2 = h_ref[...] + gates[:, 0:1] * y0 + gates[:, 1:2] * y1
    gate = jax.nn.sigmoid(_dot(_rms(h2, gn_ref[...]).astype(BF16), wg_ref[...]))
    o_ref[...] = h2 + gate * proj


def _combine(h, rf, p, yb, d0, d1, lw):
    n_tok = h.shape[0]
    T = T_TOK
    const = lambda shape: pl.BlockSpec(shape, lambda i, d0, d1: (0,) * len(shape))
    return pl.pallas_call(
        _combine_kernel,
        grid_spec=pltpu.PrefetchScalarGridSpec(
            num_scalar_prefetch=2,
            grid=(n_tok // T,),
            in_specs=[
                pl.BlockSpec((T, D_MODEL), lambda i, d0, d1: (i, 0)),
                pl.BlockSpec((SUBLANES, T), lambda i, d0, d1: (0, i)),
                pl.BlockSpec((T, PLE_DIM), lambda i, d0, d1: (i, 0)),
                pl.BlockSpec(memory_space=pl.ANY),
                const((1, D_MODEL)), const((D_MODEL, D_MODEL)), const((PLE_DIM, D_MODEL)),
            ],
            out_specs=pl.BlockSpec((T, D_MODEL), lambda i, d0, d1: (i, 0)),
            scratch_shapes=[pltpu.VMEM((T * ROW_TILE, LANES), F32), pltpu.VMEM((T * ROW_TILE, LANES), F32),
                            pltpu.SemaphoreType.DMA((2,))],
        ),
        out_shape=jax.ShapeDtypeStruct((n_tok, D_MODEL), F32),
        compiler_params=_cparams(("arbitrary",)),
        name="combine_ple",
    )(d0, d1, h, rf, p, yb, lw["ple_norm"], lw["ple_w_gate"], lw["ple_w_proj"])


def _pad_lanes(x, lo, width=LANES):
    n = x.shape[-1]
    pad = [(0, 0)] * (x.ndim - 1) + [(lo, width - lo - n)]
    return jnp.pad(x, pad)


def _rotate_half_cols(x):
    half = QK_ROPE // 2
    return jnp.concatenate([-x[..., half:], x[..., :half]], axis=-1)


def _swap_halves(g):
    half = QK_ROPE // 2
    return jnp.concatenate([g[..., half:], g[..., :half]], axis=-1)


def _prepare(attn_norm, attn_w_in, attn_q_a_norm, attn_kv_a_norm, attn_w_qb, attn_w_kvb, attn_q_gain,
             attn_k_gain, attn_w_o, conv_norm, conv_w_in, conv_w, conv_w_out, moe_norm, router_group_w,
             router_group_b, router_expert_w, router_expert_b, expert_w_gate, expert_w_up, expert_w_down,
             ple_norm, ple_w_gate, ple_w_proj):
    w = {}
    w_in = attn_w_in[0]
    rope_cols = w_in[:, Q_LORA + KV_LORA:]
    w["w_in_ext"] = jnp.concatenate(
        [w_in[:, :Q_LORA + KV_LORA], _pad_lanes(rope_cols, QK_NOPE), _pad_lanes(_rotate_half_cols(rope_cols), QK_NOPE)],
        axis=1).astype(BF16)
    w["attn_norm"] = attn_norm[0][None]
    w["g_qa"] = attn_q_a_norm[0][None]
    w["g_kva"] = attn_kv_a_norm[0][None]
    wqb = attn_w_qb[0].reshape(Q_LORA, N_HEADS, QK_DIM)
    w["wq"] = _pad_lanes(wqb, 0).reshape(Q_LORA, HEAD_PAD).astype(BF16)
    w["wq_sw"] = _pad_lanes(_rotate_half_cols(wqb[..., QK_NOPE:]), QK_NOPE).reshape(Q_LORA, HEAD_PAD).astype(BF16)
    wkvb = attn_w_kvb[0].reshape(KV_LORA, N_HEADS, QK_NOPE + V_DIM)
    w["wk"] = _pad_lanes(wkvb[..., :QK_NOPE], 0).reshape(KV_LORA, HEAD_PAD).astype(BF16)
    w["wvt"] = wkvb[..., QK_NOPE:].reshape(KV_LORA, N_HEADS * V_DIM).T.astype(BF16)
    scale = LOG2_E / float(QK_DIM) ** 0.5
    gqs = attn_q_gain[0] * scale
    gk = attn_k_gain[0]
    zero = jnp.zeros((SUBLANES - 2, LANES), F32)
    key_norm_bound = jnp.full((1, LANES), BOUND_MARGIN * float(QK_DIM) ** 0.5, F32) * jnp.max(jnp.abs(gk))
    w["gq"] = jnp.concatenate([_pad_lanes(gqs, 0)[None], _pad_lanes(_swap_halves(gqs[QK_NOPE:]), QK_NOPE)[None],
                               key_norm_bound, zero[1:]])
    w["gk"] = jnp.concatenate([_pad_lanes(gk[:QK_NOPE], 0)[None], _pad_lanes(gk[QK_NOPE:], QK_NOPE)[None],
                               _pad_lanes(_swap_halves(gk[QK_NOPE:]), QK_NOPE)[None], zero[1:]])
    w["w_o"] = attn_w_o[0].astype(BF16)
    w["conv_norm"] = conv_norm[0][None]
    w["conv_w_in"] = conv_w_in[0].astype(BF16)
    w["conv_w"] = jnp.pad(conv_w[0], ((0, SUBLANES - conv_w.shape[1]), (0, 0)))
    w["conv_w_out"] = conv_w_out[0].astype(BF16)
    tri = lax.broadcasted_iota(I32, (T_TOK, T_TOK), 0) < lax.broadcasted_iota(I32, (T_TOK, T_TOK), 1)
    w["tri"] = tri.astype(BF16)
    layers = []
    for i in range(moe_norm.shape[0]):
        rw = jnp.zeros((D_MODEL, LANES), F32)
        rw = rw.at[:, GROUP_COL0:GROUP_COL0 + N_GROUPS].set(router_group_w[i])
        rw = rw.at[:, EXPERT_COL0:EXPERT_COL0 + N_EXPERTS].set(router_expert_w[i])
        rw_hi = rw.astype(BF16)
        rw_lo = (rw - rw_hi.astype(F32)).astype(BF16)
        rb = jnp.zeros((1, LANES), F32)
        rb = rb.at[0, GROUP_COL0:GROUP_COL0 + N_GROUPS].set(router_group_b[i])
        rb = rb.at[0, EXPERT_COL0:EXPERT_COL0 + N_EXPERTS].set(router_expert_b[i])
        layers.append(dict(
            moe_norm=moe_norm[i][None], rw=jnp.concatenate([rw_hi, rw_lo], axis=1), rb=rb,
            w_gate=expert_w_gate[i].astype(BF16), w_up=expert_w_up[i].astype(BF16),
            w_down=expert_w_down[i].astype(BF16),
            ple_norm=ple_norm[i][None], ple_w_gate=ple_w_gate[i].astype(BF16),
            ple_w_proj=ple_w_proj[i].astype(BF16)))
    return w, layers


def _rope_tiles(S):
    pos = jnp.arange(S, dtype=F32)
    inv = ROPE_THETA ** (-jnp.arange(0, QK_ROPE, 2, dtype=F32) / QK_ROPE)
    ang = pos[:, None] * inv[None, :]
    cos, sin = jnp.cos(ang), jnp.sin(ang)
    cos_t = jnp.concatenate([jnp.ones((S, QK_NOPE), F32), cos, cos, jnp.zeros((S, LANES - QK_DIM), F32)], axis=1)
    sin_t = jnp.concatenate([jnp.zeros((S, QK_NOPE), F32), sin, sin, jnp.zeros((S, LANES - QK_DIM), F32)], axis=1)
    return cos_t, sin_t


def _moe_ple(h1, xn, ri, rf, cnt, p, lw):
    n_tok = h1.shape[0]
    n_assign = 2 * n_tok
    n_rows = -(-n_assign // EXPERT_BLOCK) * EXPERT_BLOCK + N_EXPERTS * EXPERT_BLOCK
    dest, info = _plan(cnt, ri, n_rows // EXPERT_BLOCK)
    d0, d1 = dest[0], dest[1]
    xb = _dispatch(xn, d0, d1, n_rows)
    yb = _experts(info, xb, lw)
    return _combine(h1, rf, p, yb, d0, d1, lw)


def _trunk(x, p, w, layers):
    B, S, _ = x.shape
    n_tok = B * S
    cos_t, sin_t = _rope_tiles(S)
    h = x.reshape(n_tok, D_MODEL)
    p = p.reshape(p.shape[0], n_tok, PLE_DIM)
    q, k, vt = _attn_in(x, w, cos_t, sin_t)
    ot = _flash(q, k, vt)
    h1, xn, ri, rf, cnt = _attn_out(h, ot, w, layers[0])
    h = _moe_ple(h1, xn, ri, rf, cnt, p[0], layers[0])
    h1, xn, ri, rf, cnt = _conv(h, S, w, layers[1])
    h = _moe_ple(h1, xn, ri, rf, cnt, p[1], layers[1])
    return h.reshape(B, S, D_MODEL)


def kernel(x_prompt, x_sample, p_prompt, p_sample, attn_norm, attn_w_in, attn_q_a_norm, attn_kv_a_norm, attn_w_qb, attn_w_kvb, attn_q_gain, attn_k_gain, attn_w_o, conv_norm, conv_w_in, conv_w, conv_w_out, moe_norm, router_group_w, router_group_b, router_expert_w, router_expert_b, expert_w_gate, expert_w_up, expert_w_down, ple_norm, ple_w_gate, ple_w_proj):
    assert x_prompt.shape[-1] == D_MODEL and moe_norm.shape[0] == 2
    assert attn_w_in.shape == (1, D_MODEL, Q_LORA + KV_LORA + QK_ROPE)
    assert attn_w_qb.shape == (1, Q_LORA, N_HEADS * QK_DIM)
    assert attn_w_kvb.shape == (1, KV_LORA, N_HEADS * (QK_NOPE + V_DIM))
    assert expert_w_gate.shape[1:] == (N_EXPERTS, D_MODEL, D_EXPERT)
    w, layers = _prepare(attn_norm, attn_w_in, attn_q_a_norm, attn_kv_a_norm, attn_w_qb, attn_w_kvb,
                         attn_q_gain, attn_k_gain, attn_w_o, conv_norm, conv_w_in, conv_w, conv_w_out,
                         moe_norm, router_group_w, router_group_b, router_expert_w, router_expert_b,
                         expert_w_gate, expert_w_up, expert_w_down, ple_norm, ple_w_gate, ple_w_proj)
    return (_trunk(x_prompt, p_prompt, w, layers), _trunk(x_sample, p_sample, w, layers))
```

```python
import functools

import jax
import jax.numpy as jnp
from jax import lax
from jax.experimental import pallas as pl
from jax.experimental.pallas import tpu as pltpu

F32, BF16, I32 = jnp.float32, jnp.bfloat16, jnp.int32

D_MODEL = 1024
N_HEADS = 16
QK_NOPE, QK_ROPE, V_DIM = 64, 32, 64
QK_DIM = QK_NOPE + QK_ROPE
Q_LORA, KV_LORA = 384, 256
N_GROUPS, EXPERTS_PER_GROUP = 4, 8
N_EXPERTS = N_GROUPS * EXPERTS_PER_GROUP
D_EXPERT = 512
PLE_DIM = 256
ROPE_THETA = 10000.0
RMS_EPS = 1e-6
EXPERT_BLOCK = 128

LANES = 128
SUBLANES = 8
VMEM_LIMIT = 56 * 1024 * 1024
ROW_TILE = D_MODEL // LANES

T_ATTN_IN = 256
T_Q = 512
T_TOK = 256
KV_MERGE = 2
KV_UNROLL = 8
BLOCKS_PER_STEP = 4
LOG2_E = 1.4426950408889634
BOUND_MARGIN = 1.02
MIN_DENOMINATOR = 2.0 ** -60

BF16_ROWS = 16
V_EXT = V_DIM + BF16_ROWS
W_IN_EXT = Q_LORA + KV_LORA + 2 * LANES
HEAD_PAD = N_HEADS * LANES
GROUP_COL0, EXPERT_COL0 = 0, SUBLANES


def _cparams(sem):
    return pltpu.CompilerParams(dimension_semantics=sem, vmem_limit_bytes=VMEM_LIMIT)


def _rms(x, g):
    return x * lax.rsqrt(jnp.mean(x * x, axis=-1, keepdims=True) + RMS_EPS) * g


def _dot(a, b):
    return jnp.dot(a, b, preferred_element_type=F32)


def _dot_nt(a, b):
    return lax.dot_general(a, b, (((1,), (1,)), ((), ())), preferred_element_type=F32)


def _dot_tn(a, b):
    return lax.dot_general(a, b, (((0,), (0,)), ((), ())), preferred_element_type=F32)


def _attn_in_kernel(x_ref, gn_ref, win_ref, gqa_ref, gkva_ref, wq_ref, wqs_ref, wk_ref, wvt_ref,
                    gq_ref, gk_ref, cos_ref, sin_ref, q_ref, k_ref, vt_ref):
    x = x_ref[0]
    xn = _rms(x, gn_ref[...]).astype(BF16)
    a = _dot(xn, win_ref[...])
    cq = _rms(a[:, :Q_LORA], gqa_ref[...]).astype(BF16)
    ckv = _rms(a[:, Q_LORA:Q_LORA + KV_LORA], gkva_ref[...]).astype(BF16)
    kr = a[:, Q_LORA + KV_LORA:Q_LORA + KV_LORA + LANES]
    kr_sw = a[:, Q_LORA + KV_LORA + LANES:]
    cosv, sinv = cos_ref[...], sin_ref[...]
    gq, gk = gq_ref[...], gk_ref[...]

    q = _dot(cq, wq_ref[...])
    q_sw = _dot(cq, wqs_ref[...])
    kn = _dot(ckv, wk_ref[...])
    vt = _dot_nt(wvt_ref[...], ckv)
    vrow = lax.broadcasted_iota(I32, vt.shape, 0) % V_EXT
    vt_ref[0, 0] = jnp.where(vrow == V_DIM, 1.0, vt).astype(BF16)

    ssq_rope = jnp.sum(kr * kr, axis=-1, keepdims=True)
    k_rope = kr * (gk[1:2] * cosv) + kr_sw * (gk[2:3] * sinv)
    q_cos, q_sin = gq[0:1] * cosv, gq[1:2] * sinv
    inv_d = 1.0 / QK_DIM
    bound_lane = lax.broadcasted_iota(I32, (x.shape[0], LANES), 1) == QK_DIM
    for h in range(N_HEADS):
        sl = slice(h * LANES, (h + 1) * LANES)
        qh = q[:, sl]
        rq = lax.rsqrt(jnp.sum(qh * qh, axis=-1, keepdims=True) * inv_d + RMS_EPS)
        qt = (qh * q_cos + q_sw[:, sl] * q_sin) * rq
        bound = jnp.sqrt(jnp.sum(qt * qt, axis=-1, keepdims=True)) * gq[2:3]
        q_ref[0, h] = jnp.where(bound_lane, -bound, qt).astype(BF16)
        kh = kn[:, sl]
        rk = lax.rsqrt((jnp.sum(kh * kh, axis=-1, keepdims=True) + ssq_rope) * inv_d + RMS_EPS)
        k_ref[0, h] = jnp.where(bound_lane, 1.0, (kh * gk[0:1] + k_rope) * rk).astype(BF16)


def _attn_in(x, w, cos_t, sin_t):
    B, S, _ = x.shape
    T = T_ATTN_IN
    nS = S // T
    const = lambda shape: pl.BlockSpec(shape, lambda b, j: (0,) * len(shape))
    return pl.pallas_call(
        _attn_in_kernel,
        grid=(B, nS),
        in_specs=[
            pl.BlockSpec((1, T, D_MODEL), lambda b, j: (b, j, 0)),
            const((1, D_MODEL)), const((D_MODEL, W_IN_EXT)), const((1, Q_LORA)), const((1, KV_LORA)),
            const((Q_LORA, HEAD_PAD)), const((Q_LORA, HEAD_PAD)), const((KV_LORA, HEAD_PAD)),
            const((N_HEADS * V_EXT, KV_LORA)), const((SUBLANES, LANES)), const((SUBLANES, LANES)),
            pl.BlockSpec((T, LANES), lambda b, j: (j, 0)),
            pl.BlockSpec((T, LANES), lambda b, j: (j, 0)),
        ],
        out_specs=[
            pl.BlockSpec((1, N_HEADS, T, LANES), lambda b, j: (b, 0, j, 0)),
            pl.BlockSpec((1, N_HEADS, T, LANES), lambda b, j: (b, 0, j, 0)),
            pl.BlockSpec((1, 1, N_HEADS * V_EXT, T), lambda b, j: (b, j, 0, 0)),
        ],
        out_shape=[
            jax.ShapeDtypeStruct((B, N_HEADS, S, LANES), BF16),
            jax.ShapeDtypeStruct((B, N_HEADS, S, LANES), BF16),
            jax.ShapeDtypeStruct((B, nS, N_HEADS * V_EXT, T), BF16),
        ],
        compiler_params=_cparams(("parallel", "parallel")),
        name="attn_in",
    )(x, w["attn_norm"], w["w_in_ext"], w["g_qa"], w["g_kva"], w["wq"], w["wq_sw"], w["wk"], w["wvt"],
      w["gq"], w["gk"], cos_t, sin_t)


def _flash_kernel(q_ref, k_ref, vt_ref, o_ref, m_ref, acc_ref, s_ref, cm_ref, *, n_kv, t_kv, merge, unroll):
    q = q_ref[0, 0]
    t_blk = merge * t_kv
    n_blk = n_kv // merge

    def scores(j):
        kb = k_ref[0, 0, pl.ds(pl.multiple_of(j * t_blk, t_blk), t_blk), :]
        return _dot_nt(kb, q)

    def pv(j, p):
        out = _dot(vt_ref[0, j * merge], p[0:t_kv])
        for i in range(1, merge):
            out = out + _dot(vt_ref[0, j * merge + i], p[i * t_kv:(i + 1) * t_kv])
        return out

    def finish():
        o_ref[0] = (acc_ref[0:V_DIM] * (1.0 / acc_ref[V_DIM:V_DIM + 1])).astype(BF16)

    s_ref[0] = scores(0)
    s_ref[1] = scores(min(1, n_blk - 1))
    acc_ref[...] = jnp.zeros(acc_ref.shape, F32)

    def fast_body(it, carry):
        s, s_next, acc = s_ref[0], s_ref[1], acc_ref[...]
        for u in range(unroll):
            j = it * unroll + u
            s_next2 = scores(jnp.minimum(j + 2, n_blk - 1))
            acc = acc + pv(j, jnp.exp2(s).astype(BF16))
            s, s_next = s_next, s_next2
        s_ref[0], s_ref[1], acc_ref[...] = s, s_next, acc
        return carry

    lax.fori_loop(0, n_blk // unroll, fast_body, 0)
    safe = jnp.min(acc_ref[V_DIM:V_DIM + 1]) >= MIN_DENOMINATOR

    @pl.when(safe)
    def _():
        finish()

    @pl.when(jnp.logical_not(safe))
    def _():
        m_ref[...] = jnp.full(m_ref.shape, -jnp.inf, F32)
        acc_ref[...] = jnp.zeros(acc_ref.shape, F32)
        s0 = scores(0)
        s_ref[0] = s0
        cm_ref[...] = jnp.max(s0, axis=0, keepdims=True)
        s_ref[1] = scores(min(1, n_blk - 1))

        def body(it, carry):
            s, s_next, cm = s_ref[0], s_ref[1], cm_ref[...]
            m, acc = m_ref[...], acc_ref[...]
            for u in range(unroll):
                j = it * unroll + u
                s_next2 = scores(jnp.minimum(j + 2, n_blk - 1))
                cm_next = jnp.max(s_next, axis=0, keepdims=True)
                m_new = jnp.maximum(m, cm)
                p = jnp.exp2(s - m_new)
                alpha = jnp.exp2(m - m_new)
                acc = alpha * acc + pv(j, p.astype(BF16))
                m, s, cm, s_next = m_new, s_next, cm_next, s_next2
            s_ref[0], s_ref[1], cm_ref[...] = s, s_next, cm
            m_ref[...], acc_ref[...] = m, acc
            return carry

        lax.fori_loop(0, n_blk // unroll, body, 0)
        finish()


def _flash(q, k, vt):
    B, H, S, _ = q.shape
    t_kv = vt.shape[-1]
    n_kv = S // t_kv
    tq = min(T_Q, S)
    merge = KV_MERGE if n_kv % KV_MERGE == 0 else 1
    n_blk = n_kv // merge
    unroll = max(u for u in (KV_UNROLL, 4, 2, 1) if n_blk % u == 0)
    return pl.pallas_call(
        functools.partial(_flash_kernel, n_kv=n_kv, t_kv=t_kv, merge=merge, unroll=unroll),
        grid=(B, H, S // tq),
        in_specs=[
            pl.BlockSpec((1, 1, tq, LANES), lambda b, h, i: (b, h, i, 0)),
            pl.BlockSpec((1, 1, S, LANES), lambda b, h, i: (b, h, 0, 0)),
            pl.BlockSpec((1, n_kv, V_EXT, t_kv), lambda b, h, i: (b, 0, h, 0)),
        ],
        out_specs=pl.BlockSpec((1, V_DIM, tq), lambda b, h, i: (b, h, i)),
        out_shape=jax.ShapeDtypeStruct((B, H * V_DIM, S), BF16),
        scratch_shapes=[pltpu.VMEM((1, tq), F32), pltpu.VMEM((V_EXT, tq), F32),
                        pltpu.VMEM((2, merge * t_kv, tq), F32), pltpu.VMEM((1, tq), F32)],
        compiler_params=_cparams(("parallel", "parallel", "parallel")),
        name="flash",
    )(q, k, vt)


def _router(x1, gm_ref, rw_ref, rb_ref, tri_ref, carry_ref, xn_ref, ri_ref, rf_ref, cnt_ref):
    T = x1.shape[0]

    @pl.when(pl.program_id(0) == 0)
    def _():
        carry_ref[...] = jnp.zeros(carry_ref.shape, F32)

    xn = _rms(x1, gm_ref[...])
    _store_row_tiles(xn_ref, 0, xn)
    x_hi = xn.astype(BF16)
    x_lo = (xn - x_hi.astype(F32)).astype(BF16)
    rw = rw_ref[...]
    l_hi = _dot(x_hi, rw)
    l_lo = _dot(x_lo, rw[:, :LANES])
    logits = l_hi[:, :LANES] + l_hi[:, LANES:] + l_lo + rb_ref[...]
    lt = logits.T

    row = lax.broadcasted_iota(I32, (SUBLANES, T), 0).astype(F32)
    neg = -jnp.inf
    lg = jnp.where(row < N_GROUPS, lt[GROUP_COL0:GROUP_COL0 + SUBLANES], neg)
    mg = jnp.max(lg, axis=0, keepdims=True)
    gidx = jnp.min(jnp.where(lg == mg, row, float(SUBLANES)), axis=0, keepdims=True)
    pg_top = 1.0 / jnp.sum(jnp.exp(lg - mg), axis=0, keepdims=True)

    def grp(g):
        return lt[EXPERT_COL0 + g * EXPERTS_PER_GROUP:EXPERT_COL0 + (g + 1) * EXPERTS_PER_GROUP]

    le = jnp.where(gidx == 0.0, grp(0), jnp.where(gidx == 1.0, grp(1), jnp.where(gidx == 2.0, grp(2), grp(3))))
    m1 = jnp.max(le, axis=0, keepdims=True)
    i1 = jnp.min(jnp.where(le == m1, row, float(SUBLANES)), axis=0, keepdims=True)
    le2 = jnp.where(row == i1, neg, le)
    m2 = jnp.max(le2, axis=0, keepdims=True)
    i2 = jnp.min(jnp.where(le2 == m2, row, float(SUBLANES)), axis=0, keepdims=True)
    e21 = jnp.exp(m2 - m1)
    inv = 1.0 / (1.0 + e21)
    g1 = pg_top * inv
    g2 = pg_top * e21 * inv
    e1 = gidx * EXPERTS_PER_GROUP + i1
    e2 = gidx * EXPERTS_PER_GROUP + i2

    erow = lax.broadcasted_iota(I32, (N_EXPERTS, T), 0).astype(F32)
    oh1 = erow == e1
    oh2 = erow == e2
    oh = jnp.where(oh1, 1.0, jnp.where(oh2, 1.0, 0.0))
    before = _dot(oh.astype(BF16), tri_ref[...]) + carry_ref[...]
    r1 = jnp.sum(jnp.where(oh1, before, 0.0), axis=0, keepdims=True)
    r2 = jnp.sum(jnp.where(oh2, before, 0.0), axis=0, keepdims=True)
    carry = carry_ref[...] + jnp.sum(oh, axis=1, keepdims=True)
    carry_ref[...] = carry
    cnt_ref[...] = carry[:, :LANES].astype(I32)

    ri = jnp.where(row == 0.0, e1, jnp.where(row == 1.0, e2, jnp.where(row == 2.0, r1, jnp.where(row == 3.0, r2, 0.0))))
    ri_ref[...] = ri.astype(I32)
    rf_ref[...] = jnp.where(row == 0.0, g1, jnp.where(row == 1.0, g2, 0.0))


def _router_specs(T, n_tok):
    const = lambda shape: pl.BlockSpec(shape, lambda i: (0,) * len(shape))
    in_specs = [const((1, D_MODEL)), const((D_MODEL, 2 * LANES)), const((1, LANES)), const((T, T))]
    out_specs = [
        pl.BlockSpec((T * ROW_TILE, LANES), lambda i: (i, 0)),
        pl.BlockSpec((SUBLANES, T), lambda i: (0, i)),
        pl.BlockSpec((SUBLANES, T), lambda i: (0, i)),
        const((N_EXPERTS, LANES)),
    ]
    out_shape = [
        jax.ShapeDtypeStruct((n_tok * ROW_TILE, LANES), F32),
        jax.ShapeDtypeStruct((SUBLANES, n_tok), I32),
        jax.ShapeDtypeStruct((SUBLANES, n_tok), F32),
        jax.ShapeDtypeStruct((N_EXPERTS, LANES), I32),
    ]
    return in_specs, out_specs, out_shape


def _attn_out_kernel(h_ref, ot_ref, wo_ref, gm_ref, rw_ref, rb_ref, tri_ref,
                     h1_ref, xn_ref, ri_ref, rf_ref, cnt_ref, carry_ref):
    x1 = h_ref[...] + _dot_tn(ot_ref[0], wo_ref[...])
    h1_ref[...] = x1
    _router(x1, gm_ref, rw_ref, rb_ref, tri_ref, carry_ref, xn_ref, ri_ref, rf_ref, cnt_ref)


def _attn_out(h, ot, w, lw):
    n_tok = h.shape[0]
    S = ot.shape[-1]
    T = T_TOK
    nS = S // T
    const = lambda shape: pl.BlockSpec(shape, lambda i: (0,) * len(shape))
    r_in, r_out, r_shape = _router_specs(T, n_tok)
    return pl.pallas_call(
        _attn_out_kernel,
        grid=(n_tok // T,),
        in_specs=[
            pl.BlockSpec((T, D_MODEL), lambda i: (i, 0)),
            pl.BlockSpec((1, N_HEADS * V_DIM, T), lambda i: (i // nS, 0, i % nS)),
            const((N_HEADS * V_DIM, D_MODEL)),
        ] + r_in,
        out_specs=[pl.BlockSpec((T, D_MODEL), lambda i: (i, 0))] + r_out,
        out_shape=[jax.ShapeDtypeStruct((n_tok, D_MODEL), F32)] + r_shape,
        scratch_shapes=[pltpu.VMEM((N_EXPERTS, T), F32)],
        compiler_params=_cparams(("arbitrary",)),
        name="attn_out_router",
    )(h, ot, w["w_o"], lw["moe_norm"], lw["rw"], lw["rb"], w["tri"])


def _conv_kernel(h_ref, hp_ref, hn_ref, gn_ref, win_ref, cw_ref, wout_ref, gm_ref, rw_ref, rb_ref, tri_ref,
                 h1_ref, xn_ref, ri_ref, rf_ref, cnt_ref, carry_ref, *, tiles_per_seq):
    T = h_ref.shape[0]
    i = pl.program_id(0)
    x = h_ref[...]
    gn = gn_ref[...]
    bcu = _dot(_rms(x, gn).astype(BF16), win_ref[...])
    b = bcu[:, :D_MODEL]
    cu = bcu[:, D_MODEL:2 * D_MODEL] * bcu[:, 2 * D_MODEL:]
    halo = jnp.concatenate([hp_ref[...], hn_ref[...]], axis=0)
    hcu = _dot(_rms(halo, gn).astype(BF16), win_ref[:, D_MODEL:])
    hcu = hcu[:, :D_MODEL] * hcu[:, D_MODEL:]
    first = (i % tiles_per_seq) == 0
    last = (i % tiles_per_seq) == tiles_per_seq - 1
    cu_before = jnp.where(first, 0.0, hcu[SUBLANES - 1:SUBLANES])
    cu_after = jnp.where(last, 0.0, hcu[SUBLANES:SUBLANES + 1])
    row = lax.broadcasted_iota(I32, (T, 1), 0)
    prev = jnp.where(row == 0, cu_before, pltpu.roll(cu, 1, 0))
    nxt = jnp.where(row == T - 1, cu_after, pltpu.roll(cu, T - 1, 0))
    cw = cw_ref[...]
    y = cw[0:1] * prev + cw[1:2] * cu + cw[2:3] * nxt
    x1 = x + _dot((b * y).astype(BF16), wout_ref[...])
    h1_ref[...] = x1
    _router(x1, gm_ref, rw_ref, rb_ref, tri_ref, carry_ref, xn_ref, ri_ref, rf_ref, cnt_ref)


def _conv(h, S, w, lw):
    n_tok = h.shape[0]
    T = T_TOK
    nS = S // T
    rows8 = T // SUBLANES
    n8 = n_tok // SUBLANES
    const = lambda shape: pl.BlockSpec(shape, lambda i: (0,) * len(shape))
    r_in, r_out, r_shape = _router_specs(T, n_tok)
    return pl.pallas_call(
        functools.partial(_conv_kernel, tiles_per_seq=nS),
        grid=(n_tok // T,),
        in_specs=[
            pl.BlockSpec((T, D_MODEL), lambda i: (i, 0)),
            pl.BlockSpec((SUBLANES, D_MODEL), lambda i: (jnp.maximum(i * rows8 - 1, 0), 0)),
            pl.BlockSpec((SUBLANES, D_MODEL), lambda i: (jnp.minimum((i + 1) * rows8, n8 - 1), 0)),
            const((1, D_MODEL)), const((D_MODEL, 3 * D_MODEL)), const((SUBLANES, D_MODEL)),
            const((D_MODEL, D_MODEL)),
        ] + r_in,
        out_specs=[pl.BlockSpec((T, D_MODEL), lambda i: (i, 0))] + r_out,
        out_shape=[jax.ShapeDtypeStruct((n_tok, D_MODEL), F32)] + r_shape,
        scratch_shapes=[pltpu.VMEM((N_EXPERTS, T), F32)],
        compiler_params=_cparams(("arbitrary",)),
        name="conv_router",
    )(h, h, h, w["conv_norm"], w["conv_w_in"], w["conv_w"], w["conv_w_out"],
      lw["moe_norm"], lw["rw"], lw["rb"], w["tri"])


def _plan_kernel(cnt_ref, ri_ref, dest_ref, info_ref, *, n_blocks):
    ri = ri_ref[...]
    eid = ri[0:2]
    offset = jnp.zeros(eid.shape, I32)
    blocks_before = jnp.int32(0)
    for e in range(N_EXPERTS):
        nb = (cnt_ref[e, 0] + (EXPERT_BLOCK - 1)) // EXPERT_BLOCK
        offset = jnp.where(eid == e, blocks_before * EXPERT_BLOCK, offset)

        def fill(j, c, e=e, base=blocks_before):
            info_ref[base + j] = e
            return c

        lax.fori_loop(0, nb, fill, 0)
        blocks_before = blocks_before + nb

    def fill_tail(j, c):
        info_ref[j] = N_EXPERTS - 1
        return c

    lax.fori_loop(blocks_before, n_blocks, fill_tail, 0)
    info_ref[n_blocks] = blocks_before
    dest_ref[...] = jnp.zeros(dest_ref.shape, I32)
    dest_ref[0:2, :] = ri[2:4] + offset


def _plan(cnt, ri, n_blocks):
    n_tok = ri.shape[1]
    return pl.pallas_call(
        functools.partial(_plan_kernel, n_blocks=n_blocks),
        in_specs=[pl.BlockSpec(memory_space=pltpu.SMEM), pl.BlockSpec(memory_space=pltpu.VMEM)],
        out_specs=[pl.BlockSpec(memory_space=pltpu.VMEM), pl.BlockSpec(memory_space=pltpu.SMEM)],
        out_shape=[jax.ShapeDtypeStruct((SUBLANES, n_tok), I32),
                   jax.ShapeDtypeStruct((n_blocks + 1,), I32)],
        compiler_params=pltpu.CompilerParams(vmem_limit_bytes=VMEM_LIMIT),
        name="plan",
    )(cnt, ri)


def _store_row_tiles(ref, r0, x):
    for s in range(ROW_TILE):
        ref[pl.ds(r0 * ROW_TILE + s, x.shape[0], stride=ROW_TILE), :] = x[:, s * LANES:(s + 1) * LANES]


def _load_row_tiles(ref, r0, n):
    cols = [ref[pl.ds(r0 * ROW_TILE + s, n, stride=ROW_TILE), :] for s in range(ROW_TILE)]
    return jnp.concatenate(cols, axis=1)


def _row_copy(src_ref, s, dst_ref, d, sem):
    src = src_ref.at[pl.ds(pl.multiple_of(s * ROW_TILE, ROW_TILE), ROW_TILE)]
    dst = dst_ref.at[pl.ds(pl.multiple_of(d * ROW_TILE, ROW_TILE), ROW_TILE)]
    return pltpu.make_async_copy(src, dst, sem)


def _dispatch_kernel(d0_ref, d1_ref, x_ref, xb_in_ref, xb_ref, sem):
    del xb_in_ref
    T = x_ref.shape[0] // ROW_TILE
    base = pl.program_id(0) * T

    def start(t, c):
        _row_copy(x_ref, t, xb_ref, d0_ref[base + t], sem.at[0]).start(priority=0)
        _row_copy(x_ref, t, xb_ref, d1_ref[base + t], sem.at[1]).start(priority=1)
        return c

    lax.fori_loop(0, T, start, 0, unroll=8)

    def wait(t, c):
        _row_copy(x_ref, t, xb_ref, d0_ref[base + t], sem.at[0]).wait()
        _row_copy(x_ref, t, xb_ref, d1_ref[base + t], sem.at[1]).wait()
        return c

    lax.fori_loop(0, T, wait, 0, unroll=8)


def _dispatch(xn, d0, d1, n_rows):
    n_tok = xn.shape[0] // ROW_TILE
    T = T_TOK
    xb0 = jnp.zeros((n_rows * ROW_TILE, LANES), F32)
    return pl.pallas_call(
        _dispatch_kernel,
        grid_spec=pltpu.PrefetchScalarGridSpec(
            num_scalar_prefetch=2,
            grid=(n_tok // T,),
            in_specs=[pl.BlockSpec((T * ROW_TILE, LANES), lambda i, d0, d1: (i, 0)),
                      pl.BlockSpec(memory_space=pl.ANY)],
            out_specs=pl.BlockSpec(memory_space=pl.ANY),
            scratch_shapes=[pltpu.SemaphoreType.DMA((2,))],
        ),
        out_shape=jax.ShapeDtypeStruct((n_rows * ROW_TILE, LANES), F32),
        input_output_aliases={3: 0},
        compiler_params=_cparams(("arbitrary",)),
        name="dispatch",
    )(d0, d1, xn, xb0)


def _expert_kernel(info_ref, xb_ref, *refs, n_blocks):
    yb_ref = refs[-1]
    i = pl.program_id(0)
    blk0 = i * BLOCKS_PER_STEP
    n_used = info_ref[n_blocks]

    def swiglu(r0, n, wg_ref, wu_ref, wd_ref):
        x = _load_row_tiles(xb_ref, r0, n).astype(BF16)
        g = _dot(x, wg_ref[0])
        up = _dot(x, wu_ref[0])
        hmid = (g * jax.nn.sigmoid(g) * up).astype(BF16)
        _store_row_tiles(yb_ref, r0, _dot(hmid, wd_ref[0]))

    one_expert = jnp.logical_and(info_ref[blk0] == info_ref[blk0 + BLOCKS_PER_STEP - 1],
                                 blk0 + BLOCKS_PER_STEP - 1 < n_used)

    @pl.when(one_expert)
    def _():
        swiglu(0, BLOCKS_PER_STEP * EXPERT_BLOCK, *refs[0:3])

    for u in range(BLOCKS_PER_STEP):
        blk = blk0 + u

        @pl.when(jnp.logical_and(jnp.logical_not(one_expert), blk < n_used))
        def _():
            swiglu(u * EXPERT_BLOCK, EXPERT_BLOCK, *refs[3 * u:3 * u + 3])

        @pl.when(blk >= n_used)
        def _():
            n = EXPERT_BLOCK * ROW_TILE
            yb_ref[pl.ds(u * n, n), :] = jnp.zeros((n, LANES), F32)


def _experts(info, xb, lw):
    n_rows = xb.shape[0] // ROW_TILE
    n_blocks = n_rows // EXPERT_BLOCK
    assert n_blocks % BLOCKS_PER_STEP == 0
    step_rows = BLOCKS_PER_STEP * EXPERT_BLOCK * ROW_TILE
    w_specs, w_args = [], []
    for u in range(BLOCKS_PER_STEP):
        pick = lambda i, info, u=u: (info[i * BLOCKS_PER_STEP + u], 0, 0)
        w_specs += [pl.BlockSpec((1, D_MODEL, D_EXPERT), pick), pl.BlockSpec((1, D_MODEL, D_EXPERT), pick),
                    pl.BlockSpec((1, D_EXPERT, D_MODEL), pick)]
        w_args += [lw["w_gate"], lw["w_up"], lw["w_down"]]
    return pl.pallas_call(
        functools.partial(_expert_kernel, n_blocks=n_blocks),
        grid_spec=pltpu.PrefetchScalarGridSpec(
            num_scalar_prefetch=1,
            grid=(n_blocks // BLOCKS_PER_STEP,),
            in_specs=[pl.BlockSpec((step_rows, LANES), lambda i, info: (i, 0))] + w_specs,
            out_specs=pl.BlockSpec((step_rows, LANES), lambda i, info: (i, 0)),
        ),
        out_shape=jax.ShapeDtypeStruct((n_rows * ROW_TILE, LANES), F32),
        compiler_params=_cparams(("arbitrary",)),
        name="experts",
    )(info, xb, *w_args)


def _combine_kernel(d0_ref, d1_ref, h_ref, rf_ref, p_ref, yb_ref, gn_ref, wg_ref, wp_ref, o_ref,
                    y0_ref, y1_ref, sem):
    T = h_ref.shape[0]
    base = pl.program_id(0) * T

    def start(t, c):
        _row_copy(yb_ref, d0_ref[base + t], y0_ref, t, sem.at[0]).start(priority=0)
        _row_copy(yb_ref, d1_ref[base + t], y1_ref, t, sem.at[1]).start(priority=1)
        return c

    lax.fori_loop(0, T, start, 0, unroll=8)

    proj = _dot(p_ref[...].astype(BF16), wp_ref[...])
    gates = jnp.concatenate([rf_ref[...], jnp.zeros((LANES - SUBLANES, T), F32)], axis=0).T

    def wait(t, c):
        _row_copy(yb_ref, d0_ref[base + t], y0_ref, t, sem.at[0]).wait()
        _row_copy(yb_ref, d1_ref[base + t], y1_ref, t, sem.at[1]).wait()
        return c

    lax.fori_loop(0, T, wait, 0, unroll=8)

    y0 = _load_row_tiles(y0_ref, 0, T)
    y1 = _load_row_tiles(y1_ref, 0, T)
    h2 = h_ref[...] + gates[:, 0:1] * y0 + gates[:, 1:2] * y1
    gate = jax.nn.sigmoid(_dot(_rms(h2, gn_ref[...]).astype(BF16), wg_ref[...]))
    o_ref[...] = h2 + gate * proj


def _combine(h, rf, p, yb, d0, d1, lw):
    n_tok = h.shape[0]
    T = T_TOK
    const = lambda shape: pl.BlockSpec(shape, lambda i, d0, d1: (0,) * len(shape))
    return pl.pallas_call(
        _combine_kernel,
        grid_spec=pltpu.PrefetchScalarGridSpec(
            num_scalar_prefetch=2,
            grid=(n_tok // T,),
            in_specs=[
                pl.BlockSpec((T, D_MODEL), lambda i, d0, d1: (i, 0)),
                pl.BlockSpec((SUBLANES, T), lambda i, d0, d1: (0, i)),
                pl.BlockSpec((T, PLE_DIM), lambda i, d0, d1: (i, 0)),
                pl.BlockSpec(memory_space=pl.ANY),
                const((1, D_MODEL)), const((D_MODEL, D_MODEL)), const((PLE_DIM, D_MODEL)),
            ],
            out_specs=pl.BlockSpec((T, D_MODEL), lambda i, d0, d1: (i, 0)),
            scratch_shapes=[pltpu.VMEM((T * ROW_TILE, LANES), F32), pltpu.VMEM((T * ROW_TILE, LANES), F32),
                            pltpu.SemaphoreType.DMA((2,))],
        ),
        out_shape=jax.ShapeDtypeStruct((n_tok, D_MODEL), F32),
        compiler_params=_cparams(("arbitrary",)),
        name="combine_ple",
    )(d0, d1, h, rf, p, yb, lw["ple_norm"], lw["ple_w_gate"], lw["ple_w_proj"])


def _pad_lanes(x, lo, width=LANES):
    n = x.shape[-1]
    pad = [(0, 0)] * (x.ndim - 1) + [(lo, width - lo - n)]
    return jnp.pad(x, pad)


def _rotate_half_cols(x):
    half = QK_ROPE // 2
    return jnp.concatenate([-x[..., half:], x[..., :half]], axis=-1)


def _swap_halves(g):
    half = QK_ROPE // 2
    return jnp.concatenate([g[..., half:], g[..., :half]], axis=-1)


def _prepare(attn_norm, attn_w_in, attn_q_a_norm, attn_kv_a_norm, attn_w_qb, attn_w_kvb, attn_q_gain,
             attn_k_gain, attn_w_o, conv_norm, conv_w_in, conv_w, conv_w_out, moe_norm, router_group_w,
             router_group_b, router_expert_w, router_expert_b, expert_w_gate, expert_w_up, expert_w_down,
             ple_norm, ple_w_gate, ple_w_proj):
    w = {}
    w_in = attn_w_in[0]
    rope_cols = w_in[:, Q_LORA + KV_LORA:]
    w["w_in_ext"] = jnp.concatenate(
        [w_in[:, :Q_LORA + KV_LORA], _pad_lanes(rope_cols, QK_NOPE), _pad_lanes(_rotate_half_cols(rope_cols), QK_NOPE)],
        axis=1).astype(BF16)
    w["attn_norm"] = attn_norm[0][None]
    w["g_qa"] = attn_q_a_norm[0][None]
    w["g_kva"] = attn_kv_a_norm[0][None]
    wqb = attn_w_qb[0].reshape(Q_LORA, N_HEADS, QK_DIM)
    w["wq"] = _pad_lanes(wqb, 0).reshape(Q_LORA, HEAD_PAD).astype(BF16)
    w["wq_sw"] = _pad_lanes(_rotate_half_cols(wqb[..., QK_NOPE:]), QK_NOPE).reshape(Q_LORA, HEAD_PAD).astype(BF16)
    wkvb = attn_w_kvb[0].reshape(KV_LORA, N_HEADS, QK_NOPE + V_DIM)
    w["wk"] = _pad_lanes(wkvb[..., :QK_NOPE], 0).reshape(KV_LORA, HEAD_PAD).astype(BF16)
    w["wvt"] = _pad_lanes(wkvb[..., QK_NOPE:], 0, V_EXT).reshape(KV_LORA, N_HEADS * V_EXT).T.astype(BF16)
    scale = LOG2_E / float(QK_DIM) ** 0.5
    gqs = attn_q_gain[0] * scale
    gk = attn_k_gain[0]
    zero = jnp.zeros((SUBLANES - 2, LANES), F32)
    key_norm_bound = jnp.full((1, LANES), BOUND_MARGIN * float(QK_DIM) ** 0.5, F32) * jnp.max(jnp.abs(gk))
    w["gq"] = jnp.concatenate([_pad_lanes(gqs, 0)[None], _pad_lanes(_swap_halves(gqs[QK_NOPE:]), QK_NOPE)[None],
                               key_norm_bound, zero[1:]])
    w["gk"] = jnp.concatenate([_pad_lanes(gk[:QK_NOPE], 0)[None], _pad_lanes(gk[QK_NOPE:], QK_NOPE)[None],
                               _pad_lanes(_swap_halves(gk[QK_NOPE:]), QK_NOPE)[None], zero[1:]])
    w["w_o"] = attn_w_o[0].astype(BF16)
    w["conv_norm"] = conv_norm[0][None]
    w["conv_w_in"] = conv_w_in[0].astype(BF16)
    w["conv_w"] = jnp.pad(conv_w[0], ((0, SUBLANES - conv_w.shape[1]), (0, 0)))
    w["conv_w_out"] = conv_w_out[0].astype(BF16)
    tri = lax.broadcasted_iota(I32, (T_TOK, T_TOK), 0) < lax.broadcasted_iota(I32, (T_TOK, T_TOK), 1)
    w["tri"] = tri.astype(BF16)
    layers = []
    for i in range(moe_norm.shape[0]):
        rw = jnp.zeros((D_MODEL, LANES), F32)
        rw = rw.at[:, GROUP_COL0:GROUP_COL0 + N_GROUPS].set(router_group_w[i])
        rw = rw.at[:, EXPERT_COL0:EXPERT_COL0 + N_EXPERTS].set(router_expert_w[i])
        rw_hi = rw.astype(BF16)
        rw_lo = (rw - rw_hi.astype(F32)).astype(BF16)
        rb = jnp.zeros((1, LANES), F32)
        rb = rb.at[0, GROUP_COL0:GROUP_COL0 + N_GROUPS].set(router_group_b[i])
        rb = rb.at[0, EXPERT_COL0:EXPERT_COL0 + N_EXPERTS].set(router_expert_b[i])
        layers.append(dict(
            moe_norm=moe_norm[i][None], rw=jnp.concatenate([rw_hi, rw_lo], axis=1), rb=rb,
            w_gate=expert_w_gate[i].astype(BF16), w_up=expert_w_up[i].astype(BF16),
            w_down=expert_w_down[i].astype(BF16),
            ple_norm=ple_norm[i][None], ple_w_gate=ple_w_gate[i].astype(BF16),
            ple_w_proj=ple_w_proj[i].astype(BF16)))
    return w, layers


def _rope_tiles(S):
    pos = jnp.arange(S, dtype=F32)
    inv = ROPE_THETA ** (-jnp.arange(0, QK_ROPE, 2, dtype=F32) / QK_ROPE)
    ang = pos[:, None] * inv[None, :]
    cos, sin = jnp.cos(ang), jnp.sin(ang)
    cos_t = jnp.concatenate([jnp.ones((S, QK_NOPE), F32), cos, cos, jnp.zeros((S, LANES - QK_DIM), F32)], axis=1)
    sin_t = jnp.concatenate([jnp.zeros((S, QK_NOPE), F32), sin, sin, jnp.zeros((S, LANES - QK_DIM), F32)], axis=1)
    return cos_t, sin_t


def _moe_ple(h1, xn, ri, rf, cnt, p, lw):
    n_tok = h1.shape[0]
    n_assign = 2 * n_tok
    n_rows = -(-n_assign // EXPERT_BLOCK) * EXPERT_BLOCK + N_EXPERTS * EXPERT_BLOCK
    dest, info = _plan(cnt, ri, n_rows // EXPERT_BLOCK)
    d0, d1 = dest[0], dest[1]
    xb = _dispatch(xn, d0, d1, n_rows)
    yb = _experts(info, xb, lw)
    return _combine(h1, rf, p, yb, d0, d1, lw)


def _trunk(x, p, w, layers):
    B, S, _ = x.shape
    n_tok = B * S
    cos_t, sin_t = _rope_tiles(S)
    h = x.reshape(n_tok, D_MODEL)
    p = p.reshape(p.shape[0], n_tok, PLE_DIM)
    q, k, vt = _attn_in(x, w, cos_t, sin_t)
    ot = _flash(q, k, vt)
    h1, xn, ri, rf, cnt = _attn_out(h, ot, w, layers[0])
    h = _moe_ple(h1, xn, ri, rf, cnt, p[0], layers[0])
    h1, xn, ri, rf, cnt = _conv(h, S, w, layers[1])
    h = _moe_ple(h1, xn, ri, rf, cnt, p[1], layers[1])
    return h.reshape(B, S, D_MODEL)


def kernel(x_prompt, x_sample, p_prompt, p_sample, attn_norm, attn_w_in, attn_q_a_norm, attn_kv_a_norm, attn_w_qb, attn_w_kvb, attn_q_gain, attn_k_gain, attn_w_o, conv_norm, conv_w_in, conv_w, conv_w_out, moe_norm, router_group_w, router_group_b, router_expert_w, router_expert_b, expert_w_gate, expert_w_up, expert_w_down, ple_norm, ple_w_gate, ple_w_proj):
    assert x_prompt.shape[-1] == D_MODEL and moe_norm.shape[0] == 2
    assert attn_w_in.shape == (1, D_MODEL, Q_LORA + KV_LORA + QK_ROPE)
    assert attn_w_qb.shape == (1, Q_LORA, N_HEADS * QK_DIM)
    assert attn_w_kvb.shape == (1, KV_LORA, N_HEADS * (QK_NOPE + V_DIM))
    assert expert_w_gate.shape[1:] == (N_EXPERTS, D_MODEL, D_EXPERT)
    w, layers = _prepare(attn_norm, attn_w_in, attn_q_a_norm, attn_kv_a_norm, attn_w_qb, attn_w_kvb,
                         attn_q_gain, attn_k_gain, attn_w_o, conv_norm, conv_w_in, conv_w, conv_w_out,
                         moe_norm, router_group_w, router_group_b, router_expert_w, router_expert_b,
                         expert_w_gate, expert_w_up, expert_w_down, ple_norm, ple_w_gate, ple_w_proj)
    return (_trunk(x_prompt, p_prompt, w, layers), _trunk(x_sample, p_sample, w, layers))
```

```python
import functools

import jax
import jax.numpy as jnp
from jax import lax
from jax.experimental import pallas as pl
from jax.experimental.pallas import tpu as pltpu

F32, BF16, I32 = jnp.float32, jnp.bfloat16, jnp.int32

D_MODEL = 1024
N_HEADS = 16
QK_NOPE, QK_ROPE, V_DIM = 64, 32, 64
QK_DIM = QK_NOPE + QK_ROPE
Q_LORA, KV_LORA = 384, 256
N_GROUPS, EXPERTS_PER_GROUP = 4, 8
N_EXPERTS = N_GROUPS * EXPERTS_PER_GROUP
D_EXPERT = 512
PLE_DIM = 256
ROPE_THETA = 10000.0
RMS_EPS = 1e-6
EXPERT_BLOCK = 128

LANES = 128
SUBLANES = 8
VMEM_LIMIT = 56 * 1024 * 1024
ROW_TILE = D_MODEL // LANES

T_ATTN_IN = 256
T_Q = 512
T_TOK = 256
KV_MERGE = 2
KV_UNROLL = 8
BLOCKS_PER_STEP = 4
LOG2_E = 1.4426950408889634
BOUND_MARGIN = 1.02
MIN_DENOMINATOR = 2.0 ** -60

BF16_ROWS = 16
V_EXT = V_DIM + BF16_ROWS
W_IN_EXT = Q_LORA + KV_LORA + 2 * LANES
HEAD_PAD = N_HEADS * LANES
GROUP_COL0, EXPERT_COL0 = 0, SUBLANES


def _cparams(sem):
    return pltpu.CompilerParams(dimension_semantics=sem, vmem_limit_bytes=VMEM_LIMIT)


def _rms(x, g):
    return x * lax.rsqrt(jnp.mean(x * x, axis=-1, keepdims=True) + RMS_EPS) * g


def _dot(a, b):
    return jnp.dot(a, b, preferred_element_type=F32)


def _dot_nt(a, b):
    return lax.dot_general(a, b, (((1,), (1,)), ((), ())), preferred_element_type=F32)


def _dot_tn(a, b):
    return lax.dot_general(a, b, (((0,), (0,)), ((), ())), preferred_element_type=F32)


def _attn_in_kernel(x_ref, gn_ref, win_ref, gqa_ref, gkva_ref, wq_ref, wqs_ref, wk_ref, wvt_ref,
                    gq_ref, gk_ref, cos_ref, sin_ref, q_ref, k_ref, vt_ref):
    x = x_ref[0]
    xn = _rms(x, gn_ref[...]).astype(BF16)
    a = _dot(xn, win_ref[...])
    cq = _rms(a[:, :Q_LORA], gqa_ref[...]).astype(BF16)
    ckv = _rms(a[:, Q_LORA:Q_LORA + KV_LORA], gkva_ref[...]).astype(BF16)
    kr = a[:, Q_LORA + KV_LORA:Q_LORA + KV_LORA + LANES]
    kr_sw = a[:, Q_LORA + KV_LORA + LANES:]
    cosv, sinv = cos_ref[...], sin_ref[...]
    gq, gk = gq_ref[...], gk_ref[...]

    q = _dot(cq, wq_ref[...])
    q_sw = _dot(cq, wqs_ref[...])
    kn = _dot(ckv, wk_ref[...])
    vt = _dot_nt(wvt_ref[...], ckv)
    vrow = lax.broadcasted_iota(I32, vt.shape, 0) % V_EXT
    vt_ref[0, 0] = jnp.where(vrow == V_DIM, 1.0, vt).astype(BF16)

    ssq_rope = jnp.sum(kr * kr, axis=-1, keepdims=True)
    k_rope = kr * (gk[1:2] * cosv) + kr_sw * (gk[2:3] * sinv)
    q_cos, q_sin = gq[0:1] * cosv, gq[1:2] * sinv
    inv_d = 1.0 / QK_DIM
    bound_lane = lax.broadcasted_iota(I32, (x.shape[0], LANES), 1) == QK_DIM
    for h in range(N_HEADS):
        sl = slice(h * LANES, (h + 1) * LANES)
        qh = q[:, sl]
        rq = lax.rsqrt(jnp.sum(qh * qh, axis=-1, keepdims=True) * inv_d + RMS_EPS)
        qt = (qh * q_cos + q_sw[:, sl] * q_sin) * rq
        bound = jnp.sqrt(jnp.sum(qt * qt, axis=-1, keepdims=True)) * gq[2:3]
        q_ref[0, h] = jnp.where(bound_lane, -bound, qt).astype(BF16)
        kh = kn[:, sl]
        rk = lax.rsqrt((jnp.sum(kh * kh, axis=-1, keepdims=True) + ssq_rope) * inv_d + RMS_EPS)
        k_ref[0, h] = jnp.where(bound_lane, 1.0, (kh * gk[0:1] + k_rope) * rk).astype(BF16)


def _attn_in(x, w, cos_t, sin_t):
    B, S, _ = x.shape
    T = T_ATTN_IN
    nS = S // T
    const = lambda shape: pl.BlockSpec(shape, lambda b, j: (0,) * len(shape))
    return pl.pallas_call(
        _attn_in_kernel,
        grid=(B, nS),
        in_specs=[
            pl.BlockSpec((1, T, D_MODEL), lambda b, j: (b, j, 0)),
            const((1, D_MODEL)), const((D_MODEL, W_IN_EXT)), const((1, Q_LORA)), const((1, KV_LORA)),
            const((Q_LORA, HEAD_PAD)), const((Q_LORA, HEAD_PAD)), const((KV_LORA, HEAD_PAD)),
            const((N_HEADS * V_EXT, KV_LORA)), const((SUBLANES, LANES)), const((SUBLANES, LANES)),
            pl.BlockSpec((T, LANES), lambda b, j: (j, 0)),
            pl.BlockSpec((T, LANES), lambda b, j: (j, 0)),
        ],
        out_specs=[
            pl.BlockSpec((1, N_HEADS, T, LANES), lambda b, j: (b, 0, j, 0)),
            pl.BlockSpec((1, N_HEADS, T, LANES), lambda b, j: (b, 0, j, 0)),
            pl.BlockSpec((1, 1, N_HEADS * V_EXT, T), lambda b, j: (b, j, 0, 0)),
        ],
        out_shape=[
            jax.ShapeDtypeStruct((B, N_HEADS, S, LANES), BF16),
            jax.ShapeDtypeStruct((B, N_HEADS, S, LANES), BF16),
            jax.ShapeDtypeStruct((B, nS, N_HEADS * V_EXT, T), BF16),
        ],
        compiler_params=_cparams(("parallel", "parallel")),
        name="attn_in",
    )(x, w["attn_norm"], w["w_in_ext"], w["g_qa"], w["g_kva"], w["wq"], w["wq_sw"], w["wk"], w["wvt"],
      w["gq"], w["gk"], cos_t, sin_t)


def _flash_kernel(q_ref, k_ref, vt_ref, o_ref, m_ref, acc_ref, s_ref, sr_ref, cm_ref, *, n_kv, t_kv, merge, unroll, tq):
    t_blk = merge * t_kv
    n_blk = n_kv // merge
    n_q = q_ref.shape[2] // tq

    def scores(qi, j):
        kb = k_ref[0, 0, pl.ds(pl.multiple_of(j * t_blk, t_blk), t_blk), :]
        q = q_ref[0, 0, pl.ds(pl.multiple_of(qi * tq, tq), tq), :]
        return _dot_nt(kb, q)

    def pv(j, p):
        out = _dot(vt_ref[0, j * merge], p[0:t_kv])
        for i in range(1, merge):
            out = out + _dot(vt_ref[0, j * merge + i], p[i * t_kv:(i + 1) * t_kv])
        return out

    def finish(qi, acc):
        o_ref[0, qi] = (acc[0:V_DIM] * (1.0 / acc[V_DIM:V_DIM + 1])).astype(BF16)

    def running_max_tile(qi):
        m_ref[...] = jnp.full(m_ref.shape, -jnp.inf, F32)
        acc_ref[...] = jnp.zeros(acc_ref.shape, F32)
        s0 = scores(qi, 0)
        sr_ref[0] = s0
        cm_ref[...] = jnp.max(s0, axis=0, keepdims=True)
        sr_ref[1] = scores(qi, min(1, n_blk - 1))

        def body(it, carry):
            s, s_next, cm = sr_ref[0], sr_ref[1], cm_ref[...]
            m, acc = m_ref[...], acc_ref[...]
            for u in range(unroll):
                j = it * unroll + u
                s_next2 = scores(qi, jnp.minimum(j + 2, n_blk - 1))
                cm_next = jnp.max(s_next, axis=0, keepdims=True)
                m_new = jnp.maximum(m, cm)
                p = jnp.exp2(s - m_new)
                alpha = jnp.exp2(m - m_new)
                acc = alpha * acc + pv(j, p.astype(BF16))
                m, s, cm, s_next = m_new, s_next, cm_next, s_next2
            sr_ref[0], sr_ref[1], cm_ref[...] = s, s_next, cm
            m_ref[...], acc_ref[...] = m, acc
            return carry

        lax.fori_loop(0, n_blk // unroll, body, 0)
        finish(qi, acc_ref[...])

    assert n_blk >= 2
    s_ref[0] = scores(0, 0)
    s_ref[1] = scores(0, 1)

    def q_tile(qi, carry):
        s, s_next = s_ref[0], s_ref[1]
        qn = jnp.minimum(qi + 1, n_q - 1)
        acc = jnp.zeros((V_EXT, tq), F32)
        for j in range(n_blk):
            s_next2 = scores(qi, j + 2) if j + 2 < n_blk else scores(qn, j + 2 - n_blk)
            acc = acc + pv(j, jnp.exp2(s).astype(BF16))
            s, s_next = s_next, s_next2
        s_ref[0], s_ref[1] = s, s_next
        acc_ref[...] = acc
        safe = jnp.min(acc_ref[V_DIM:V_DIM + 1]) >= MIN_DENOMINATOR

        @pl.when(safe)
        def _():
            finish(qi, acc_ref[...])

        @pl.when(jnp.logical_not(safe))
        def _():
            running_max_tile(qi)

        return carry

    lax.fori_loop(0, n_q, q_tile, 0)


def _flash(q, k, vt):
    B, H, S, _ = q.shape
    t_kv = vt.shape[-1]
    n_kv = S // t_kv
    tq = min(T_Q, S)
    n_q = S // tq
    merge = KV_MERGE if n_kv % KV_MERGE == 0 else 1
    n_blk = n_kv // merge
    unroll = max(u for u in (KV_UNROLL, 4, 2, 1) if n_blk % u == 0)
    return pl.pallas_call(
        functools.partial(_flash_kernel, n_kv=n_kv, t_kv=t_kv, merge=merge, unroll=unroll, tq=tq),
        grid=(B, H),
        in_specs=[
            pl.BlockSpec((1, 1, S, LANES), lambda b, h: (b, h, 0, 0)),
            pl.BlockSpec((1, 1, S, LANES), lambda b, h: (b, h, 0, 0)),
            pl.BlockSpec((1, n_kv, V_EXT, t_kv), lambda b, h: (b, 0, h, 0)),
        ],
        out_specs=pl.BlockSpec((1, n_q, V_DIM, tq), lambda b, h: (b, 0, h, 0)),
        out_shape=jax.ShapeDtypeStruct((B, n_q, H * V_DIM, tq), BF16),
        scratch_shapes=[pltpu.VMEM((1, tq), F32), pltpu.VMEM((V_EXT, tq), F32),
                        pltpu.VMEM((2, merge * t_kv, tq), F32), pltpu.VMEM((2, merge * t_kv, tq), F32),
                        pltpu.VMEM((1, tq), F32)],
        compiler_params=_cparams(("parallel", "parallel")),
        name="flash",
    )(q, k, vt)


def _router(x1, gm_ref, rw_ref, rb_ref, tri_ref, carry_ref, xn_ref, ri_ref, rf_ref, cnt_ref):
    T = x1.shape[0]

    @pl.when(pl.program_id(0) == 0)
    def _():
        carry_ref[...] = jnp.zeros(carry_ref.shape, F32)

    xn = _rms(x1, gm_ref[...])
    _store_row_tiles(xn_ref, 0, xn)
    x_hi = xn.astype(BF16)
    x_lo = (xn - x_hi.astype(F32)).astype(BF16)
    rw = rw_ref[...]
    l_hi = _dot(x_hi, rw)
    l_lo = _dot(x_lo, rw[:, :LANES])
    logits = l_hi[:, :LANES] + l_hi[:, LANES:] + l_lo + rb_ref[...]
    lt = logits.T

    row = lax.broadcasted_iota(I32, (SUBLANES, T), 0).astype(F32)
    neg = -jnp.inf
    lg = jnp.where(row < N_GROUPS, lt[GROUP_COL0:GROUP_COL0 + SUBLANES], neg)
    mg = jnp.max(lg, axis=0, keepdims=True)
    gidx = jnp.min(jnp.where(lg == mg, row, float(SUBLANES)), axis=0, keepdims=True)
    pg_top = 1.0 / jnp.sum(jnp.exp(lg - mg), axis=0, keepdims=True)

    def grp(g):
        return lt[EXPERT_COL0 + g * EXPERTS_PER_GROUP:EXPERT_COL0 + (g + 1) * EXPERTS_PER_GROUP]

    le = jnp.where(gidx == 0.0, grp(0), jnp.where(gidx == 1.0, grp(1), jnp.where(gidx == 2.0, grp(2), grp(3))))
    m1 = jnp.max(le, axis=0, keepdims=True)
    i1 = jnp.min(jnp.where(le == m1, row, float(SUBLANES)), axis=0, keepdims=True)
    le2 = jnp.where(row == i1, neg, le)
    m2 = jnp.max(le2, axis=0, keepdims=True)
    i2 = jnp.min(jnp.where(le2 == m2, row, float(SUBLANES)), axis=0, keepdims=True)
    e21 = jnp.exp(m2 - m1)
    inv = 1.0 / (1.0 + e21)
    g1 = pg_top * inv
    g2 = pg_top * e21 * inv
    e1 = gidx * EXPERTS_PER_GROUP + i1
    e2 = gidx * EXPERTS_PER_GROUP + i2

    erow = lax.broadcasted_iota(I32, (N_EXPERTS, T), 0).astype(F32)
    oh1 = erow == e1
    oh2 = erow == e2
    oh = jnp.where(oh1, 1.0, jnp.where(oh2, 1.0, 0.0))
    before = _dot(oh.astype(BF16), tri_ref[...]) + carry_ref[...]
    r1 = jnp.sum(jnp.where(oh1, before, 0.0), axis=0, keepdims=True)
    r2 = jnp.sum(jnp.where(oh2, before, 0.0), axis=0, keepdims=True)
    carry = carry_ref[...] + jnp.sum(oh, axis=1, keepdims=True)
    carry_ref[...] = carry
    cnt_ref[...] = carry[:, :LANES].astype(I32)

    ri = jnp.where(row == 0.0, e1, jnp.where(row == 1.0, e2, jnp.where(row == 2.0, r1, jnp.where(row == 3.0, r2, 0.0))))
    ri_ref[...] = ri.astype(I32)
    rf_ref[...] = jnp.where(row == 0.0, g1, jnp.where(row == 1.0, g2, 0.0))


def _router_specs(T, n_tok):
    const = lambda shape: pl.BlockSpec(shape, lambda i: (0,) * len(shape))
    in_specs = [const((1, D_MODEL)), const((D_MODEL, 2 * LANES)), const((1, LANES)), const((T, T))]
    out_specs = [
        pl.BlockSpec((T * ROW_TILE, LANES), lambda i: (i, 0)),
        pl.BlockSpec((SUBLANES, T), lambda i: (0, i)),
        pl.BlockSpec((SUBLANES, T), lambda i: (0, i)),
        const((N_EXPERTS, LANES)),
    ]
    out_shape = [
        jax.ShapeDtypeStruct((n_tok * ROW_TILE, LANES), F32),
        jax.ShapeDtypeStruct((SUBLANES, n_tok), I32),
        jax.ShapeDtypeStruct((SUBLANES, n_tok), F32),
        jax.ShapeDtypeStruct((N_EXPERTS, LANES), I32),
    ]
    return in_specs, out_specs, out_shape


def _attn_out_kernel(h_ref, ot_ref, wo_ref, gm_ref, rw_ref, rb_ref, tri_ref,
                     h1_ref, xn_ref, ri_ref, rf_ref, cnt_ref, carry_ref):
    x1 = h_ref[...] + _dot_tn(ot_ref[0, 0], wo_ref[...])
    h1_ref[...] = x1
    _router(x1, gm_ref, rw_ref, rb_ref, tri_ref, carry_ref, xn_ref, ri_ref, rf_ref, cnt_ref)


def _attn_out(h, ot, w, lw):
    n_tok = h.shape[0]
    _, n_q, _, tq = ot.shape
    T = T_TOK
    assert tq % T == 0
    per_q = tq // T
    per_b = n_q * per_q
    const = lambda shape: pl.BlockSpec(shape, lambda i: (0,) * len(shape))
    r_in, r_out, r_shape = _router_specs(T, n_tok)
    return pl.pallas_call(
        _attn_out_kernel,
        grid=(n_tok // T,),
        in_specs=[
            pl.BlockSpec((T, D_MODEL), lambda i: (i, 0)),
            pl.BlockSpec((1, 1, N_HEADS * V_DIM, T),
                         lambda i: (i // per_b, (i % per_b) // per_q, 0, i % per_q)),
            const((N_HEADS * V_DIM, D_MODEL)),
        ] + r_in,
        out_specs=[pl.BlockSpec((T, D_MODEL), lambda i: (i, 0))] + r_out,
        out_shape=[jax.ShapeDtypeStruct((n_tok, D_MODEL), F32)] + r_shape,
        scratch_shapes=[pltpu.VMEM((N_EXPERTS, T), F32)],
        compiler_params=_cparams(("arbitrary",)),
        name="attn_out_router",
    )(h, ot, w["w_o"], lw["moe_norm"], lw["rw"], lw["rb"], w["tri"])


def _conv_kernel(h_ref, hp_ref, hn_ref, gn_ref, win_ref, cw_ref, wout_ref, gm_ref, rw_ref, rb_ref, tri_ref,
                 h1_ref, xn_ref, ri_ref, rf_ref, cnt_ref, carry_ref, *, tiles_per_seq):
    T = h_ref.shape[0]
    i = pl.program_id(0)
    x = h_ref[...]
    gn = gn_ref[...]
    bcu = _dot(_rms(x, gn).astype(BF16), win_ref[...])
    b = bcu[:, :D_MODEL]
    cu = bcu[:, D_MODEL:2 * D_MODEL] * bcu[:, 2 * D_MODEL:]
    halo = jnp.concatenate([hp_ref[...], hn_ref[...]], axis=0)
    hcu = _dot(_rms(halo, gn).astype(BF16), win_ref[:, D_MODEL:])
    hcu = hcu[:, :D_MODEL] * hcu[:, D_MODEL:]
    first = (i % tiles_per_seq) == 0
    last = (i % tiles_per_seq) == tiles_per_seq - 1
    cu_before = jnp.where(first, 0.0, hcu[SUBLANES - 1:SUBLANES])
    cu_after = jnp.where(last, 0.0, hcu[SUBLANES:SUBLANES + 1])
    row = lax.broadcasted_iota(I32, (T, 1), 0)
    prev = jnp.where(row == 0, cu_before, pltpu.roll(cu, 1, 0))
    nxt = jnp.where(row == T - 1, cu_after, pltpu.roll(cu, T - 1, 0))
    cw = cw_ref[...]
    y = cw[0:1] * prev + cw[1:2] * cu + cw[2:3] * nxt
    x1 = x + _dot((b * y).astype(BF16), wout_ref[...])
    h1_ref[...] = x1
    _router(x1, gm_ref, rw_ref, rb_ref, tri_ref, carry_ref, xn_ref, ri_ref, rf_ref, cnt_ref)


def _conv(h, S, w, lw):
    n_tok = h.shape[0]
    T = T_TOK
    nS = S // T
    rows8 = T // SUBLANES
    n8 = n_tok // SUBLANES
    const = lambda shape: pl.BlockSpec(shape, lambda i: (0,) * len(shape))
    r_in, r_out, r_shape = _router_specs(T, n_tok)
    return pl.pallas_call(
        functools.partial(_conv_kernel, tiles_per_seq=nS),
        grid=(n_tok // T,),
        in_specs=[
            pl.BlockSpec((T, D_MODEL), lambda i: (i, 0)),
            pl.BlockSpec((SUBLANES, D_MODEL), lambda i: (jnp.maximum(i * rows8 - 1, 0), 0)),
            pl.BlockSpec((SUBLANES, D_MODEL), lambda i: (jnp.minimum((i + 1) * rows8, n8 - 1), 0)),
            const((1, D_MODEL)), const((D_MODEL, 3 * D_MODEL)), const((SUBLANES, D_MODEL)),
            const((D_MODEL, D_MODEL)),
        ] + r_in,
        out_specs=[pl.BlockSpec((T, D_MODEL), lambda i: (i, 0))] + r_out,
        out_shape=[jax.ShapeDtypeStruct((n_tok, D_MODEL), F32)] + r_shape,
        scratch_shapes=[pltpu.VMEM((N_EXPERTS, T), F32)],
        compiler_params=_cparams(("arbitrary",)),
        name="conv_router",
    )(h, h, h, w["conv_norm"], w["conv_w_in"], w["conv_w"], w["conv_w_out"],
      lw["moe_norm"], lw["rw"], lw["rb"], w["tri"])


def _plan_kernel(cnt_ref, ri_ref, dest_ref, info_ref, *, n_blocks):
    ri = ri_ref[...]
    eid = ri[0:2]
    offset = jnp.zeros(eid.shape, I32)
    blocks_before = jnp.int32(0)
    for e in range(N_EXPERTS):
        nb = (cnt_ref[e, 0] + (EXPERT_BLOCK - 1)) // EXPERT_BLOCK
        offset = jnp.where(eid == e, blocks_before * EXPERT_BLOCK, offset)

        def fill(j, c, e=e, base=blocks_before):
            info_ref[base + j] = e
            return c

        lax.fori_loop(0, nb, fill, 0)
        blocks_before = blocks_before + nb

    def fill_tail(j, c):
        info_ref[j] = N_EXPERTS - 1
        return c

    lax.fori_loop(blocks_before, n_blocks, fill_tail, 0)
    info_ref[n_blocks] = blocks_before
    dest_ref[...] = jnp.zeros(dest_ref.shape, I32)
    dest_ref[0:2, :] = ri[2:4] + offset


def _plan(cnt, ri, n_blocks):
    n_tok = ri.shape[1]
    return pl.pallas_call(
        functools.partial(_plan_kernel, n_blocks=n_blocks),
        in_specs=[pl.BlockSpec(memory_space=pltpu.SMEM), pl.BlockSpec(memory_space=pltpu.VMEM)],
        out_specs=[pl.BlockSpec(memory_space=pltpu.VMEM), pl.BlockSpec(memory_space=pltpu.SMEM)],
        out_shape=[jax.ShapeDtypeStruct((SUBLANES, n_tok), I32),
                   jax.ShapeDtypeStruct((n_blocks + 1,), I32)],
        compiler_params=pltpu.CompilerParams(vmem_limit_bytes=VMEM_LIMIT),
        name="plan",
    )(cnt, ri)


def _store_row_tiles(ref, r0, x):
    for s in range(ROW_TILE):
        ref[pl.ds(r0 * ROW_TILE + s, x.shape[0], stride=ROW_TILE), :] = x[:, s * LANES:(s + 1) * LANES]


def _load_row_tiles(ref, r0, n):
    cols = [ref[pl.ds(r0 * ROW_TILE + s, n, stride=ROW_TILE), :] for s in range(ROW_TILE)]
    return jnp.concatenate(cols, axis=1)


def _row_copy(src_ref, s, dst_ref, d, sem):
    src = src_ref.at[pl.ds(pl.multiple_of(s * ROW_TILE, ROW_TILE), ROW_TILE)]
    dst = dst_ref.at[pl.ds(pl.multiple_of(d * ROW_TILE, ROW_TILE), ROW_TILE)]
    return pltpu.make_async_copy(src, dst, sem)


def _dispatch_kernel(d0_ref, d1_ref, x_ref, xb_in_ref, xb_ref, sem):
    del xb_in_ref
    T = x_ref.shape[0] // ROW_TILE
    base = pl.program_id(0) * T

    def start(t, c):
        _row_copy(x_ref, t, xb_ref, d0_ref[base + t], sem.at[0]).start(priority=0)
        _row_copy(x_ref, t, xb_ref, d1_ref[base + t], sem.at[1]).start(priority=1)
        return c

    lax.fori_loop(0, T, start, 0, unroll=8)

    def wait(t, c):
        _row_copy(x_ref, t, xb_ref, d0_ref[base + t], sem.at[0]).wait()
        _row_copy(x_ref, t, xb_ref, d1_ref[base + t], sem.at[1]).wait()
        return c

    lax.fori_loop(0, T, wait, 0, unroll=8)


def _dispatch(xn, d0, d1, n_rows):
    n_tok = xn.shape[0] // ROW_TILE
    T = T_TOK
    xb0 = jnp.zeros((n_rows * ROW_TILE, LANES), F32)
    return pl.pallas_call(
        _dispatch_kernel,
        grid_spec=pltpu.PrefetchScalarGridSpec(
            num_scalar_prefetch=2,
            grid=(n_tok // T,),
            in_specs=[pl.BlockSpec((T * ROW_TILE, LANES), lambda i, d0, d1: (i, 0)),
                      pl.BlockSpec(memory_space=pl.ANY)],
            out_specs=pl.BlockSpec(memory_space=pl.ANY),
            scratch_shapes=[pltpu.SemaphoreType.DMA((2,))],
        ),
        out_shape=jax.ShapeDtypeStruct((n_rows * ROW_TILE, LANES), F32),
        input_output_aliases={3: 0},
        compiler_params=_cparams(("arbitrary",)),
        name="dispatch",
    )(d0, d1, xn, xb0)


def _expert_kernel(info_ref, xb_ref, *refs, n_blocks):
    yb_ref = refs[-1]
    i = pl.program_id(0)
    blk0 = i * BLOCKS_PER_STEP
    n_used = info_ref[n_blocks]

    def swiglu(r0, n, wg_ref, wu_ref, wd_ref):
        x = _load_row_tiles(xb_ref, r0, n).astype(BF16)
        g = _dot(x, wg_ref[0])
        up = _dot(x, wu_ref[0])
        hmid = (g * jax.nn.sigmoid(g) * up).astype(BF16)
        _store_row_tiles(yb_ref, r0, _dot(hmid, wd_ref[0]))

    one_expert = jnp.logical_and(info_ref[blk0] == info_ref[blk0 + BLOCKS_PER_STEP - 1],
                                 blk0 + BLOCKS_PER_STEP - 1 < n_used)

    @pl.when(one_expert)
    def _():
        swiglu(0, BLOCKS_PER_STEP * EXPERT_BLOCK, *refs[0:3])

    for u in range(BLOCKS_PER_STEP):
        blk = blk0 + u

        @pl.when(jnp.logical_and(jnp.logical_not(one_expert), blk < n_used))
        def _():
            swiglu(u * EXPERT_BLOCK, EXPERT_BLOCK, *refs[3 * u:3 * u + 3])

        @pl.when(blk >= n_used)
        def _():
            n = EXPERT_BLOCK * ROW_TILE
            yb_ref[pl.ds(u * n, n), :] = jnp.zeros((n, LANES), F32)


def _experts(info, xb, lw):
    n_rows = xb.shape[0] // ROW_TILE
    n_blocks = n_rows // EXPERT_BLOCK
    assert n_blocks % BLOCKS_PER_STEP == 0
    step_rows = BLOCKS_PER_STEP * EXPERT_BLOCK * ROW_TILE
    w_specs, w_args = [], []
    for u in range(BLOCKS_PER_STEP):
        pick = lambda i, info, u=u: (info[i * BLOCKS_PER_STEP + u], 0, 0)
        w_specs += [pl.BlockSpec((1, D_MODEL, D_EXPERT), pick), pl.BlockSpec((1, D_MODEL, D_EXPERT), pick),
                    pl.BlockSpec((1, D_EXPERT, D_MODEL), pick)]
        w_args += [lw["w_gate"], lw["w_up"], lw["w_down"]]
    return pl.pallas_call(
        functools.partial(_expert_kernel, n_blocks=n_blocks),
        grid_spec=pltpu.PrefetchScalarGridSpec(
            num_scalar_prefetch=1,
            grid=(n_blocks // BLOCKS_PER_STEP,),
            in_specs=[pl.BlockSpec((step_rows, LANES), lambda i, info: (i, 0))] + w_specs,
            out_specs=pl.BlockSpec((step_rows, LANES), lambda i, info: (i, 0)),
        ),
        out_shape=jax.ShapeDtypeStruct((n_rows * ROW_TILE, LANES), F32),
        compiler_params=_cparams(("arbitrary",)),
        name="experts",
    )(info, xb, *w_args)


def _combine_kernel(d0_ref, d1_ref, h_ref, rf_ref, p_ref, yb_ref, gn_ref, wg_ref, wp_ref, o_ref,
                    y0_ref, y1_ref, sem):
    T = h_ref.shape[0]
    base = pl.program_id(0) * T

    def start(t, c):
        _row_copy(yb_ref, d0_ref[base + t], y0_ref, t, sem.at[0]).start(priority=0)
        _row_copy(yb_ref, d1_ref[base + t], y1_ref, t, sem.at[1]).start(priority=1)
        return c

    lax.fori_loop(0, T, start, 0, unroll=8)

    proj = _dot(p_ref[...].astype(BF16), wp_ref[...])
    gates = jnp.concatenate([rf_ref[...], jnp.zeros((LANES - SUBLANES, T), F32)], axis=0).T

    def wait(t, c):
        _row_copy(yb_ref, d0_ref[base + t], y0_ref, t, sem.at[0]).wait()
        _row_copy(yb_ref, d1_ref[base + t], y1_ref, t, sem.at[1]).wait()
        return c

    lax.fori_loop(0, T, wait, 0, unroll=8)

    y0 = _load_row_tiles(y0_ref, 0, T)
    y1 = _load_row_tiles(y1_ref, 0, T)
    h2 = h_ref[...] + gates[:, 0:1] * y0 + gates[:, 1:2] * y1
    gate = jax.nn.sigmoid(_dot(_rms(h2, gn_ref[...]).astype(BF16), wg_ref[...]))
    o_ref[...] = h2 + gate * proj


def _combine(h, rf, p, yb, d0, d1, lw):
    n_tok = h.shape[0]
    T = T_TOK
    const = lambda shape: pl.BlockSpec(shape, lambda i, d0, d1: (0,) * len(shape))
    return pl.pallas_call(
        _combine_kernel,
        grid_spec=pltpu.PrefetchScalarGridSpec(
            num_scalar_prefetch=2,
            grid=(n_tok // T,),
            in_specs=[
                pl.BlockSpec((T, D_MODEL), lambda i, d0, d1: (i, 0)),
                pl.BlockSpec((SUBLANES, T), lambda i, d0, d1: (0, i)),
                pl.BlockSpec((T, PLE_DIM), lambda i, d0, d1: (i, 0)),
                pl.BlockSpec(memory_space=pl.ANY),
                const((1, D_MODEL)), const((D_MODEL, D_MODEL)), const((PLE_DIM, D_MODEL)),
            ],
            out_specs=pl.BlockSpec((T, D_MODEL), lambda i, d0, d1: (i, 0)),
            scratch_shapes=[pltpu.VMEM((T * ROW_TILE, LANES), F32), pltpu.VMEM((T * ROW_TILE, LANES), F32),
                            pltpu.SemaphoreType.DMA((2,))],
        ),
        out_shape=jax.ShapeDtypeStruct((n_tok, D_MODEL), F32),
        compiler_params=_cparams(("arbitrary",)),
        name="combine_ple",
    )(d0, d1, h, rf, p, yb, lw["ple_norm"], lw["ple_w_gate"], lw["ple_w_proj"])


def _pad_lanes(x, lo, width=LANES):
    n = x.shape[-1]
    pad = [(0, 0)] * (x.ndim - 1) + [(lo, width - lo - n)]
    return jnp.pad(x, pad)


def _rotate_half_cols(x):
    half = QK_ROPE // 2
    return jnp.concatenate([-x[..., half:], x[..., :half]], axis=-1)


def _swap_halves(g):
    half = QK_ROPE // 2
    return jnp.concatenate([g[..., half:], g[..., :half]], axis=-1)


def _prepare(attn_norm, attn_w_in, attn_q_a_norm, attn_kv_a_norm, attn_w_qb, attn_w_kvb, attn_q_gain,
             attn_k_gain, attn_w_o, conv_norm, conv_w_in, conv_w, conv_w_out, moe_norm, router_group_w,
             router_group_b, router_expert_w, router_expert_b, expert_w_gate, expert_w_up, expert_w_down,
             ple_norm, ple_w_gate, ple_w_proj):
    w = {}
    w_in = attn_w_in[0]
    rope_cols = w_in[:, Q_LORA + KV_LORA:]
    w["w_in_ext"] = jnp.concatenate(
        [w_in[:, :Q_LORA + KV_LORA], _pad_lanes(rope_cols, QK_NOPE), _pad_lanes(_rotate_half_cols(rope_cols), QK_NOPE)],
        axis=1).astype(BF16)
    w["attn_norm"] = attn_norm[0][None]
    w["g_qa"] = attn_q_a_norm[0][None]
    w["g_kva"] = attn_kv_a_norm[0][None]
    wqb = attn_w_qb[0].reshape(Q_LORA, N_HEADS, QK_DIM)
    w["wq"] = _pad_lanes(wqb, 0).reshape(Q_LORA, HEAD_PAD).astype(BF16)
    w["wq_sw"] = _pad_lanes(_rotate_half_cols(wqb[..., QK_NOPE:]), QK_NOPE).reshape(Q_LORA, HEAD_PAD).astype(BF16)
    wkvb = attn_w_kvb[0].reshape(KV_LORA, N_HEADS, QK_NOPE + V_DIM)
    w["wk"] = _pad_lanes(wkvb[..., :QK_NOPE], 0).reshape(KV_LORA, HEAD_PAD).astype(BF16)
    w["wvt"] = _pad_lanes(wkvb[..., QK_NOPE:], 0, V_EXT).reshape(KV_LORA, N_HEADS * V_EXT).T.astype(BF16)
    scale = LOG2_E / float(QK_DIM) ** 0.5
    gqs = attn_q_gain[0] * scale
    gk = attn_k_gain[0]
    zero = jnp.zeros((SUBLANES - 2, LANES), F32)
    key_norm_bound = jnp.full((1, LANES), BOUND_MARGIN * float(QK_DIM) ** 0.5, F32) * jnp.max(jnp.abs(gk))
    w["gq"] = jnp.concatenate([_pad_lanes(gqs, 0)[None], _pad_lanes(_swap_halves(gqs[QK_NOPE:]), QK_NOPE)[None],
                               key_norm_bound, zero[1:]])
    w["gk"] = jnp.concatenate([_pad_lanes(gk[:QK_NOPE], 0)[None], _pad_lanes(gk[QK_NOPE:], QK_NOPE)[None],
                               _pad_lanes(_swap_halves(gk[QK_NOPE:]), QK_NOPE)[None], zero[1:]])
    w["w_o"] = attn_w_o[0].astype(BF16)
    w["conv_norm"] = conv_norm[0][None]
    w["conv_w_in"] = conv_w_in[0].astype(BF16)
    w["conv_w"] = jnp.pad(conv_w[0], ((0, SUBLANES - conv_w.shape[1]), (0, 0)))
    w["conv_w_out"] = conv_w_out[0].astype(BF16)
    tri = lax.broadcasted_iota(I32, (T_TOK, T_TOK), 0) < lax.broadcasted_iota(I32, (T_TOK, T_TOK), 1)
    w["tri"] = tri.astype(BF16)
    layers = []
    for i in range(moe_norm.shape[0]):
        rw = jnp.zeros((D_MODEL, LANES), F32)
        rw = rw.at[:, GROUP_COL0:GROUP_COL0 + N_GROUPS].set(router_group_w[i])
        rw = rw.at[:, EXPERT_COL0:EXPERT_COL0 + N_EXPERTS].set(router_expert_w[i])
        rw_hi = rw.astype(BF16)
        rw_lo = (rw - rw_hi.astype(F32)).astype(BF16)
        rb = jnp.zeros((1, LANES), F32)
        rb = rb.at[0, GROUP_COL0:GROUP_COL0 + N_GROUPS].set(router_group_b[i])
        rb = rb.at[0, EXPERT_COL0:EXPERT_COL0 + N_EXPERTS].set(router_expert_b[i])
        layers.append(dict(
            moe_norm=moe_norm[i][None], rw=jnp.concatenate([rw_hi, rw_lo], axis=1), rb=rb,
            w_gate=expert_w_gate[i].astype(BF16), w_up=expert_w_up[i].astype(BF16),
            w_down=expert_w_down[i].astype(BF16),
            ple_norm=ple_norm[i][None], ple_w_gate=ple_w_gate[i].astype(BF16),
            ple_w_proj=ple_w_proj[i].astype(BF16)))
    return w, layers


def _rope_tiles(S):
    pos = jnp.arange(S, dtype=F32)
    inv = ROPE_THETA ** (-jnp.arange(0, QK_ROPE, 2, dtype=F32) / QK_ROPE)
    ang = pos[:, None] * inv[None, :]
    cos, sin = jnp.cos(ang), jnp.sin(ang)
    cos_t = jnp.concatenate([jnp.ones((S, QK_NOPE), F32), cos, cos, jnp.zeros((S, LANES - QK_DIM), F32)], axis=1)
    sin_t = jnp.concatenate([jnp.zeros((S, QK_NOPE), F32), sin, sin, jnp.zeros((S, LANES - QK_DIM), F32)], axis=1)
    return cos_t, sin_t


def _moe_ple(h1, xn, ri, rf, cnt, p, lw):
    n_tok = h1.shape[0]
    n_assign = 2 * n_tok
    n_rows = -(-n_assign // EXPERT_BLOCK) * EXPERT_BLOCK + N_EXPERTS * EXPERT_BLOCK
    dest, info = _plan(cnt, ri, n_rows // EXPERT_BLOCK)
    d0, d1 = dest[0], dest[1]
    xb = _dispatch(xn, d0, d1, n_rows)
    yb = _experts(info, xb, lw)
    return _combine(h1, rf, p, yb, d0, d1, lw)


def _trunk(x, p, w, layers):
    B, S, _ = x.shape
    n_tok = B * S
    cos_t, sin_t = _rope_tiles(S)
    h = x.reshape(n_tok, D_MODEL)
    p = p.reshape(p.shape[0], n_tok, PLE_DIM)
    q, k, vt = _attn_in(x, w, cos_t, sin_t)
    ot = _flash(q, k, vt)
    h1, xn, ri, rf, cnt = _attn_out(h, ot, w, layers[0])
    h = _moe_ple(h1, xn, ri, rf, cnt, p[0], layers[0])
    h1, xn, ri, rf, cnt = _conv(h, S, w, layers[1])
    h = _moe_ple(h1, xn, ri, rf, cnt, p[1], layers[1])
    return h.reshape(B, S, D_MODEL)


def kernel(x_prompt, x_sample, p_prompt, p_sample, attn_norm, attn_w_in, attn_q_a_norm, attn_kv_a_norm, attn_w_qb, attn_w_kvb, attn_q_gain, attn_k_gain, attn_w_o, conv_norm, conv_w_in, conv_w, conv_w_out, moe_norm, router_group_w, router_group_b, router_expert_w, router_expert_b, expert_w_gate, expert_w_up, expert_w_down, ple_norm, ple_w_gate, ple_w_proj):
    assert x_prompt.shape[-1] == D_MODEL and moe_norm.shape[0] == 2
    assert attn_w_in.shape == (1, D_MODEL, Q_LORA + KV_LORA + QK_ROPE)
    assert attn_w_qb.shape == (1, Q_LORA, N_HEADS * QK_DIM)
    assert attn_w_kvb.shape == (1, KV_LORA, N_HEADS * (QK_NOPE + V_DIM))
    assert expert_w_gate.shape[1:] == (N_EXPERTS, D_MODEL, D_EXPERT)
    w, layers = _prepare(attn_norm, attn_w_in, attn_q_a_norm, attn_kv_a_norm, attn_w_qb, attn_w_kvb,
                         attn_q_gain, attn_k_gain, attn_w_o, conv_norm, conv_w_in, conv_w, conv_w_out,
                         moe_norm, router_group_w, router_group_b, router_expert_w, router_expert_b,
                         expert_w_gate, expert_w_up, expert_w_down, ple_norm, ple_w_gate, ple_w_proj)
    return (_trunk(x_prompt, p_prompt, w, layers), _trunk(x_sample, p_sample, w, layers))
```

```python
import functools

import jax
import jax.numpy as jnp
from jax import lax
from jax.experimental import pallas as pl
from jax.experimental.pallas import tpu as pltpu

F32, BF16, I32 = jnp.float32, jnp.bfloat16, jnp.int32

D_MODEL = 1024
N_HEADS = 16
QK_NOPE, QK_ROPE, V_DIM = 64, 32, 64
QK_DIM = QK_NOPE + QK_ROPE
Q_LORA, KV_LORA = 384, 256
N_GROUPS, EXPERTS_PER_GROUP = 4, 8
N_EXPERTS = N_GROUPS * EXPERTS_PER_GROUP
D_EXPERT = 512
PLE_DIM = 256
ROPE_THETA = 10000.0
RMS_EPS = 1e-6
EXPERT_BLOCK = 128

LANES = 128
SUBLANES = 8
VMEM_LIMIT = 56 * 1024 * 1024
ROW_TILE = D_MODEL // LANES

T_ATTN_IN = 256
T_Q = 512
T_TOK = 512
KV_MERGE = 2
KV_UNROLL = 8
BLOCKS_PER_STEP = 4
LOG2_E = 1.4426950408889634
BOUND_MARGIN = 1.02
MIN_DENOMINATOR = 2.0 ** -60

BF16_ROWS = 16
V_EXT = V_DIM + BF16_ROWS
W_IN_EXT = Q_LORA + KV_LORA + 2 * LANES
HEAD_PAD = N_HEADS * LANES
GROUP_COL0, EXPERT_COL0 = 0, SUBLANES


def _cparams(sem):
    return pltpu.CompilerParams(dimension_semantics=sem, vmem_limit_bytes=VMEM_LIMIT)


def _rms(x, g):
    return x * lax.rsqrt(jnp.mean(x * x, axis=-1, keepdims=True) + RMS_EPS) * g


def _dot(a, b):
    return jnp.dot(a, b, preferred_element_type=F32)


def _dot_nt(a, b):
    return lax.dot_general(a, b, (((1,), (1,)), ((), ())), preferred_element_type=F32)


def _dot_tn(a, b):
    return lax.dot_general(a, b, (((0,), (0,)), ((), ())), preferred_element_type=F32)


def _attn_in_kernel(x_ref, gn_ref, win_ref, gqa_ref, gkva_ref, wq_ref, wqs_ref, wk_ref, wvt_ref,
                    gq_ref, gk_ref, cos_ref, sin_ref, q_ref, k_ref, vt_ref):
    x = x_ref[0]
    xn = _rms(x, gn_ref[...]).astype(BF16)
    a = _dot(xn, win_ref[...])
    cq = _rms(a[:, :Q_LORA], gqa_ref[...]).astype(BF16)
    ckv = _rms(a[:, Q_LORA:Q_LORA + KV_LORA], gkva_ref[...]).astype(BF16)
    kr = a[:, Q_LORA + KV_LORA:Q_LORA + KV_LORA + LANES]
    kr_sw = a[:, Q_LORA + KV_LORA + LANES:]
    cosv, sinv = cos_ref[...], sin_ref[...]
    gq, gk = gq_ref[...], gk_ref[...]

    q = _dot(cq, wq_ref[...])
    q_sw = _dot(cq, wqs_ref[...])
    kn = _dot(ckv, wk_ref[...])
    vt = _dot_nt(wvt_ref[...], ckv)
    vrow = lax.broadcasted_iota(I32, vt.shape, 0) % V_EXT
    vt_ref[0, 0] = jnp.where(vrow == V_DIM, 1.0, vt).astype(BF16)

    ssq_rope = jnp.sum(kr * kr, axis=-1, keepdims=True)
    k_rope = kr * (gk[1:2] * cosv) + kr_sw * (gk[2:3] * sinv)
    q_cos, q_sin = gq[0:1] * cosv, gq[1:2] * sinv
    inv_d = 1.0 / QK_DIM
    bound_lane = lax.broadcasted_iota(I32, (x.shape[0], LANES), 1) == QK_DIM
    for h in range(N_HEADS):
        sl = slice(h * LANES, (h + 1) * LANES)
        qh = q[:, sl]
        rq = lax.rsqrt(jnp.sum(qh * qh, axis=-1, keepdims=True) * inv_d + RMS_EPS)
        qt = (qh * q_cos + q_sw[:, sl] * q_sin) * rq
        bound = jnp.sqrt(jnp.sum(qt * qt, axis=-1, keepdims=True)) * gq[2:3]
        q_ref[0, h] = jnp.where(bound_lane, -bound, qt).astype(BF16)
        kh = kn[:, sl]
        rk = lax.rsqrt((jnp.sum(kh * kh, axis=-1, keepdims=True) + ssq_rope) * inv_d + RMS_EPS)
        k_ref[0, h] = jnp.where(bound_lane, 1.0, (kh * gk[0:1] + k_rope) * rk).astype(BF16)


def _attn_in(x, w, cos_t, sin_t):
    B, S, _ = x.shape
    T = T_ATTN_IN
    nS = S // T
    const = lambda shape: pl.BlockSpec(shape, lambda b, j: (0,) * len(shape))
    return pl.pallas_call(
        _attn_in_kernel,
        grid=(B, nS),
        in_specs=[
            pl.BlockSpec((1, T, D_MODEL), lambda b, j: (b, j, 0)),
            const((1, D_MODEL)), const((D_MODEL, W_IN_EXT)), const((1, Q_LORA)), const((1, KV_LORA)),
            const((Q_LORA, HEAD_PAD)), const((Q_LORA, HEAD_PAD)), const((KV_LORA, HEAD_PAD)),
            const((N_HEADS * V_EXT, KV_LORA)), const((SUBLANES, LANES)), const((SUBLANES, LANES)),
            pl.BlockSpec((T, LANES), lambda b, j: (j, 0)),
            pl.BlockSpec((T, LANES), lambda b, j: (j, 0)),
        ],
        out_specs=[
            pl.BlockSpec((1, N_HEADS, T, LANES), lambda b, j: (b, 0, j, 0)),
            pl.BlockSpec((1, N_HEADS, T, LANES), lambda b, j: (b, 0, j, 0)),
            pl.BlockSpec((1, 1, N_HEADS * V_EXT, T), lambda b, j: (b, j, 0, 0)),
        ],
        out_shape=[
            jax.ShapeDtypeStruct((B, N_HEADS, S, LANES), BF16),
            jax.ShapeDtypeStruct((B, N_HEADS, S, LANES), BF16),
            jax.ShapeDtypeStruct((B, nS, N_HEADS * V_EXT, T), BF16),
        ],
        compiler_params=_cparams(("parallel", "parallel")),
        name="attn_in",
    )(x, w["attn_norm"], w["w_in_ext"], w["g_qa"], w["g_kva"], w["wq"], w["wq_sw"], w["wk"], w["wvt"],
      w["gq"], w["gk"], cos_t, sin_t)


def _flash_kernel(q_ref, k_ref, vt_ref, o_ref, m_ref, acc_ref, s_ref, sr_ref, cm_ref, *, n_kv, t_kv, merge, unroll, tq):
    t_blk = merge * t_kv
    n_blk = n_kv // merge
    n_q = q_ref.shape[2] // tq

    def scores(qi, j):
        kb = k_ref[0, 0, pl.ds(pl.multiple_of(j * t_blk, t_blk), t_blk), :]
        q = q_ref[0, 0, pl.ds(pl.multiple_of(qi * tq, tq), tq), :]
        return _dot_nt(kb, q)

    def pv(j, p):
        out = _dot(vt_ref[0, j * merge], p[0:t_kv])
        for i in range(1, merge):
            out = out + _dot(vt_ref[0, j * merge + i], p[i * t_kv:(i + 1) * t_kv])
        return out

    def finish(qi, acc):
        o_ref[0, qi] = (acc[0:V_DIM] * (1.0 / acc[V_DIM:V_DIM + 1])).astype(BF16)

    def running_max_tile(qi):
        m_ref[...] = jnp.full(m_ref.shape, -jnp.inf, F32)
        acc_ref[...] = jnp.zeros(acc_ref.shape, F32)
        s0 = scores(qi, 0)
        sr_ref[0] = s0
        cm_ref[...] = jnp.max(s0, axis=0, keepdims=True)
        sr_ref[1] = scores(qi, min(1, n_blk - 1))

        def body(it, carry):
            s, s_next, cm = sr_ref[0], sr_ref[1], cm_ref[...]
            m, acc = m_ref[...], acc_ref[...]
            for u in range(unroll):
                j = it * unroll + u
                s_next2 = scores(qi, jnp.minimum(j + 2, n_blk - 1))
                cm_next = jnp.max(s_next, axis=0, keepdims=True)
                m_new = jnp.maximum(m, cm)
                p = jnp.exp2(s - m_new)
                alpha = jnp.exp2(m - m_new)
                acc = alpha * acc + pv(j, p.astype(BF16))
                m, s, cm, s_next = m_new, s_next, cm_next, s_next2
            sr_ref[0], sr_ref[1], cm_ref[...] = s, s_next, cm
            m_ref[...], acc_ref[...] = m, acc
            return carry

        lax.fori_loop(0, n_blk // unroll, body, 0)
        finish(qi, acc_ref[...])

    assert n_blk >= 2
    s_ref[0] = scores(0, 0)
    s_ref[1] = scores(0, 1)

    def q_tile(qi, carry):
        s, s_next = s_ref[0], s_ref[1]
        qn = jnp.minimum(qi + 1, n_q - 1)
        acc = jnp.zeros((V_EXT, tq), F32)
        for j in range(n_blk):
            s_next2 = scores(qi, j + 2) if j + 2 < n_blk else scores(qn, j + 2 - n_blk)
            acc = acc + pv(j, jnp.exp2(s).astype(BF16))
            s, s_next = s_next, s_next2
        s_ref[0], s_ref[1] = s, s_next
        acc_ref[...] = acc
        safe = jnp.min(acc_ref[V_DIM:V_DIM + 1]) >= MIN_DENOMINATOR

        @pl.when(safe)
        def _():
            finish(qi, acc_ref[...])

        @pl.when(jnp.logical_not(safe))
        def _():
            running_max_tile(qi)

        return carry

    lax.fori_loop(0, n_q, q_tile, 0)


def _flash(q, k, vt):
    B, H, S, _ = q.shape
    t_kv = vt.shape[-1]
    n_kv = S // t_kv
    tq = min(T_Q, S)
    n_q = S // tq
    merge = KV_MERGE if n_kv % KV_MERGE == 0 else 1
    n_blk = n_kv // merge
    unroll = max(u for u in (KV_UNROLL, 4, 2, 1) if n_blk % u == 0)
    return pl.pallas_call(
        functools.partial(_flash_kernel, n_kv=n_kv, t_kv=t_kv, merge=merge, unroll=unroll, tq=tq),
        grid=(B, H),
        in_specs=[
            pl.BlockSpec((1, 1, S, LANES), lambda b, h: (b, h, 0, 0)),
            pl.BlockSpec((1, 1, S, LANES), lambda b, h: (b, h, 0, 0)),
            pl.BlockSpec((1, n_kv, V_EXT, t_kv), lambda b, h: (b, 0, h, 0)),
        ],
        out_specs=pl.BlockSpec((1, n_q, V_DIM, tq), lambda b, h: (b, 0, h, 0)),
        out_shape=jax.ShapeDtypeStruct((B, n_q, H * V_DIM, tq), BF16),
        scratch_shapes=[pltpu.VMEM((1, tq), F32), pltpu.VMEM((V_EXT, tq), F32),
                        pltpu.VMEM((2, merge * t_kv, tq), F32), pltpu.VMEM((2, merge * t_kv, tq), F32),
                        pltpu.VMEM((1, tq), F32)],
        compiler_params=_cparams(("parallel", "parallel")),
        name="flash",
    )(q, k, vt)


def _router(x1, gm_ref, rw_ref, rb_ref, tri_ref, carry_ref, xn_ref, ri_ref, rf_ref, cnt_ref):
    T = x1.shape[0]

    @pl.when(pl.program_id(0) == 0)
    def _():
        carry_ref[...] = jnp.zeros(carry_ref.shape, F32)

    xn = _rms(x1, gm_ref[...])
    _store_row_tiles(xn_ref, 0, xn)
    x_hi = xn.astype(BF16)
    x_lo = (xn - x_hi.astype(F32)).astype(BF16)
    rw = rw_ref[...]
    l_hi = _dot(x_hi, rw)
    l_lo = _dot(x_lo, rw[:, :LANES])
    logits = l_hi[:, :LANES] + l_hi[:, LANES:] + l_lo + rb_ref[...]
    lt = logits.T

    row = lax.broadcasted_iota(I32, (SUBLANES, T), 0).astype(F32)
    neg = -jnp.inf
    lg = jnp.where(row < N_GROUPS, lt[GROUP_COL0:GROUP_COL0 + SUBLANES], neg)
    mg = jnp.max(lg, axis=0, keepdims=True)
    gidx = jnp.min(jnp.where(lg == mg, row, float(SUBLANES)), axis=0, keepdims=True)
    pg_top = 1.0 / jnp.sum(jnp.exp(lg - mg), axis=0, keepdims=True)

    def grp(g):
        return lt[EXPERT_COL0 + g * EXPERTS_PER_GROUP:EXPERT_COL0 + (g + 1) * EXPERTS_PER_GROUP]

    le = jnp.where(gidx == 0.0, grp(0), jnp.where(gidx == 1.0, grp(1), jnp.where(gidx == 2.0, grp(2), grp(3))))
    m1 = jnp.max(le, axis=0, keepdims=True)
    i1 = jnp.min(jnp.where(le == m1, row, float(SUBLANES)), axis=0, keepdims=True)
    le2 = jnp.where(row == i1, neg, le)
    m2 = jnp.max(le2, axis=0, keepdims=True)
    i2 = jnp.min(jnp.where(le2 == m2, row, float(SUBLANES)), axis=0, keepdims=True)
    e21 = jnp.exp(m2 - m1)
    inv = 1.0 / (1.0 + e21)
    g1 = pg_top * inv
    g2 = pg_top * e21 * inv
    e1 = gidx * EXPERTS_PER_GROUP + i1
    e2 = gidx * EXPERTS_PER_GROUP + i2

    erow = lax.broadcasted_iota(I32, (N_EXPERTS, T), 0).astype(F32)
    oh1 = erow == e1
    oh2 = erow == e2
    oh = jnp.where(oh1, 1.0, jnp.where(oh2, 1.0, 0.0))
    before = _dot(oh.astype(BF16), tri_ref[...]) + carry_ref[...]
    r1 = jnp.sum(jnp.where(oh1, before, 0.0), axis=0, keepdims=True)
    r2 = jnp.sum(jnp.where(oh2, before, 0.0), axis=0, keepdims=True)
    carry = carry_ref[...] + jnp.sum(oh, axis=1, keepdims=True)
    carry_ref[...] = carry
    cnt_ref[...] = carry[:, :LANES].astype(I32)

    ri = jnp.where(row == 0.0, e1, jnp.where(row == 1.0, e2, jnp.where(row == 2.0, r1, jnp.where(row == 3.0, r2, 0.0))))
    ri_ref[...] = ri.astype(I32)
    rf_ref[...] = jnp.where(row == 0.0, g1, jnp.where(row == 1.0, g2, 0.0))


def _router_specs(T, n_tok):
    const = lambda shape: pl.BlockSpec(shape, lambda i: (0,) * len(shape))
    in_specs = [const((1, D_MODEL)), const((D_MODEL, 2 * LANES)), const((1, LANES)), const((T, T))]
    out_specs = [
        pl.BlockSpec((T * ROW_TILE, LANES), lambda i: (i, 0)),
        pl.BlockSpec((SUBLANES, T), lambda i: (0, i)),
        pl.BlockSpec((SUBLANES, T), lambda i: (0, i)),
        const((N_EXPERTS, LANES)),
    ]
    out_shape = [
        jax.ShapeDtypeStruct((n_tok * ROW_TILE, LANES), F32),
        jax.ShapeDtypeStruct((SUBLANES, n_tok), I32),
        jax.ShapeDtypeStruct((SUBLANES, n_tok), F32),
        jax.ShapeDtypeStruct((N_EXPERTS, LANES), I32),
    ]
    return in_specs, out_specs, out_shape


def _attn_out_kernel(h_ref, ot_ref, wo_ref, gm_ref, rw_ref, rb_ref, tri_ref,
                     h1_ref, xn_ref, ri_ref, rf_ref, cnt_ref, carry_ref):
    x1 = h_ref[...] + _dot_tn(ot_ref[0, 0], wo_ref[...])
    h1_ref[...] = x1
    _router(x1, gm_ref, rw_ref, rb_ref, tri_ref, carry_ref, xn_ref, ri_ref, rf_ref, cnt_ref)


def _attn_out(h, ot, w, lw):
    n_tok = h.shape[0]
    _, n_q, _, tq = ot.shape
    T = T_TOK
    assert tq % T == 0
    per_q = tq // T
    per_b = n_q * per_q
    const = lambda shape: pl.BlockSpec(shape, lambda i: (0,) * len(shape))
    r_in, r_out, r_shape = _router_specs(T, n_tok)
    return pl.pallas_call(
        _attn_out_kernel,
        grid=(n_tok // T,),
        in_specs=[
            pl.BlockSpec((T, D_MODEL), lambda i: (i, 0)),
            pl.BlockSpec((1, 1, N_HEADS * V_DIM, T),
                         lambda i: (i // per_b, (i % per_b) // per_q, 0, i % per_q)),
            const((N_HEADS * V_DIM, D_MODEL)),
        ] + r_in,
        out_specs=[pl.BlockSpec((T, D_MODEL), lambda i: (i, 0))] + r_out,
        out_shape=[jax.ShapeDtypeStruct((n_tok, D_MODEL), F32)] + r_shape,
        scratch_shapes=[pltpu.VMEM((N_EXPERTS, T), F32)],
        compiler_params=_cparams(("arbitrary",)),
        name="attn_out_router",
    )(h, ot, w["w_o"], lw["moe_norm"], lw["rw"], lw["rb"], w["tri"])


def _conv_kernel(h_ref, hp_ref, hn_ref, gn_ref, win_ref, cw_ref, wout_ref, gm_ref, rw_ref, rb_ref, tri_ref,
                 h1_ref, xn_ref, ri_ref, rf_ref, cnt_ref, carry_ref, *, tiles_per_seq):
    T = h_ref.shape[0]
    i = pl.program_id(0)
    x = h_ref[...]
    gn = gn_ref[...]
    bcu = _dot(_rms(x, gn).astype(BF16), win_ref[...])
    b = bcu[:, :D_MODEL]
    cu = bcu[:, D_MODEL:2 * D_MODEL] * bcu[:, 2 * D_MODEL:]
    halo = jnp.concatenate([hp_ref[...], hn_ref[...]], axis=0)
    hcu = _dot(_rms(halo, gn).astype(BF16), win_ref[:, D_MODEL:])
    hcu = hcu[:, :D_MODEL] * hcu[:, D_MODEL:]
    first = (i % tiles_per_seq) == 0
    last = (i % tiles_per_seq) == tiles_per_seq - 1
    cu_before = jnp.where(first, 0.0, hcu[SUBLANES - 1:SUBLANES])
    cu_after = jnp.where(last, 0.0, hcu[SUBLANES:SUBLANES + 1])
    row = lax.broadcasted_iota(I32, (T, 1), 0)
    prev = jnp.where(row == 0, cu_before, pltpu.roll(cu, 1, 0))
    nxt = jnp.where(row == T - 1, cu_after, pltpu.roll(cu, T - 1, 0))
    cw = cw_ref[...]
    y = cw[0:1] * prev + cw[1:2] * cu + cw[2:3] * nxt
    x1 = x + _dot((b * y).astype(BF16), wout_ref[...])
    h1_ref[...] = x1
    _router(x1, gm_ref, rw_ref, rb_ref, tri_ref, carry_ref, xn_ref, ri_ref, rf_ref, cnt_ref)


def _conv(h, S, w, lw):
    n_tok = h.shape[0]
    T = T_TOK
    nS = S // T
    rows8 = T // SUBLANES
    n8 = n_tok // SUBLANES
    const = lambda shape: pl.BlockSpec(shape, lambda i: (0,) * len(shape))
    r_in, r_out, r_shape = _router_specs(T, n_tok)
    return pl.pallas_call(
        functools.partial(_conv_kernel, tiles_per_seq=nS),
        grid=(n_tok // T,),
        in_specs=[
            pl.BlockSpec((T, D_MODEL), lambda i: (i, 0)),
            pl.BlockSpec((SUBLANES, D_MODEL), lambda i: (jnp.maximum(i * rows8 - 1, 0), 0)),
            pl.BlockSpec((SUBLANES, D_MODEL), lambda i: (jnp.minimum((i + 1) * rows8, n8 - 1), 0)),
            const((1, D_MODEL)), const((D_MODEL, 3 * D_MODEL)), const((SUBLANES, D_MODEL)),
            const((D_MODEL, D_MODEL)),
        ] + r_in,
        out_specs=[pl.BlockSpec((T, D_MODEL), lambda i: (i, 0))] + r_out,
        out_shape=[jax.ShapeDtypeStruct((n_tok, D_MODEL), F32)] + r_shape,
        scratch_shapes=[pltpu.VMEM((N_EXPERTS, T), F32)],
        compiler_params=_cparams(("arbitrary",)),
        name="conv_router",
    )(h, h, h, w["conv_norm"], w["conv_w_in"], w["conv_w"], w["conv_w_out"],
      lw["moe_norm"], lw["rw"], lw["rb"], w["tri"])


def _plan_kernel(cnt_ref, ri_ref, dest_ref, info_ref, *, n_blocks):
    ri = ri_ref[...]
    eid = ri[0:2]
    offset = jnp.zeros(eid.shape, I32)
    blocks_before = jnp.int32(0)
    for e in range(N_EXPERTS):
        nb = (cnt_ref[e, 0] + (EXPERT_BLOCK - 1)) // EXPERT_BLOCK
        offset = jnp.where(eid == e, blocks_before * EXPERT_BLOCK, offset)

        def fill(j, c, e=e, base=blocks_before):
            info_ref[base + j] = e
            return c

        lax.fori_loop(0, nb, fill, 0)
        blocks_before = blocks_before + nb

    def fill_tail(j, c):
        info_ref[j] = N_EXPERTS - 1
        return c

    lax.fori_loop(blocks_before, n_blocks, fill_tail, 0)
    info_ref[n_blocks] = blocks_before
    dest_ref[...] = jnp.zeros(dest_ref.shape, I32)
    dest_ref[0:2, :] = ri[2:4] + offset


def _plan(cnt, ri, n_blocks):
    n_tok = ri.shape[1]
    return pl.pallas_call(
        functools.partial(_plan_kernel, n_blocks=n_blocks),
        in_specs=[pl.BlockSpec(memory_space=pltpu.SMEM), pl.BlockSpec(memory_space=pltpu.VMEM)],
        out_specs=[pl.BlockSpec(memory_space=pltpu.VMEM), pl.BlockSpec(memory_space=pltpu.SMEM)],
        out_shape=[jax.ShapeDtypeStruct((SUBLANES, n_tok), I32),
                   jax.ShapeDtypeStruct((n_blocks + 1,), I32)],
        compiler_params=pltpu.CompilerParams(vmem_limit_bytes=VMEM_LIMIT),
        name="plan",
    )(cnt, ri)


def _store_row_tiles(ref, r0, x):
    for s in range(ROW_TILE):
        ref[pl.ds(r0 * ROW_TILE + s, x.shape[0], stride=ROW_TILE), :] = x[:, s * LANES:(s + 1) * LANES]


def _load_row_tiles(ref, r0, n):
    cols = [ref[pl.ds(r0 * ROW_TILE + s, n, stride=ROW_TILE), :] for s in range(ROW_TILE)]
    return jnp.concatenate(cols, axis=1)


def _row_copy(src_ref, s, dst_ref, d, sem):
    src = src_ref.at[pl.ds(pl.multiple_of(s * ROW_TILE, ROW_TILE), ROW_TILE)]
    dst = dst_ref.at[pl.ds(pl.multiple_of(d * ROW_TILE, ROW_TILE), ROW_TILE)]
    return pltpu.make_async_copy(src, dst, sem)


def _dispatch_kernel(d0_ref, d1_ref, x_ref, xb_in_ref, xb_ref, sem):
    del xb_in_ref
    T = x_ref.shape[0] // ROW_TILE
    base = pl.program_id(0) * T

    def start(t, c):
        _row_copy(x_ref, t, xb_ref, d0_ref[base + t], sem.at[0]).start(priority=0)
        _row_copy(x_ref, t, xb_ref, d1_ref[base + t], sem.at[1]).start(priority=1)
        return c

    lax.fori_loop(0, T, start, 0, unroll=8)

    def wait(t, c):
        _row_copy(x_ref, t, xb_ref, d0_ref[base + t], sem.at[0]).wait()
        _row_copy(x_ref, t, xb_ref, d1_ref[base + t], sem.at[1]).wait()
        return c

    lax.fori_loop(0, T, wait, 0, unroll=8)


def _dispatch(xn, d0, d1, n_rows):
    n_tok = xn.shape[0] // ROW_TILE
    T = T_TOK
    xb0 = jnp.zeros((n_rows * ROW_TILE, LANES), F32)
    return pl.pallas_call(
        _dispatch_kernel,
        grid_spec=pltpu.PrefetchScalarGridSpec(
            num_scalar_prefetch=2,
            grid=(n_tok // T,),
            in_specs=[pl.BlockSpec((T * ROW_TILE, LANES), lambda i, d0, d1: (i, 0)),
                      pl.BlockSpec(memory_space=pl.ANY)],
            out_specs=pl.BlockSpec(memory_space=pl.ANY),
            scratch_shapes=[pltpu.SemaphoreType.DMA((2,))],
        ),
        out_shape=jax.ShapeDtypeStruct((n_rows * ROW_TILE, LANES), F32),
        input_output_aliases={3: 0},
        compiler_params=_cparams(("arbitrary",)),
        name="dispatch",
    )(d0, d1, xn, xb0)


def _expert_kernel(info_ref, xb_ref, *refs, n_blocks):
    yb_ref = refs[-1]
    i = pl.program_id(0)
    blk0 = i * BLOCKS_PER_STEP
    n_used = info_ref[n_blocks]

    def swiglu(r0, n, wg_ref, wu_ref, wd_ref):
        x = _load_row_tiles(xb_ref, r0, n).astype(BF16)
        g = _dot(x, wg_ref[0])
        up = _dot(x, wu_ref[0])
        hmid = (g * jax.nn.sigmoid(g) * up).astype(BF16)
        _store_row_tiles(yb_ref, r0, _dot(hmid, wd_ref[0]))

    one_expert = jnp.logical_and(info_ref[blk0] == info_ref[blk0 + BLOCKS_PER_STEP - 1],
                                 blk0 + BLOCKS_PER_STEP - 1 < n_used)

    @pl.when(one_expert)
    def _():
        swiglu(0, BLOCKS_PER_STEP * EXPERT_BLOCK, *refs[0:3])

    for u in range(BLOCKS_PER_STEP):
        blk = blk0 + u

        @pl.when(jnp.logical_and(jnp.logical_not(one_expert), blk < n_used))
        def _():
            swiglu(u * EXPERT_BLOCK, EXPERT_BLOCK, *refs[3 * u:3 * u + 3])

        @pl.when(blk >= n_used)
        def _():
            n = EXPERT_BLOCK * ROW_TILE
            yb_ref[pl.ds(u * n, n), :] = jnp.zeros((n, LANES), F32)


def _experts(info, xb, lw):
    n_rows = xb.shape[0] // ROW_TILE
    n_blocks = n_rows // EXPERT_BLOCK
    assert n_blocks % BLOCKS_PER_STEP == 0
    step_rows = BLOCKS_PER_STEP * EXPERT_BLOCK * ROW_TILE
    w_specs, w_args = [], []
    for u in range(BLOCKS_PER_STEP):
        pick = lambda i, info, u=u: (info[i * BLOCKS_PER_STEP + u], 0, 0)
        w_specs += [pl.BlockSpec((1, D_MODEL, D_EXPERT), pick), pl.BlockSpec((1, D_MODEL, D_EXPERT), pick),
                    pl.BlockSpec((1, D_EXPERT, D_MODEL), pick)]
        w_args += [lw["w_gate"], lw["w_up"], lw["w_down"]]
    return pl.pallas_call(
        functools.partial(_expert_kernel, n_blocks=n_blocks),
        grid_spec=pltpu.PrefetchScalarGridSpec(
            num_scalar_prefetch=1,
            grid=(n_blocks // BLOCKS_PER_STEP,),
            in_specs=[pl.BlockSpec((step_rows, LANES), lambda i, info: (i, 0))] + w_specs,
            out_specs=pl.BlockSpec((step_rows, LANES), lambda i, info: (i, 0)),
        ),
        out_shape=jax.ShapeDtypeStruct((n_rows * ROW_TILE, LANES), F32),
        compiler_params=_cparams(("arbitrary",)),
        name="experts",
    )(info, xb, *w_args)


def _combine_kernel(d0_ref, d1_ref, h_ref, rf_ref, p_ref, yb_ref, gn_ref, wg_ref, wp_ref, o_ref,
                    y0_ref, y1_ref, sem):
    T = h_ref.shape[0]
    base = pl.program_id(0) * T

    def start(t, c):
        _row_copy(yb_ref, d0_ref[base + t], y0_ref, t, sem.at[0]).start(priority=0)
        _row_copy(yb_ref, d1_ref[base + t], y1_ref, t, sem.at[1]).start(priority=1)
        return c

    lax.fori_loop(0, T, start, 0, unroll=8)

    proj = _dot(p_ref[...].astype(BF16), wp_ref[...])
    gates = jnp.concatenate([rf_ref[...], jnp.zeros((LANES - SUBLANES, T), F32)], axis=0).T

    def wait(t, c):
        _row_copy(yb_ref, d0_ref[base + t], y0_ref, t, sem.at[0]).wait()
        _row_copy(yb_ref, d1_ref[base + t], y1_ref, t, sem.at[1]).wait()
        return c

    lax.fori_loop(0, T, wait, 0, unroll=8)

    y0 = _load_row_tiles(y0_ref, 0, T)
    y1 = _load_row_tiles(y1_ref, 0, T)
    h2 = h_ref[...] + gates[:, 0:1] * y0 + gates[:, 1:2] * y1
    gate = jax.nn.sigmoid(_dot(_rms(h2, gn_ref[...]).astype(BF16), wg_ref[...]))
    o_ref[...] = h2 + gate * proj


def _combine(h, rf, p, yb, d0, d1, lw):
    n_tok = h.shape[0]
    T = T_TOK
    const = lambda shape: pl.BlockSpec(shape, lambda i, d0, d1: (0,) * len(shape))
    return pl.pallas_call(
        _combine_kernel,
        grid_spec=pltpu.PrefetchScalarGridSpec(
            num_scalar_prefetch=2,
            grid=(n_tok // T,),
            in_specs=[
                pl.BlockSpec((T, D_MODEL), lambda i, d0, d1: (i, 0)),
                pl.BlockSpec((SUBLANES, T), lambda i, d0, d1: (0, i)),
                pl.BlockSpec((T, PLE_DIM), lambda i, d0, d1: (i, 0)),
                pl.BlockSpec(memory_space=pl.ANY),
                const((1, D_MODEL)), const((D_MODEL, D_MODEL)), const((PLE_DIM, D_MODEL)),
            ],
            out_specs=pl.BlockSpec((T, D_MODEL), lambda i, d0, d1: (i, 0)),
            scratch_shapes=[pltpu.VMEM((T * ROW_TILE, LANES), F32), pltpu.VMEM((T * ROW_TILE, LANES), F32),
                            pltpu.SemaphoreType.DMA((2,))],
        ),
        out_shape=jax.ShapeDtypeStruct((n_tok, D_MODEL), F32),
        compiler_params=_cparams(("arbitrary",)),
        name="combine_ple",
    )(d0, d1, h, rf, p, yb, lw["ple_norm"], lw["ple_w_gate"], lw["ple_w_proj"])


def _pad_lanes(x, lo, width=LANES):
    n = x.shape[-1]
    pad = [(0, 0)] * (x.ndim - 1) + [(lo, width - lo - n)]
    return jnp.pad(x, pad)


def _rotate_half_cols(x):
    half = QK_ROPE // 2
    return jnp.concatenate([-x[..., half:], x[..., :half]], axis=-1)


def _swap_halves(g):
    half = QK_ROPE // 2
    return jnp.concatenate([g[..., half:], g[..., :half]], axis=-1)


def _prepare(attn_norm, attn_w_in, attn_q_a_norm, attn_kv_a_norm, attn_w_qb, attn_w_kvb, attn_q_gain,
             attn_k_gain, attn_w_o, conv_norm, conv_w_in, conv_w, conv_w_out, moe_norm, router_group_w,
             router_group_b, router_expert_w, router_expert_b, expert_w_gate, expert_w_up, expert_w_down,
             ple_norm, ple_w_gate, ple_w_proj):
    w = {}
    w_in = attn_w_in[0]
    rope_cols = w_in[:, Q_LORA + KV_LORA:]
    w["w_in_ext"] = jnp.concatenate(
        [w_in[:, :Q_LORA + KV_LORA], _pad_lanes(rope_cols, QK_NOPE), _pad_lanes(_rotate_half_cols(rope_cols), QK_NOPE)],
        axis=1).astype(BF16)
    w["attn_norm"] = attn_norm[0][None]
    w["g_qa"] = attn_q_a_norm[0][None]
    w["g_kva"] = attn_kv_a_norm[0][None]
    wqb = attn_w_qb[0].reshape(Q_LORA, N_HEADS, QK_DIM)
    w["wq"] = _pad_lanes(wqb, 0).reshape(Q_LORA, HEAD_PAD).astype(BF16)
    w["wq_sw"] = _pad_lanes(_rotate_half_cols(wqb[..., QK_NOPE:]), QK_NOPE).reshape(Q_LORA, HEAD_PAD).astype(BF16)
    wkvb = attn_w_kvb[0].reshape(KV_LORA, N_HEADS, QK_NOPE + V_DIM)
    w["wk"] = _pad_lanes(wkvb[..., :QK_NOPE], 0).reshape(KV_LORA, HEAD_PAD).astype(BF16)
    w["wvt"] = _pad_lanes(wkvb[..., QK_NOPE:], 0, V_EXT).reshape(KV_LORA, N_HEADS * V_EXT).T.astype(BF16)
    scale = LOG2_E / float(QK_DIM) ** 0.5
    gqs = attn_q_gain[0] * scale
    gk = attn_k_gain[0]
    zero = jnp.zeros((SUBLANES - 2, LANES), F32)
    key_norm_bound = jnp.full((1, LANES), BOUND_MARGIN * float(QK_DIM) ** 0.5, F32) * jnp.max(jnp.abs(gk))
    w["gq"] = jnp.concatenate([_pad_lanes(gqs, 0)[None], _pad_lanes(_swap_halves(gqs[QK_NOPE:]), QK_NOPE)[None],
                               key_norm_bound, zero[1:]])
    w["gk"] = jnp.concatenate([_pad_lanes(gk[:QK_NOPE], 0)[None], _pad_lanes(gk[QK_NOPE:], QK_NOPE)[None],
                               _pad_lanes(_swap_halves(gk[QK_NOPE:]), QK_NOPE)[None], zero[1:]])
    w["w_o"] = attn_w_o[0].astype(BF16)
    w["conv_norm"] = conv_norm[0][None]
    w["conv_w_in"] = conv_w_in[0].astype(BF16)
    w["conv_w"] = jnp.pad(conv_w[0], ((0, SUBLANES - conv_w.shape[1]), (0, 0)))
    w["conv_w_out"] = conv_w_out[0].astype(BF16)
    tri = lax.broadcasted_iota(I32, (T_TOK, T_TOK), 0) < lax.broadcasted_iota(I32, (T_TOK, T_TOK), 1)
    w["tri"] = tri.astype(BF16)
    layers = []
    for i in range(moe_norm.shape[0]):
        rw = jnp.zeros((D_MODEL, LANES), F32)
        rw = rw.at[:, GROUP_COL0:GROUP_COL0 + N_GROUPS].set(router_group_w[i])
        rw = rw.at[:, EXPERT_COL0:EXPERT_COL0 + N_EXPERTS].set(router_expert_w[i])
        rw_hi = rw.astype(BF16)
        rw_lo = (rw - rw_hi.astype(F32)).astype(BF16)
        rb = jnp.zeros((1, LANES), F32)
        rb = rb.at[0, GROUP_COL0:GROUP_COL0 + N_GROUPS].set(router_group_b[i])
        rb = rb.at[0, EXPERT_COL0:EXPERT_COL0 + N_EXPERTS].set(router_expert_b[i])
        layers.append(dict(
            moe_norm=moe_norm[i][None], rw=jnp.concatenate([rw_hi, rw_lo], axis=1), rb=rb,
            w_gate=expert_w_gate[i].astype(BF16), w_up=expert_w_up[i].astype(BF16),
            w_down=expert_w_down[i].astype(BF16),
            ple_norm=ple_norm[i][None], ple_w_gate=ple_w_gate[i].astype(BF16),
            ple_w_proj=ple_w_proj[i].astype(BF16)))
    return w, layers


def _rope_tiles(S):
    pos = jnp.arange(S, dtype=F32)
    inv = ROPE_THETA ** (-jnp.arange(0, QK_ROPE, 2, dtype=F32) / QK_ROPE)
    ang = pos[:, None] * inv[None, :]
    cos, sin = jnp.cos(ang), jnp.sin(ang)
    cos_t = jnp.concatenate([jnp.ones((S, QK_NOPE), F32), cos, cos, jnp.zeros((S, LANES - QK_DIM), F32)], axis=1)
    sin_t = jnp.concatenate([jnp.zeros((S, QK_NOPE), F32), sin, sin, jnp.zeros((S, LANES - QK_DIM), F32)], axis=1)
    return cos_t, sin_t


def _moe_ple(h1, xn, ri, rf, cnt, p, lw):
    n_tok = h1.shape[0]
    n_assign = 2 * n_tok
    n_rows = -(-n_assign // EXPERT_BLOCK) * EXPERT_BLOCK + N_EXPERTS * EXPERT_BLOCK
    dest, info = _plan(cnt, ri, n_rows // EXPERT_BLOCK)
    d0, d1 = dest[0], dest[1]
    xb = _dispatch(xn, d0, d1, n_rows)
    yb = _experts(info, xb, lw)
    return _combine(h1, rf, p, yb, d0, d1, lw)


def _trunk(x, p, w, layers):
    B, S, _ = x.shape
    n_tok = B * S
    cos_t, sin_t = _rope_tiles(S)
    h = x.reshape(n_tok, D_MODEL)
    p = p.reshape(p.shape[0], n_tok, PLE_DIM)
    q, k, vt = _attn_in(x, w, cos_t, sin_t)
    ot = _flash(q, k, vt)
    h1, xn, ri, rf, cnt = _attn_out(h, ot, w, layers[0])
    h = _moe_ple(h1, xn, ri, rf, cnt, p[0], layers[0])
    h1, xn, ri, rf, cnt = _conv(h, S, w, layers[1])
    h = _moe_ple(h1, xn, ri, rf, cnt, p[1], layers[1])
    return h.reshape(B, S, D_MODEL)


def kernel(x_prompt, x_sample, p_prompt, p_sample, attn_norm, attn_w_in, attn_q_a_norm, attn_kv_a_norm, attn_w_qb, attn_w_kvb, attn_q_gain, attn_k_gain, attn_w_o, conv_norm, conv_w_in, conv_w, conv_w_out, moe_norm, router_group_w, router_group_b, router_expert_w, router_expert_b, expert_w_gate, expert_w_up, expert_w_down, ple_norm, ple_w_gate, ple_w_proj):
    assert x_prompt.shape[-1] == D_MODEL and moe_norm.shape[0] == 2
    assert attn_w_in.shape == (1, D_MODEL, Q_LORA + KV_LORA + QK_ROPE)
    assert attn_w_qb.shape == (1, Q_LORA, N_HEADS * QK_DIM)
    assert attn_w_kvb.shape == (1, KV_LORA, N_HEADS * (QK_NOPE + V_DIM))
    assert expert_w_gate.shape[1:] == (N_EXPERTS, D_MODEL, D_EXPERT)
    w, layers = _prepare(attn_norm, attn_w_in, attn_q_a_norm, attn_kv_a_norm, attn_w_qb, attn_w_kvb,
                         attn_q_gain, attn_k_gain, attn_w_o, conv_norm, conv_w_in, conv_w, conv_w_out,
                         moe_norm, router_group_w, router_group_b, router_expert_w, router_expert_b,
                         expert_w_gate, expert_w_up, expert_w_down, ple_norm, ple_w_gate, ple_w_proj)
    return (_trunk(x_prompt, p_prompt, w, layers), _trunk(x_sample, p_sample, w, layers))
```

```python
import functools

import jax
import jax.numpy as jnp
from jax import lax
from jax.experimental import pallas as pl
from jax.experimental.pallas import tpu as pltpu

F32, BF16, I32 = jnp.float32, jnp.bfloat16, jnp.int32

D_MODEL = 1024
N_HEADS = 16
QK_NOPE, QK_ROPE, V_DIM = 64, 32, 64
QK_DIM = QK_NOPE + QK_ROPE
Q_LORA, KV_LORA = 384, 256
N_GROUPS, EXPERTS_PER_GROUP = 4, 8
N_EXPERTS = N_GROUPS * EXPERTS_PER_GROUP
D_EXPERT = 512
PLE_DIM = 256
ROPE_THETA = 10000.0
RMS_EPS = 1e-6
EXPERT_BLOCK = 128

LANES = 128
SUBLANES = 8
VMEM_LIMIT = 56 * 1024 * 1024
ROW_TILE = D_MODEL // LANES

T_ATTN_IN = 256
T_Q = 512
T_TOK = 512
KV_MERGE = 2
KV_UNROLL = 8
BLOCKS_PER_STEP = 4
LOG2_E = 1.4426950408889634
BOUND_MARGIN = 1.02
MIN_DENOMINATOR = 2.0 ** -60

BF16_ROWS = 16
V_EXT = V_DIM + BF16_ROWS
W_IN_EXT = Q_LORA + KV_LORA + 2 * LANES
HEAD_PAD = N_HEADS * LANES
GROUP_COL0, EXPERT_COL0 = 0, SUBLANES


def _cparams(sem):
    return pltpu.CompilerParams(dimension_semantics=sem, vmem_limit_bytes=VMEM_LIMIT)


def _rms(x, g):
    return x * lax.rsqrt(jnp.mean(x * x, axis=-1, keepdims=True) + RMS_EPS) * g


def _dot(a, b):
    return jnp.dot(a, b, preferred_element_type=F32)


def _dot_nt(a, b):
    return lax.dot_general(a, b, (((1,), (1,)), ((), ())), preferred_element_type=F32)


def _dot_tn(a, b):
    return lax.dot_general(a, b, (((0,), (0,)), ((), ())), preferred_element_type=F32)


def _attn_in_kernel(x_ref, gn_ref, win_ref, gqa_ref, gkva_ref, wq_ref, wqs_ref, wk_ref, wvt_ref,
                    gq_ref, gk_ref, cos_ref, sin_ref, q_ref, k_ref, vt_ref):
    x = x_ref[0]
    xn = _rms(x, gn_ref[...]).astype(BF16)
    a = _dot(xn, win_ref[...])
    cq = _rms(a[:, :Q_LORA], gqa_ref[...]).astype(BF16)
    ckv = _rms(a[:, Q_LORA:Q_LORA + KV_LORA], gkva_ref[...]).astype(BF16)
    kr = a[:, Q_LORA + KV_LORA:Q_LORA + KV_LORA + LANES]
    kr_sw = a[:, Q_LORA + KV_LORA + LANES:]
    cosv, sinv = cos_ref[...], sin_ref[...]
    gq, gk = gq_ref[...], gk_ref[...]

    q = _dot(cq, wq_ref[...])
    q_sw = _dot(cq, wqs_ref[...])
    kn = _dot(ckv, wk_ref[...])
    vt = _dot_nt(wvt_ref[...], ckv)
    vrow = lax.broadcasted_iota(I32, vt.shape, 0) % V_EXT
    vt_ref[0, 0] = jnp.where(vrow == V_DIM, 1.0, vt).astype(BF16)

    ssq_rope = jnp.sum(kr * kr, axis=-1, keepdims=True)
    k_rope = kr * (gk[1:2] * cosv) + kr_sw * (gk[2:3] * sinv)
    q_cos, q_sin = gq[0:1] * cosv, gq[1:2] * sinv
    inv_d = 1.0 / QK_DIM
    bound_lane = lax.broadcasted_iota(I32, (x.shape[0], LANES), 1) == QK_DIM
    for h in range(N_HEADS):
        sl = slice(h * LANES, (h + 1) * LANES)
        qh = q[:, sl]
        rq = lax.rsqrt(jnp.sum(qh * qh, axis=-1, keepdims=True) * inv_d + RMS_EPS)
        qt = (qh * q_cos + q_sw[:, sl] * q_sin) * rq
        bound = jnp.sqrt(jnp.sum(qt * qt, axis=-1, keepdims=True)) * gq[2:3]
        q_ref[0, h] = jnp.where(bound_lane, -bound, qt).astype(BF16)
        kh = kn[:, sl]
        rk = lax.rsqrt((jnp.sum(kh * kh, axis=-1, keepdims=True) + ssq_rope) * inv_d + RMS_EPS)
        k_ref[0, h] = jnp.where(bound_lane, 1.0, (kh * gk[0:1] + k_rope) * rk).astype(BF16)


def _attn_in(x, w, cos_t, sin_t):
    B, S, _ = x.shape
    T = T_ATTN_IN
    nS = S // T
    const = lambda shape: pl.BlockSpec(shape, lambda b, j: (0,) * len(shape))
    return pl.pallas_call(
        _attn_in_kernel,
        grid=(B, nS),
        in_specs=[
            pl.BlockSpec((1, T, D_MODEL), lambda b, j: (b, j, 0)),
            const((1, D_MODEL)), const((D_MODEL, W_IN_EXT)), const((1, Q_LORA)), const((1, KV_LORA)),
            const((Q_LORA, HEAD_PAD)), const((Q_LORA, HEAD_PAD)), const((KV_LORA, HEAD_PAD)),
            const((N_HEADS * V_EXT, KV_LORA)), const((SUBLANES, LANES)), const((SUBLANES, LANES)),
            pl.BlockSpec((T, LANES), lambda b, j: (j, 0)),
            pl.BlockSpec((T, LANES), lambda b, j: (j, 0)),
        ],
        out_specs=[
            pl.BlockSpec((1, N_HEADS, T, LANES), lambda b, j: (b, 0, j, 0)),
            pl.BlockSpec((1, N_HEADS, T, LANES), lambda b, j: (b, 0, j, 0)),
            pl.BlockSpec((1, 1, N_HEADS * V_EXT, T), lambda b, j: (b, j, 0, 0)),
        ],
        out_shape=[
            jax.ShapeDtypeStruct((B, N_HEADS, S, LANES), BF16),
            jax.ShapeDtypeStruct((B, N_HEADS, S, LANES), BF16),
            jax.ShapeDtypeStruct((B, nS, N_HEADS * V_EXT, T), BF16),
        ],
        compiler_params=_cparams(("parallel", "parallel")),
        name="attn_in",
    )(x, w["attn_norm"], w["w_in_ext"], w["g_qa"], w["g_kva"], w["wq"], w["wq_sw"], w["wk"], w["wvt"],
      w["gq"], w["gk"], cos_t, sin_t)


def _flash_kernel(q_ref, k_ref, vt_ref, o_ref, m_ref, acc_ref, s_ref, sr_ref, cm_ref, *, n_kv, t_kv, merge, unroll, tq):
    t_blk = merge * t_kv
    n_blk = n_kv // merge
    n_q = q_ref.shape[2] // tq

    def scores(qi, j):
        kb = k_ref[0, 0, pl.ds(pl.multiple_of(j * t_blk, t_blk), t_blk), :]
        q = q_ref[0, 0, pl.ds(pl.multiple_of(qi * tq, tq), tq), :]
        return _dot_nt(kb, q)

    def pv(j, p):
        out = _dot(vt_ref[0, j * merge], p[0:t_kv])
        for i in range(1, merge):
            out = out + _dot(vt_ref[0, j * merge + i], p[i * t_kv:(i + 1) * t_kv])
        return out

    def finish(qi, acc):
        o_ref[0, qi] = (acc[0:V_DIM] * (1.0 / acc[V_DIM:V_DIM + 1])).astype(BF16)

    def running_max_tile(qi):
        m_ref[...] = jnp.full(m_ref.shape, -jnp.inf, F32)
        acc_ref[...] = jnp.zeros(acc_ref.shape, F32)
        s0 = scores(qi, 0)
        sr_ref[0] = s0
        cm_ref[...] = jnp.max(s0, axis=0, keepdims=True)
        sr_ref[1] = scores(qi, min(1, n_blk - 1))

        def body(it, carry):
            s, s_next, cm = sr_ref[0], sr_ref[1], cm_ref[...]
            m, acc = m_ref[...], acc_ref[...]
            for u in range(unroll):
                j = it * unroll + u
                s_next2 = scores(qi, jnp.minimum(j + 2, n_blk - 1))
                cm_next = jnp.max(s_next, axis=0, keepdims=True)
                m_new = jnp.maximum(m, cm)
                p = jnp.exp2(s - m_new)
                alpha = jnp.exp2(m - m_new)
                acc = alpha * acc + pv(j, p.astype(BF16))
                m, s, cm, s_next = m_new, s_next, cm_next, s_next2
            sr_ref[0], sr_ref[1], cm_ref[...] = s, s_next, cm
            m_ref[...], acc_ref[...] = m, acc
            return carry

        lax.fori_loop(0, n_blk // unroll, body, 0)
        finish(qi, acc_ref[...])

    assert n_blk >= 2
    s_ref[0] = scores(0, 0)
    s_ref[1] = scores(0, 1)

    def q_tile(qi, carry):
        s, s_next = s_ref[0], s_ref[1]
        qn = jnp.minimum(qi + 1, n_q - 1)
        acc = jnp.zeros((V_EXT, tq), F32)
        for j in range(n_blk):
            s_next2 = scores(qi, j + 2) if j + 2 < n_blk else scores(qn, j + 2 - n_blk)
            acc = acc + pv(j, jnp.exp2(s).astype(BF16))
            s, s_next = s_next, s_next2
        s_ref[0], s_ref[1] = s, s_next
        acc_ref[...] = acc
        safe = jnp.min(acc_ref[V_DIM:V_DIM + 1]) >= MIN_DENOMINATOR

        @pl.when(safe)
        def _():
            finish(qi, acc_ref[...])

        @pl.when(jnp.logical_not(safe))
        def _():
            running_max_tile(qi)

        return carry

    lax.fori_loop(0, n_q, q_tile, 0)


def _flash(q, k, vt):
    B, H, S, _ = q.shape
    t_kv = vt.shape[-1]
    n_kv = S // t_kv
    tq = min(T_Q, S)
    n_q = S // tq
    merge = KV_MERGE if n_kv % KV_MERGE == 0 else 1
    n_blk = n_kv // merge
    unroll = max(u for u in (KV_UNROLL, 4, 2, 1) if n_blk % u == 0)
    return pl.pallas_call(
        functools.partial(_flash_kernel, n_kv=n_kv, t_kv=t_kv, merge=merge, unroll=unroll, tq=tq),
        grid=(B, H),
        in_specs=[
            pl.BlockSpec((1, 1, S, LANES), lambda b, h: (b, h, 0, 0)),
            pl.BlockSpec((1, 1, S, LANES), lambda b, h: (b, h, 0, 0)),
            pl.BlockSpec((1, n_kv, V_EXT, t_kv), lambda b, h: (b, 0, h, 0)),
        ],
        out_specs=pl.BlockSpec((1, n_q, V_DIM, tq), lambda b, h: (b, 0, h, 0)),
        out_shape=jax.ShapeDtypeStruct((B, n_q, H * V_DIM, tq), BF16),
        scratch_shapes=[pltpu.VMEM((1, tq), F32), pltpu.VMEM((V_EXT, tq), F32),
                        pltpu.VMEM((2, merge * t_kv, tq), F32), pltpu.VMEM((2, merge * t_kv, tq), F32),
                        pltpu.VMEM((1, tq), F32)],
        compiler_params=_cparams(("parallel", "parallel")),
        name="flash",
    )(q, k, vt)


def _router(x1, gm_ref, rw_ref, rb_ref, tri_ref, carry_ref, xn_ref, ri_ref, rf_ref, cnt_ref):
    T = x1.shape[0]

    @pl.when(pl.program_id(0) == 0)
    def _():
        carry_ref[...] = jnp.zeros(carry_ref.shape, F32)

    xn = _rms(x1, gm_ref[...])
    _store_row_tiles(xn_ref, 0, xn)
    x_hi = xn.astype(BF16)
    x_lo = (xn - x_hi.astype(F32)).astype(BF16)
    rw = rw_ref[...]
    l_hi = _dot(x_hi, rw)
    l_lo = _dot(x_lo, rw[:, :LANES])
    logits = l_hi[:, :LANES] + l_hi[:, LANES:] + l_lo + rb_ref[...]
    lt = logits.T

    row = lax.broadcasted_iota(I32, (SUBLANES, T), 0).astype(F32)
    neg = -jnp.inf
    lg = jnp.where(row < N_GROUPS, lt[GROUP_COL0:GROUP_COL0 + SUBLANES], neg)
    mg = jnp.max(lg, axis=0, keepdims=True)
    gidx = jnp.min(jnp.where(lg == mg, row, float(SUBLANES)), axis=0, keepdims=True)
    pg_top = 1.0 / jnp.sum(jnp.exp(lg - mg), axis=0, keepdims=True)

    def grp(g):
        return lt[EXPERT_COL0 + g * EXPERTS_PER_GROUP:EXPERT_COL0 + (g + 1) * EXPERTS_PER_GROUP]

    le = jnp.where(gidx == 0.0, grp(0), jnp.where(gidx == 1.0, grp(1), jnp.where(gidx == 2.0, grp(2), grp(3))))
    m1 = jnp.max(le, axis=0, keepdims=True)
    i1 = jnp.min(jnp.where(le == m1, row, float(SUBLANES)), axis=0, keepdims=True)
    le2 = jnp.where(row == i1, neg, le)
    m2 = jnp.max(le2, axis=0, keepdims=True)
    i2 = jnp.min(jnp.where(le2 == m2, row, float(SUBLANES)), axis=0, keepdims=True)
    e21 = jnp.exp(m2 - m1)
    inv = 1.0 / (1.0 + e21)
    g1 = pg_top * inv
    g2 = pg_top * e21 * inv
    e1 = gidx * EXPERTS_PER_GROUP + i1
    e2 = gidx * EXPERTS_PER_GROUP + i2

    erow = lax.broadcasted_iota(I32, (N_EXPERTS, T), 0).astype(F32)
    oh1 = erow == e1
    oh2 = erow == e2
    oh = jnp.where(oh1, 1.0, jnp.where(oh2, 1.0, 0.0))
    before = _dot(oh.astype(BF16), tri_ref[...]) + carry_ref[...]
    r1 = jnp.sum(jnp.where(oh1, before, 0.0), axis=0, keepdims=True)
    r2 = jnp.sum(jnp.where(oh2, before, 0.0), axis=0, keepdims=True)
    carry = carry_ref[...] + jnp.sum(oh, axis=1, keepdims=True)
    carry_ref[...] = carry
    cnt_ref[...] = carry[:, :LANES].astype(I32)

    ri = jnp.where(row == 0.0, e1, jnp.where(row == 1.0, e2, jnp.where(row == 2.0, r1, jnp.where(row == 3.0, r2, 0.0))))
    ri_ref[...] = ri.astype(I32)
    rf_ref[...] = jnp.where(row == 0.0, g1, jnp.where(row == 1.0, g2, 0.0))


def _router_specs(T, n_tok):
    const = lambda shape: pl.BlockSpec(shape, lambda i: (0,) * len(shape))
    in_specs = [const((1, D_MODEL)), const((D_MODEL, 2 * LANES)), const((1, LANES)), const((T, T))]
    out_specs = [
        pl.BlockSpec((T * ROW_TILE, LANES), lambda i: (i, 0)),
        pl.BlockSpec((SUBLANES, T), lambda i: (0, i)),
        pl.BlockSpec((SUBLANES, T), lambda i: (0, i)),
        const((N_EXPERTS, LANES)),
    ]
    out_shape = [
        jax.ShapeDtypeStruct((n_tok * ROW_TILE, LANES), F32),
        jax.ShapeDtypeStruct((SUBLANES, n_tok), I32),
        jax.ShapeDtypeStruct((SUBLANES, n_tok), F32),
        jax.ShapeDtypeStruct((N_EXPERTS, LANES), I32),
    ]
    return in_specs, out_specs, out_shape


def _attn_out_kernel(h_ref, ot_ref, wo_ref, gm_ref, rw_ref, rb_ref, tri_ref,
                     h1_ref, xn_ref, ri_ref, rf_ref, cnt_ref, carry_ref):
    x1 = h_ref[...] + _dot_tn(ot_ref[0, 0], wo_ref[...])
    h1_ref[...] = x1
    _router(x1, gm_ref, rw_ref, rb_ref, tri_ref, carry_ref, xn_ref, ri_ref, rf_ref, cnt_ref)


def _attn_out(h, ot, w, lw):
    n_tok = h.shape[0]
    _, n_q, _, tq = ot.shape
    T = T_TOK
    assert tq % T == 0
    per_q = tq // T
    per_b = n_q * per_q
    const = lambda shape: pl.BlockSpec(shape, lambda i: (0,) * len(shape))
    r_in, r_out, r_shape = _router_specs(T, n_tok)
    return pl.pallas_call(
        _attn_out_kernel,
        grid=(n_tok // T,),
        in_specs=[
            pl.BlockSpec((T, D_MODEL), lambda i: (i, 0)),
            pl.BlockSpec((1, 1, N_HEADS * V_DIM, T),
                         lambda i: (i // per_b, (i % per_b) // per_q, 0, i % per_q)),
            const((N_HEADS * V_DIM, D_MODEL)),
        ] + r_in,
        out_specs=[pl.BlockSpec((T, D_MODEL), lambda i: (i, 0))] + r_out,
        out_shape=[jax.ShapeDtypeStruct((n_tok, D_MODEL), F32)] + r_shape,
        scratch_shapes=[pltpu.VMEM((N_EXPERTS, T), F32)],
        compiler_params=_cparams(("arbitrary",)),
        name="attn_out_router",
    )(h, ot, w["w_o"], lw["moe_norm"], lw["rw"], lw["rb"], w["tri"])


def _conv_kernel(h_ref, hp_ref, hn_ref, gn_ref, win_ref, cw_ref, wout_ref, gm_ref, rw_ref, rb_ref, tri_ref,
                 h1_ref, xn_ref, ri_ref, rf_ref, cnt_ref, carry_ref, *, tiles_per_seq):
    T = h_ref.shape[0]
    i = pl.program_id(0)
    x = h_ref[...]
    gn = gn_ref[...]
    bcu = _dot(_rms(x, gn).astype(BF16), win_ref[...])
    b = bcu[:, :D_MODEL]
    cu = bcu[:, D_MODEL:2 * D_MODEL] * bcu[:, 2 * D_MODEL:]
    halo = jnp.concatenate([hp_ref[...], hn_ref[...]], axis=0)
    hcu = _dot(_rms(halo, gn).astype(BF16), win_ref[:, D_MODEL:])
    hcu = hcu[:, :D_MODEL] * hcu[:, D_MODEL:]
    first = (i % tiles_per_seq) == 0
    last = (i % tiles_per_seq) == tiles_per_seq - 1
    cu_before = jnp.where(first, 0.0, hcu[SUBLANES - 1:SUBLANES])
    cu_after = jnp.where(last, 0.0, hcu[SUBLANES:SUBLANES + 1])
    row = lax.broadcasted_iota(I32, (T, 1), 0)
    prev = jnp.where(row == 0, cu_before, pltpu.roll(cu, 1, 0))
    nxt = jnp.where(row == T - 1, cu_after, pltpu.roll(cu, T - 1, 0))
    cw = cw_ref[...]
    y = cw[0:1] * prev + cw[1:2] * cu + cw[2:3] * nxt
    x1 = x + _dot((b * y).astype(BF16), wout_ref[...])
    h1_ref[...] = x1
    _router(x1, gm_ref, rw_ref, rb_ref, tri_ref, carry_ref, xn_ref, ri_ref, rf_ref, cnt_ref)


def _conv(h, S, w, lw):
    n_tok = h.shape[0]
    T = T_TOK
    nS = S // T
    rows8 = T // SUBLANES
    n8 = n_tok // SUBLANES
    const = lambda shape: pl.BlockSpec(shape, lambda i: (0,) * len(shape))
    r_in, r_out, r_shape = _router_specs(T, n_tok)
    return pl.pallas_call(
        functools.partial(_conv_kernel, tiles_per_seq=nS),
        grid=(n_tok // T,),
        in_specs=[
            pl.BlockSpec((T, D_MODEL), lambda i: (i, 0)),
            pl.BlockSpec((SUBLANES, D_MODEL), lambda i: (jnp.maximum(i * rows8 - 1, 0), 0)),
            pl.BlockSpec((SUBLANES, D_MODEL), lambda i: (jnp.minimum((i + 1) * rows8, n8 - 1), 0)),
            const((1, D_MODEL)), const((D_MODEL, 3 * D_MODEL)), const((SUBLANES, D_MODEL)),
            const((D_MODEL, D_MODEL)),
        ] + r_in,
        out_specs=[pl.BlockSpec((T, D_MODEL), lambda i: (i, 0))] + r_out,
        out_shape=[jax.ShapeDtypeStruct((n_tok, D_MODEL), F32)] + r_shape,
        scratch_shapes=[pltpu.VMEM((N_EXPERTS, T), F32)],
        compiler_params=_cparams(("arbitrary",)),
        name="conv_router",
    )(h, h, h, w["conv_norm"], w["conv_w_in"], w["conv_w"], w["conv_w_out"],
      lw["moe_norm"], lw["rw"], lw["rb"], w["tri"])


def _plan_kernel(cnt_ref, ri_ref, dest_ref, info_ref, *, n_blocks):
    ri = ri_ref[...]
    eid = ri[0:2]
    offset = jnp.zeros(eid.shape, I32)
    blocks_before = jnp.int32(0)
    for e in range(N_EXPERTS):
        nb = (cnt_ref[e, 0] + (EXPERT_BLOCK - 1)) // EXPERT_BLOCK
        offset = jnp.where(eid == e, blocks_before * EXPERT_BLOCK, offset)

        def fill(j, c, e=e, base=blocks_before):
            info_ref[base + j] = e
            return c

        lax.fori_loop(0, nb, fill, 0)
        blocks_before = blocks_before + nb

    def fill_tail(j, c):
        info_ref[j] = N_EXPERTS - 1
        return c

    lax.fori_loop(blocks_before, n_blocks, fill_tail, 0)
    info_ref[n_blocks] = blocks_before
    dest_ref[...] = jnp.zeros(dest_ref.shape, I32)
    dest_ref[0:2, :] = ri[2:4] + offset


def _plan(cnt, ri, n_blocks):
    n_tok = ri.shape[1]
    return pl.pallas_call(
        functools.partial(_plan_kernel, n_blocks=n_blocks),
        in_specs=[pl.BlockSpec(memory_space=pltpu.SMEM), pl.BlockSpec(memory_space=pltpu.VMEM)],
        out_specs=[pl.BlockSpec(memory_space=pltpu.VMEM), pl.BlockSpec(memory_space=pltpu.SMEM)],
        out_shape=[jax.ShapeDtypeStruct((SUBLANES, n_tok), I32),
                   jax.ShapeDtypeStruct((n_blocks + 1,), I32)],
        compiler_params=pltpu.CompilerParams(vmem_limit_bytes=VMEM_LIMIT),
        name="plan",
    )(cnt, ri)


def _store_row_tiles(ref, r0, x):
    for s in range(ROW_TILE):
        ref[pl.ds(r0 * ROW_TILE + s, x.shape[0], stride=ROW_TILE), :] = x[:, s * LANES:(s + 1) * LANES]


def _load_row_tiles(ref, r0, n):
    cols = [ref[pl.ds(r0 * ROW_TILE + s, n, stride=ROW_TILE), :] for s in range(ROW_TILE)]
    return jnp.concatenate(cols, axis=1)


def _row_copy(src_ref, s, dst_ref, d, sem):
    src = src_ref.at[pl.ds(pl.multiple_of(s * ROW_TILE, ROW_TILE), ROW_TILE)]
    dst = dst_ref.at[pl.ds(pl.multiple_of(d * ROW_TILE, ROW_TILE), ROW_TILE)]
    return pltpu.make_async_copy(src, dst, sem)


def _dispatch_kernel(d0_ref, d1_ref, x_ref, xb_in_ref, xb_ref, sem):
    del xb_in_ref
    T = x_ref.shape[0] // ROW_TILE
    base = pl.program_id(0) * T

    def start(t, c):
        _row_copy(x_ref, t, xb_ref, d0_ref[base + t], sem.at[0]).start(priority=0)
        _row_copy(x_ref, t, xb_ref, d1_ref[base + t], sem.at[1]).start(priority=1)
        return c

    lax.fori_loop(0, T, start, 0, unroll=8)

    for k in range(2):
        pltpu.make_async_copy(x_ref, xb_ref.at[pl.ds(0, T * ROW_TILE)], sem.at[k]).wait()


def _dispatch(xn, d0, d1, n_rows):
    n_tok = xn.shape[0] // ROW_TILE
    T = T_TOK
    xb0 = jnp.zeros((n_rows * ROW_TILE, LANES), F32)
    return pl.pallas_call(
        _dispatch_kernel,
        grid_spec=pltpu.PrefetchScalarGridSpec(
            num_scalar_prefetch=2,
            grid=(n_tok // T,),
            in_specs=[pl.BlockSpec((T * ROW_TILE, LANES), lambda i, d0, d1: (i, 0)),
                      pl.BlockSpec(memory_space=pl.ANY)],
            out_specs=pl.BlockSpec(memory_space=pl.ANY),
            scratch_shapes=[pltpu.SemaphoreType.DMA((2,))],
        ),
        out_shape=jax.ShapeDtypeStruct((n_rows * ROW_TILE, LANES), F32),
        input_output_aliases={3: 0},
        compiler_params=_cparams(("arbitrary",)),
        name="dispatch",
    )(d0, d1, xn, xb0)


def _expert_kernel(info_ref, xb_ref, *refs, n_blocks):
    yb_ref = refs[-1]
    i = pl.program_id(0)
    blk0 = i * BLOCKS_PER_STEP
    n_used = info_ref[n_blocks]

    def swiglu(r0, n, wg_ref, wu_ref, wd_ref):
        x = _load_row_tiles(xb_ref, r0, n).astype(BF16)
        g = _dot(x, wg_ref[0])
        up = _dot(x, wu_ref[0])
        hmid = (g * jax.nn.sigmoid(g) * up).astype(BF16)
        _store_row_tiles(yb_ref, r0, _dot(hmid, wd_ref[0]))

    one_expert = jnp.logical_and(info_ref[blk0] == info_ref[blk0 + BLOCKS_PER_STEP - 1],
                                 blk0 + BLOCKS_PER_STEP - 1 < n_used)

    @pl.when(one_expert)
    def _():
        swiglu(0, BLOCKS_PER_STEP * EXPERT_BLOCK, *refs[0:3])

    for u in range(BLOCKS_PER_STEP):
        blk = blk0 + u

        @pl.when(jnp.logical_and(jnp.logical_not(one_expert), blk < n_used))
        def _():
            swiglu(u * EXPERT_BLOCK, EXPERT_BLOCK, *refs[3 * u:3 * u + 3])

        @pl.when(blk >= n_used)
        def _():
            n = EXPERT_BLOCK * ROW_TILE
            yb_ref[pl.ds(u * n, n), :] = jnp.zeros((n, LANES), F32)


def _experts(info, xb, lw):
    n_rows = xb.shape[0] // ROW_TILE
    n_blocks = n_rows // EXPERT_BLOCK
    assert n_blocks % BLOCKS_PER_STEP == 0
    step_rows = BLOCKS_PER_STEP * EXPERT_BLOCK * ROW_TILE
    w_specs, w_args = [], []
    for u in range(BLOCKS_PER_STEP):
        pick = lambda i, info, u=u: (info[i * BLOCKS_PER_STEP + u], 0, 0)
        w_specs += [pl.BlockSpec((1, D_MODEL, D_EXPERT), pick), pl.BlockSpec((1, D_MODEL, D_EXPERT), pick),
                    pl.BlockSpec((1, D_EXPERT, D_MODEL), pick)]
        w_args += [lw["w_gate"], lw["w_up"], lw["w_down"]]
    return pl.pallas_call(
        functools.partial(_expert_kernel, n_blocks=n_blocks),
        grid_spec=pltpu.PrefetchScalarGridSpec(
            num_scalar_prefetch=1,
            grid=(n_blocks // BLOCKS_PER_STEP,),
            in_specs=[pl.BlockSpec((step_rows, LANES), lambda i, info: (i, 0))] + w_specs,
            out_specs=pl.BlockSpec((step_rows, LANES), lambda i, info: (i, 0)),
        ),
        out_shape=jax.ShapeDtypeStruct((n_rows * ROW_TILE, LANES), F32),
        compiler_params=_cparams(("arbitrary",)),
        name="experts",
    )(info, xb, *w_args)


def _combine_kernel(d0_ref, d1_ref, h_ref, rf_ref, p_ref, yb_ref, gn_ref, wg_ref, wp_ref, o_ref,
                    y0_ref, y1_ref, sem):
    T = h_ref.shape[0]
    base = pl.program_id(0) * T

    def start(t, c):
        _row_copy(yb_ref, d0_ref[base + t], y0_ref, t, sem.at[0]).start(priority=0)
        _row_copy(yb_ref, d1_ref[base + t], y1_ref, t, sem.at[1]).start(priority=1)
        return c

    lax.fori_loop(0, T, start, 0, unroll=8)

    proj = _dot(p_ref[...].astype(BF16), wp_ref[...])
    gates = jnp.concatenate([rf_ref[...], jnp.zeros((LANES - SUBLANES, T), F32)], axis=0).T

    pltpu.make_async_copy(yb_ref.at[pl.ds(0, T * ROW_TILE)], y0_ref, sem.at[0]).wait()
    pltpu.make_async_copy(yb_ref.at[pl.ds(0, T * ROW_TILE)], y1_ref, sem.at[1]).wait()

    y0 = _load_row_tiles(y0_ref, 0, T)
    y1 = _load_row_tiles(y1_ref, 0, T)
    h2 = h_ref[...] + gates[:, 0:1] * y0 + gates[:, 1:2] * y1
    gate = jax.nn.sigmoid(_dot(_rms(h2, gn_ref[...]).astype(BF16), wg_ref[...]))
    o_ref[...] = h2 + gate * proj


def _combine(h, rf, p, yb, d0, d1, lw):
    n_tok = h.shape[0]
    T = T_TOK
    const = lambda shape: pl.BlockSpec(shape, lambda i, d0, d1: (0,) * len(shape))
    return pl.pallas_call(
        _combine_kernel,
        grid_spec=pltpu.PrefetchScalarGridSpec(
            num_scalar_prefetch=2,
            grid=(n_tok // T,),
            in_specs=[
                pl.BlockSpec((T, D_MODEL), lambda i, d0, d1: (i, 0)),
                pl.BlockSpec((SUBLANES, T), lambda i, d0, d1: (0, i)),
                pl.BlockSpec((T, PLE_DIM), lambda i, d0, d1: (i, 0)),
                pl.BlockSpec(memory_space=pl.ANY),
                const((1, D_MODEL)), const((D_MODEL, D_MODEL)), const((PLE_DIM, D_MODEL)),
            ],
            out_specs=pl.BlockSpec((T, D_MODEL), lambda i, d0, d1: (i, 0)),
            scratch_shapes=[pltpu.VMEM((T * ROW_TILE, LANES), F32), pltpu.VMEM((T * ROW_TILE, LANES), F32),
                            pltpu.SemaphoreType.DMA((2,))],
        ),
        out_shape=jax.ShapeDtypeStruct((n_tok, D_MODEL), F32),
        compiler_params=_cparams(("arbitrary",)),
        name="combine_ple",
    )(d0, d1, h, rf, p, yb, lw["ple_norm"], lw["ple_w_gate"], lw["ple_w_proj"])


def _pad_lanes(x, lo, width=LANES):
    n = x.shape[-1]
    pad = [(0, 0)] * (x.ndim - 1) + [(lo, width - lo - n)]
    return jnp.pad(x, pad)


def _rotate_half_cols(x):
    half = QK_ROPE // 2
    return jnp.concatenate([-x[..., half:], x[..., :half]], axis=-1)


def _swap_halves(g):
    half = QK_ROPE // 2
    return jnp.concatenate([g[..., half:], g[..., :half]], axis=-1)


def _prepare(attn_norm, attn_w_in, attn_q_a_norm, attn_kv_a_norm, attn_w_qb, attn_w_kvb, attn_q_gain,
             attn_k_gain, attn_w_o, conv_norm, conv_w_in, conv_w, conv_w_out, moe_norm, router_group_w,
             router_group_b, router_expert_w, router_expert_b, expert_w_gate, expert_w_up, expert_w_down,
             ple_norm, ple_w_gate, ple_w_proj):
    w = {}
    w_in = attn_w_in[0]
    rope_cols = w_in[:, Q_LORA + KV_LORA:]
    w["w_in_ext"] = jnp.concatenate(
        [w_in[:, :Q_LORA + KV_LORA], _pad_lanes(rope_cols, QK_NOPE), _pad_lanes(_rotate_half_cols(rope_cols), QK_NOPE)],
        axis=1).astype(BF16)
    w["attn_norm"] = attn_norm[0][None]
    w["g_qa"] = attn_q_a_norm[0][None]
    w["g_kva"] = attn_kv_a_norm[0][None]
    wqb = attn_w_qb[0].reshape(Q_LORA, N_HEADS, QK_DIM)
    w["wq"] = _pad_lanes(wqb, 0).reshape(Q_LORA, HEAD_PAD).astype(BF16)
    w["wq_sw"] = _pad_lanes(_rotate_half_cols(wqb[..., QK_NOPE:]), QK_NOPE).reshape(Q_LORA, HEAD_PAD).astype(BF16)
    wkvb = attn_w_kvb[0].reshape(KV_LORA, N_HEADS, QK_NOPE + V_DIM)
    w["wk"] = _pad_lanes(wkvb[..., :QK_NOPE], 0).reshape(KV_LORA, HEAD_PAD).astype(BF16)
    w["wvt"] = _pad_lanes(wkvb[..., QK_NOPE:], 0, V_EXT).reshape(KV_LORA, N_HEADS * V_EXT).T.astype(BF16)
    scale = LOG2_E / float(QK_DIM) ** 0.5
    gqs = attn_q_gain[0] * scale
    gk = attn_k_gain[0]
    zero = jnp.zeros((SUBLANES - 2, LANES), F32)
    key_norm_bound = jnp.full((1, LANES), BOUND_MARGIN * float(QK_DIM) ** 0.5, F32) * jnp.max(jnp.abs(gk))
    w["gq"] = jnp.concatenate([_pad_lanes(gqs, 0)[None], _pad_lanes(_swap_halves(gqs[QK_NOPE:]), QK_NOPE)[None],
                               key_norm_bound, zero[1:]])
    w["gk"] = jnp.concatenate([_pad_lanes(gk[:QK_NOPE], 0)[None], _pad_lanes(gk[QK_NOPE:], QK_NOPE)[None],
                               _pad_lanes(_swap_halves(gk[QK_NOPE:]), QK_NOPE)[None], zero[1:]])
    w["w_o"] = attn_w_o[0].astype(BF16)
    w["conv_norm"] = conv_norm[0][None]
    w["conv_w_in"] = conv_w_in[0].astype(BF16)
    w["conv_w"] = jnp.pad(conv_w[0], ((0, SUBLANES - conv_w.shape[1]), (0, 0)))
    w["conv_w_out"] = conv_w_out[0].astype(BF16)
    tri = lax.broadcasted_iota(I32, (T_TOK, T_TOK), 0) < lax.broadcasted_iota(I32, (T_TOK, T_TOK), 1)
    w["tri"] = tri.astype(BF16)
    layers = []
    for i in range(moe_norm.shape[0]):
        rw = jnp.zeros((D_MODEL, LANES), F32)
        rw = rw.at[:, GROUP_COL0:GROUP_COL0 + N_GROUPS].set(router_group_w[i])
        rw = rw.at[:, EXPERT_COL0:EXPERT_COL0 + N_EXPERTS].set(router_expert_w[i])
        rw_hi = rw.astype(BF16)
        rw_lo = (rw - rw_hi.astype(F32)).astype(BF16)
        rb = jnp.zeros((1, LANES), F32)
        rb = rb.at[0, GROUP_COL0:GROUP_COL0 + N_GROUPS].set(router_group_b[i])
        rb = rb.at[0, EXPERT_COL0:EXPERT_COL0 + N_EXPERTS].set(router_expert_b[i])
        layers.append(dict(
            moe_norm=moe_norm[i][None], rw=jnp.concatenate([rw_hi, rw_lo], axis=1), rb=rb,
            w_gate=expert_w_gate[i].astype(BF16), w_up=expert_w_up[i].astype(BF16),
            w_down=expert_w_down[i].astype(BF16),
            ple_norm=ple_norm[i][None], ple_w_gate=ple_w_gate[i].astype(BF16),
            ple_w_proj=ple_w_proj[i].astype(BF16)))
    return w, layers


def _rope_tiles(S):
    pos = jnp.arange(S, dtype=F32)
    inv = ROPE_THETA ** (-jnp.arange(0, QK_ROPE, 2, dtype=F32) / QK_ROPE)
    ang = pos[:, None] * inv[None, :]
    cos, sin = jnp.cos(ang), jnp.sin(ang)
    cos_t = jnp.concatenate([jnp.ones((S, QK_NOPE), F32), cos, cos, jnp.zeros((S, LANES - QK_DIM), F32)], axis=1)
    sin_t = jnp.concatenate([jnp.zeros((S, QK_NOPE), F32), sin, sin, jnp.zeros((S, LANES - QK_DIM), F32)], axis=1)
    return cos_t, sin_t


def _moe_ple(h1, xn, ri, rf, cnt, p, lw):
    n_tok = h1.shape[0]
    n_assign = 2 * n_tok
    n_rows = -(-n_assign // EXPERT_BLOCK) * EXPERT_BLOCK + N_EXPERTS * EXPERT_BLOCK
    dest, info = _plan(cnt, ri, n_rows // EXPERT_BLOCK)
    d0, d1 = dest[0], dest[1]
    xb = _dispatch(xn, d0, d1, n_rows)
    yb = _experts(info, xb, lw)
    return _combine(h1, rf, p, yb, d0, d1, lw)


def _trunk(x, p, w, layers):
    B, S, _ = x.shape
    n_tok = B * S
    cos_t, sin_t = _rope_tiles(S)
    h = x.reshape(n_tok, D_MODEL)
    p = p.reshape(p.shape[0], n_tok, PLE_DIM)
    q, k, vt = _attn_in(x, w, cos_t, sin_t)
    ot = _flash(q, k, vt)
    h1, xn, ri, rf, cnt = _attn_out(h, ot, w, layers[0])
    h = _moe_ple(h1, xn, ri, rf, cnt, p[0], layers[0])
    h1, xn, ri, rf, cnt = _conv(h, S, w, layers[1])
    h = _moe_ple(h1, xn, ri, rf, cnt, p[1], layers[1])
    return h.reshape(B, S, D_MODEL)


def kernel(x_prompt, x_sample, p_prompt, p_sample, attn_norm, attn_w_in, attn_q_a_norm, attn_kv_a_norm, attn_w_qb, attn_w_kvb, attn_q_gain, attn_k_gain, attn_w_o, conv_norm, conv_w_in, conv_w, conv_w_out, moe_norm, router_group_w, router_group_b, router_expert_w, router_expert_b, expert_w_gate, expert_w_up, expert_w_down, ple_norm, ple_w_gate, ple_w_proj):
    assert x_prompt.shape[-1] == D_MODEL and moe_norm.shape[0] == 2
    assert attn_w_in.shape == (1, D_MODEL, Q_LORA + KV_LORA + QK_ROPE)
    assert attn_w_qb.shape == (1, Q_LORA, N_HEADS * QK_DIM)
    assert attn_w_kvb.shape == (1, KV_LORA, N_HEADS * (QK_NOPE + V_DIM))
    assert expert_w_gate.shape[1:] == (N_EXPERTS, D_MODEL, D_EXPERT)
    w, layers = _prepare(attn_norm, attn_w_in, attn_q_a_norm, attn_kv_a_norm, attn_w_qb, attn_w_kvb,
                         attn_q_gain, attn_k_gain, attn_w_o, conv_norm, conv_w_in, conv_w, conv_w_out,
                         moe_norm, router_group_w, router_group_b, router_expert_w, router_expert_b,
                         expert_w_gate, expert_w_up, expert_w_down, ple_norm, ple_w_gate, ple_w_proj)
    return (_trunk(x_prompt, p_prompt, w, layers), _trunk(x_sample, p_sample, w, layers))
```

```python
import functools

import jax
import jax.numpy as jnp
from jax import lax
from jax.experimental import pallas as pl
from jax.experimental.pallas import tpu as pltpu

F32, BF16, I32 = jnp.float32, jnp.bfloat16, jnp.int32

D_MODEL = 1024
N_HEADS = 16
QK_NOPE, QK_ROPE, V_DIM = 64, 32, 64
QK_DIM = QK_NOPE + QK_ROPE
Q_LORA, KV_LORA = 384, 256
N_GROUPS, EXPERTS_PER_GROUP = 4, 8
N_EXPERTS = N_GROUPS * EXPERTS_PER_GROUP
D_EXPERT = 512
PLE_DIM = 256
ROPE_THETA = 10000.0
RMS_EPS = 1e-6
EXPERT_BLOCK = 128

LANES = 128
SUBLANES = 8
VMEM_LIMIT = 56 * 1024 * 1024
ROW_TILE = D_MODEL // LANES

T_ATTN_IN = 256
T_Q = 512
T_TOK = 512
KV_MERGE = 1
KV_UNROLL = 8
BLOCKS_PER_STEP = 4
LOG2_E = 1.4426950408889634
BOUND_MARGIN = 1.02
MIN_DENOMINATOR = 2.0 ** -60

BF16_ROWS = 16
V_EXT = V_DIM + BF16_ROWS
W_IN_EXT = Q_LORA + KV_LORA + 2 * LANES
HEAD_PAD = N_HEADS * LANES
GROUP_COL0, EXPERT_COL0 = 0, SUBLANES


def _cparams(sem):
    return pltpu.CompilerParams(dimension_semantics=sem, vmem_limit_bytes=VMEM_LIMIT)


def _rms(x, g):
    return x * lax.rsqrt(jnp.mean(x * x, axis=-1, keepdims=True) + RMS_EPS) * g


def _dot(a, b):
    return jnp.dot(a, b, preferred_element_type=F32)


def _dot_nt(a, b):
    return lax.dot_general(a, b, (((1,), (1,)), ((), ())), preferred_element_type=F32)


def _dot_tn(a, b):
    return lax.dot_general(a, b, (((0,), (0,)), ((), ())), preferred_element_type=F32)


def _attn_in_kernel(x_ref, gn_ref, win_ref, gqa_ref, gkva_ref, wq_ref, wqs_ref, wk_ref, wvt_ref,
                    gq_ref, gk_ref, cos_ref, sin_ref, q_ref, k_ref, vt_ref):
    x = x_ref[0]
    xn = _rms(x, gn_ref[...]).astype(BF16)
    a = _dot(xn, win_ref[...])
    cq = _rms(a[:, :Q_LORA], gqa_ref[...]).astype(BF16)
    ckv = _rms(a[:, Q_LORA:Q_LORA + KV_LORA], gkva_ref[...]).astype(BF16)
    kr = a[:, Q_LORA + KV_LORA:Q_LORA + KV_LORA + LANES]
    kr_sw = a[:, Q_LORA + KV_LORA + LANES:]
    cosv, sinv = cos_ref[...], sin_ref[...]
    gq, gk = gq_ref[...], gk_ref[...]

    q = _dot(cq, wq_ref[...])
    q_sw = _dot(cq, wqs_ref[...])
    kn = _dot(ckv, wk_ref[...])
    vt = _dot_nt(wvt_ref[...], ckv)
    vrow = lax.broadcasted_iota(I32, vt.shape, 0) % V_EXT
    vt_ref[0, 0] = jnp.where(vrow == V_DIM, 1.0, vt).astype(BF16)

    ssq_rope = jnp.sum(kr * kr, axis=-1, keepdims=True)
    k_rope = kr * (gk[1:2] * cosv) + kr_sw * (gk[2:3] * sinv)
    q_cos, q_sin = gq[0:1] * cosv, gq[1:2] * sinv
    inv_d = 1.0 / QK_DIM
    bound_lane = lax.broadcasted_iota(I32, (x.shape[0], LANES), 1) == QK_DIM
    for h in range(N_HEADS):
        sl = slice(h * LANES, (h + 1) * LANES)
        qh = q[:, sl]
        rq = lax.rsqrt(jnp.sum(qh * qh, axis=-1, keepdims=True) * inv_d + RMS_EPS)
        qt = (qh * q_cos + q_sw[:, sl] * q_sin) * rq
        bound = jnp.sqrt(jnp.sum(qt * qt, axis=-1, keepdims=True)) * gq[2:3]
        q_ref[0, h] = jnp.where(bound_lane, -bound, qt).astype(BF16)
        kh = kn[:, sl]
        rk = lax.rsqrt((jnp.sum(kh * kh, axis=-1, keepdims=True) + ssq_rope) * inv_d + RMS_EPS)
        k_ref[0, h] = jnp.where(bound_lane, 1.0, (kh * gk[0:1] + k_rope) * rk).astype(BF16)


def _attn_in(x, w, cos_t, sin_t):
    B, S, _ = x.shape
    T = T_ATTN_IN
    nS = S // T
    const = lambda shape: pl.BlockSpec(shape, lambda b, j: (0,) * len(shape))
    return pl.pallas_call(
        _attn_in_kernel,
        grid=(B, nS),
        in_specs=[
            pl.BlockSpec((1, T, D_MODEL), lambda b, j: (b, j, 0)),
            const((1, D_MODEL)), const((D_MODEL, W_IN_EXT)), const((1, Q_LORA)), const((1, KV_LORA)),
            const((Q_LORA, HEAD_PAD)), const((Q_LORA, HEAD_PAD)), const((KV_LORA, HEAD_PAD)),
            const((N_HEADS * V_EXT, KV_LORA)), const((SUBLANES, LANES)), const((SUBLANES, LANES)),
            pl.BlockSpec((T, LANES), lambda b, j: (j, 0)),
            pl.BlockSpec((T, LANES), lambda b, j: (j, 0)),
        ],
        out_specs=[
            pl.BlockSpec((1, N_HEADS, T, LANES), lambda b, j: (b, 0, j, 0)),
            pl.BlockSpec((1, N_HEADS, T, LANES), lambda b, j: (b, 0, j, 0)),
            pl.BlockSpec((1, 1, N_HEADS * V_EXT, T), lambda b, j: (b, j, 0, 0)),
        ],
        out_shape=[
            jax.ShapeDtypeStruct((B, N_HEADS, S, LANES), BF16),
            jax.ShapeDtypeStruct((B, N_HEADS, S, LANES), BF16),
            jax.ShapeDtypeStruct((B, nS, N_HEADS * V_EXT, T), BF16),
        ],
        compiler_params=_cparams(("parallel", "parallel")),
        name="attn_in",
    )(x, w["attn_norm"], w["w_in_ext"], w["g_qa"], w["g_kva"], w["wq"], w["wq_sw"], w["wk"], w["wvt"],
      w["gq"], w["gk"], cos_t, sin_t)


def _flash_kernel(q_ref, k_ref, vt_ref, o_ref, m_ref, acc_ref, s_ref, sr_ref, cm_ref, *, n_kv, t_kv, merge, unroll, tq):
    t_blk = merge * t_kv
    n_blk = n_kv // merge
    n_q = q_ref.shape[2] // tq

    def scores(qi, j):
        kb = k_ref[0, 0, pl.ds(pl.multiple_of(j * t_blk, t_blk), t_blk), :]
        q = q_ref[0, 0, pl.ds(pl.multiple_of(qi * tq, tq), tq), :]
        return _dot_nt(kb, q)

    def pv(j, p):
        out = _dot(vt_ref[0, j * merge], p[0:t_kv])
        for i in range(1, merge):
            out = out + _dot(vt_ref[0, j * merge + i], p[i * t_kv:(i + 1) * t_kv])
        return out

    def finish(qi, acc):
        o_ref[0, qi] = (acc[0:V_DIM] * (1.0 / acc[V_DIM:V_DIM + 1])).astype(BF16)

    def running_max_tile(qi):
        m_ref[...] = jnp.full(m_ref.shape, -jnp.inf, F32)
        acc_ref[...] = jnp.zeros(acc_ref.shape, F32)
        s0 = scores(qi, 0)
        sr_ref[0] = s0
        cm_ref[...] = jnp.max(s0, axis=0, keepdims=True)
        sr_ref[1] = scores(qi, min(1, n_blk - 1))

        def body(it, carry):
            s, s_next, cm = sr_ref[0], sr_ref[1], cm_ref[...]
            m, acc = m_ref[...], acc_ref[...]
            for u in range(unroll):
                j = it * unroll + u
                s_next2 = scores(qi, jnp.minimum(j + 2, n_blk - 1))
                cm_next = jnp.max(s_next, axis=0, keepdims=True)
                m_new = jnp.maximum(m, cm)
                p = jnp.exp2(s - m_new)
                alpha = jnp.exp2(m - m_new)
                acc = alpha * acc + pv(j, p.astype(BF16))
                m, s, cm, s_next = m_new, s_next, cm_next, s_next2
            sr_ref[0], sr_ref[1], cm_ref[...] = s, s_next, cm
            m_ref[...], acc_ref[...] = m, acc
            return carry

        lax.fori_loop(0, n_blk // unroll, body, 0)
        finish(qi, acc_ref[...])

    assert n_blk >= 2
    s_ref[0] = scores(0, 0)
    s_ref[1] = scores(0, 1)

    def q_tile(qi, carry):
        s, s_next = s_ref[0], s_ref[1]
        qn = jnp.minimum(qi + 1, n_q - 1)
        acc = jnp.zeros((V_EXT, tq), F32)
        for j in range(n_blk):
            s_next2 = scores(qi, j + 2) if j + 2 < n_blk else scores(qn, j + 2 - n_blk)
            acc = acc + pv(j, jnp.exp2(s).astype(BF16))
            s, s_next = s_next, s_next2
        s_ref[0], s_ref[1] = s, s_next
        acc_ref[...] = acc
        safe = jnp.min(acc_ref[V_DIM:V_DIM + 1]) >= MIN_DENOMINATOR

        @pl.when(safe)
        def _():
            finish(qi, acc_ref[...])

        @pl.when(jnp.logical_not(safe))
        def _():
            running_max_tile(qi)

        return carry

    lax.fori_loop(0, n_q, q_tile, 0)


def _flash(q, k, vt):
    B, H, S, _ = q.shape
    t_kv = vt.shape[-1]
    n_kv = S // t_kv
    tq = min(T_Q, S)
    n_q = S // tq
    merge = KV_MERGE if n_kv % KV_MERGE == 0 else 1
    n_blk = n_kv // merge
    unroll = max(u for u in (KV_UNROLL, 4, 2, 1) if n_blk % u == 0)
    return pl.pallas_call(
        functools.partial(_flash_kernel, n_kv=n_kv, t_kv=t_kv, merge=merge, unroll=unroll, tq=tq),
        grid=(B, H),
        in_specs=[
            pl.BlockSpec((1, 1, S, LANES), lambda b, h: (b, h, 0, 0)),
            pl.BlockSpec((1, 1, S, LANES), lambda b, h: (b, h, 0, 0)),
            pl.BlockSpec((1, n_kv, V_EXT, t_kv), lambda b, h: (b, 0, h, 0)),
        ],
        out_specs=pl.BlockSpec((1, n_q, V_DIM, tq), lambda b, h: (b, 0, h, 0)),
        out_shape=jax.ShapeDtypeStruct((B, n_q, H * V_DIM, tq), BF16),
        scratch_shapes=[pltpu.VMEM((1, tq), F32), pltpu.VMEM((V_EXT, tq), F32),
                        pltpu.VMEM((2, merge * t_kv, tq), F32), pltpu.VMEM((2, merge * t_kv, tq), F32),
                        pltpu.VMEM((1, tq), F32)],
        compiler_params=_cparams(("parallel", "parallel")),
        name="flash",
    )(q, k, vt)


def _router(x1, gm_ref, rw_ref, rb_ref, tri_ref, carry_ref, xn_ref, ri_ref, rf_ref, cnt_ref):
    T = x1.shape[0]

    @pl.when(pl.program_id(0) == 0)
    def _():
        carry_ref[...] = jnp.zeros(carry_ref.shape, F32)

    xn = _rms(x1, gm_ref[...])
    _store_row_tiles(xn_ref, 0, xn)
    x_hi = xn.astype(BF16)
    x_lo = (xn - x_hi.astype(F32)).astype(BF16)
    rw = rw_ref[...]
    l_hi = _dot(x_hi, rw)
    l_lo = _dot(x_lo, rw[:, :LANES])
    logits = l_hi[:, :LANES] + l_hi[:, LANES:] + l_lo + rb_ref[...]
    lt = logits.T

    row = lax.broadcasted_iota(I32, (SUBLANES, T), 0).astype(F32)
    neg = -jnp.inf
    lg = jnp.where(row < N_GROUPS, lt[GROUP_COL0:GROUP_COL0 + SUBLANES], neg)
    mg = jnp.max(lg, axis=0, keepdims=True)
    gidx = jnp.min(jnp.where(lg == mg, row, float(SUBLANES)), axis=0, keepdims=True)
    pg_top = 1.0 / jnp.sum(jnp.exp(lg - mg), axis=0, keepdims=True)

    def grp(g):
        return lt[EXPERT_COL0 + g * EXPERTS_PER_GROUP:EXPERT_COL0 + (g + 1) * EXPERTS_PER_GROUP]

    le = jnp.where(gidx == 0.0, grp(0), jnp.where(gidx == 1.0, grp(1), jnp.where(gidx == 2.0, grp(2), grp(3))))
    m1 = jnp.max(le, axis=0, keepdims=True)
    i1 = jnp.min(jnp.where(le == m1, row, float(SUBLANES)), axis=0, keepdims=True)
    le2 = jnp.where(row == i1, neg, le)
    m2 = jnp.max(le2, axis=0, keepdims=True)
    i2 = jnp.min(jnp.where(le2 == m2, row, float(SUBLANES)), axis=0, keepdims=True)
    e21 = jnp.exp(m2 - m1)
    inv = 1.0 / (1.0 + e21)
    g1 = pg_top * inv
    g2 = pg_top * e21 * inv
    e1 = gidx * EXPERTS_PER_GROUP + i1
    e2 = gidx * EXPERTS_PER_GROUP + i2

    erow = lax.broadcasted_iota(I32, (N_EXPERTS, T), 0).astype(F32)
    oh1 = erow == e1
    oh2 = erow == e2
    oh = jnp.where(oh1, 1.0, jnp.where(oh2, 1.0, 0.0))
    before = _dot(oh.astype(BF16), tri_ref[...]) + carry_ref[...]
    r1 = jnp.sum(jnp.where(oh1, before, 0.0), axis=0, keepdims=True)
    r2 = jnp.sum(jnp.where(oh2, before, 0.0), axis=0, keepdims=True)
    carry = carry_ref[...] + jnp.sum(oh, axis=1, keepdims=True)
    carry_ref[...] = carry
    cnt_ref[...] = carry[:, :LANES].astype(I32)

    ri = jnp.where(row == 0.0, e1, jnp.where(row == 1.0, e2, jnp.where(row == 2.0, r1, jnp.where(row == 3.0, r2, 0.0))))
    ri_ref[...] = ri.astype(I32)
    rf_ref[...] = jnp.where(row == 0.0, g1, jnp.where(row == 1.0, g2, 0.0))


def _router_specs(T, n_tok):
    const = lambda shape: pl.BlockSpec(shape, lambda i: (0,) * len(shape))
    in_specs = [const((1, D_MODEL)), const((D_MODEL, 2 * LANES)), const((1, LANES)), const((T, T))]
    out_specs = [
        pl.BlockSpec((T * ROW_TILE, LANES), lambda i: (i, 0)),
        pl.BlockSpec((SUBLANES, T), lambda i: (0, i)),
        pl.BlockSpec((SUBLANES, T), lambda i: (0, i)),
        const((N_EXPERTS, LANES)),
    ]
    out_shape = [
        jax.ShapeDtypeStruct((n_tok * ROW_TILE, LANES), F32),
        jax.ShapeDtypeStruct((SUBLANES, n_tok), I32),
        jax.ShapeDtypeStruct((SUBLANES, n_tok), F32),
        jax.ShapeDtypeStruct((N_EXPERTS, LANES), I32),
    ]
    return in_specs, out_specs, out_shape


def _attn_out_kernel(h_ref, ot_ref, wo_ref, gm_ref, rw_ref, rb_ref, tri_ref,
                     h1_ref, xn_ref, ri_ref, rf_ref, cnt_ref, carry_ref):
    x1 = h_ref[...] + _dot_tn(ot_ref[0, 0], wo_ref[...])
    h1_ref[...] = x1
    _router(x1, gm_ref, rw_ref, rb_ref, tri_ref, carry_ref, xn_ref, ri_ref, rf_ref, cnt_ref)


def _attn_out(h, ot, w, lw):
    n_tok = h.shape[0]
    _, n_q, _, tq = ot.shape
    T = T_TOK
    assert tq % T == 0
    per_q = tq // T
    per_b = n_q * per_q
    const = lambda shape: pl.BlockSpec(shape, lambda i: (0,) * len(shape))
    r_in, r_out, r_shape = _router_specs(T, n_tok)
    return pl.pallas_call(
        _attn_out_kernel,
        grid=(n_tok // T,),
        in_specs=[
            pl.BlockSpec((T, D_MODEL), lambda i: (i, 0)),
            pl.BlockSpec((1, 1, N_HEADS * V_DIM, T),
                         lambda i: (i // per_b, (i % per_b) // per_q, 0, i % per_q)),
            const((N_HEADS * V_DIM, D_MODEL)),
        ] + r_in,
        out_specs=[pl.BlockSpec((T, D_MODEL), lambda i: (i, 0))] + r_out,
        out_shape=[jax.ShapeDtypeStruct((n_tok, D_MODEL), F32)] + r_shape,
        scratch_shapes=[pltpu.VMEM((N_EXPERTS, T), F32)],
        compiler_params=_cparams(("arbitrary",)),
        name="attn_out_router",
    )(h, ot, w["w_o"], lw["moe_norm"], lw["rw"], lw["rb"], w["tri"])


def _conv_kernel(h_ref, hp_ref, hn_ref, gn_ref, win_ref, cw_ref, wout_ref, gm_ref, rw_ref, rb_ref, tri_ref,
                 h1_ref, xn_ref, ri_ref, rf_ref, cnt_ref, carry_ref, *, tiles_per_seq):
    T = h_ref.shape[0]
    i = pl.program_id(0)
    x = h_ref[...]
    gn = gn_ref[...]
    bcu = _dot(_rms(x, gn).astype(BF16), win_ref[...])
    b = bcu[:, :D_MODEL]
    cu = bcu[:, D_MODEL:2 * D_MODEL] * bcu[:, 2 * D_MODEL:]
    halo = jnp.concatenate([hp_ref[...], hn_ref[...]], axis=0)
    hcu = _dot(_rms(halo, gn).astype(BF16), win_ref[:, D_MODEL:])
    hcu = hcu[:, :D_MODEL] * hcu[:, D_MODEL:]
    first = (i % tiles_per_seq) == 0
    last = (i % tiles_per_seq) == tiles_per_seq - 1
    cu_before = jnp.where(first, 0.0, hcu[SUBLANES - 1:SUBLANES])
    cu_after = jnp.where(last, 0.0, hcu[SUBLANES:SUBLANES + 1])
    row = lax.broadcasted_iota(I32, (T, 1), 0)
    prev = jnp.where(row == 0, cu_before, pltpu.roll(cu, 1, 0))
    nxt = jnp.where(row == T - 1, cu_after, pltpu.roll(cu, T - 1, 0))
    cw = cw_ref[...]
    y = cw[0:1] * prev + cw[1:2] * cu + cw[2:3] * nxt
    x1 = x + _dot((b * y).astype(BF16), wout_ref[...])
    h1_ref[...] = x1
    _router(x1, gm_ref, rw_ref, rb_ref, tri_ref, carry_ref, xn_ref, ri_ref, rf_ref, cnt_ref)


def _conv(h, S, w, lw):
    n_tok = h.shape[0]
    T = T_TOK
    nS = S // T
    rows8 = T // SUBLANES
    n8 = n_tok // SUBLANES
    const = lambda shape: pl.BlockSpec(shape, lambda i: (0,) * len(shape))
    r_in, r_out, r_shape = _router_specs(T, n_tok)
    return pl.pallas_call(
        functools.partial(_conv_kernel, tiles_per_seq=nS),
        grid=(n_tok // T,),
        in_specs=[
            pl.BlockSpec((T, D_MODEL), lambda i: (i, 0)),
            pl.BlockSpec((SUBLANES, D_MODEL), lambda i: (jnp.maximum(i * rows8 - 1, 0), 0)),
            pl.BlockSpec((SUBLANES, D_MODEL), lambda i: (jnp.minimum((i + 1) * rows8, n8 - 1), 0)),
            const((1, D_MODEL)), const((D_MODEL, 3 * D_MODEL)), const((SUBLANES, D_MODEL)),
            const((D_MODEL, D_MODEL)),
        ] + r_in,
        out_specs=[pl.BlockSpec((T, D_MODEL), lambda i: (i, 0))] + r_out,
        out_shape=[jax.ShapeDtypeStruct((n_tok, D_MODEL), F32)] + r_shape,
        scratch_shapes=[pltpu.VMEM((N_EXPERTS, T), F32)],
        compiler_params=_cparams(("arbitrary",)),
        name="conv_router",
    )(h, h, h, w["conv_norm"], w["conv_w_in"], w["conv_w"], w["conv_w_out"],
      lw["moe_norm"], lw["rw"], lw["rb"], w["tri"])


def _plan_kernel(cnt_ref, ri_ref, dest_ref, info_ref, *, n_blocks):
    ri = ri_ref[...]
    eid = ri[0:2]
    offset = jnp.zeros(eid.shape, I32)
    blocks_before = jnp.int32(0)
    for e in range(N_EXPERTS):
        nb = (cnt_ref[e, 0] + (EXPERT_BLOCK - 1)) // EXPERT_BLOCK
        offset = jnp.where(eid == e, blocks_before * EXPERT_BLOCK, offset)

        def fill(j, c, e=e, base=blocks_before):
            info_ref[base + j] = e
            return c

        lax.fori_loop(0, nb, fill, 0)
        blocks_before = blocks_before + nb

    def fill_tail(j, c):
        info_ref[j] = N_EXPERTS - 1
        return c

    lax.fori_loop(blocks_before, n_blocks, fill_tail, 0)
    info_ref[n_blocks] = blocks_before
    dest_ref[...] = jnp.zeros(dest_ref.shape, I32)
    dest_ref[0:2, :] = ri[2:4] + offset


def _plan(cnt, ri, n_blocks):
    n_tok = ri.shape[1]
    return pl.pallas_call(
        functools.partial(_plan_kernel, n_blocks=n_blocks),
        in_specs=[pl.BlockSpec(memory_space=pltpu.SMEM), pl.BlockSpec(memory_space=pltpu.VMEM)],
        out_specs=[pl.BlockSpec(memory_space=pltpu.VMEM), pl.BlockSpec(memory_space=pltpu.SMEM)],
        out_shape=[jax.ShapeDtypeStruct((SUBLANES, n_tok), I32),
                   jax.ShapeDtypeStruct((n_blocks + 1,), I32)],
        compiler_params=pltpu.CompilerParams(vmem_limit_bytes=VMEM_LIMIT),
        name="plan",
    )(cnt, ri)


def _store_row_tiles(ref, r0, x):
    for s in range(ROW_TILE):
        ref[pl.ds(r0 * ROW_TILE + s, x.shape[0], stride=ROW_TILE), :] = x[:, s * LANES:(s + 1) * LANES]


def _load_row_tiles(ref, r0, n):
    cols = [ref[pl.ds(r0 * ROW_TILE + s, n, stride=ROW_TILE), :] for s in range(ROW_TILE)]
    return jnp.concatenate(cols, axis=1)


def _row_copy(src_ref, s, dst_ref, d, sem):
    src = src_ref.at[pl.ds(pl.multiple_of(s * ROW_TILE, ROW_TILE), ROW_TILE)]
    dst = dst_ref.at[pl.ds(pl.multiple_of(d * ROW_TILE, ROW_TILE), ROW_TILE)]
    return pltpu.make_async_copy(src, dst, sem)


def _dispatch_kernel(d0_ref, d1_ref, x_ref, xb_in_ref, xb_ref, sem):
    del xb_in_ref
    T = x_ref.shape[0] // ROW_TILE
    base = pl.program_id(0) * T

    def start(t, c):
        _row_copy(x_ref, t, xb_ref, d0_ref[base + t], sem.at[0]).start(priority=0)
        _row_copy(x_ref, t, xb_ref, d1_ref[base + t], sem.at[1]).start(priority=1)
        return c

    lax.fori_loop(0, T, start, 0, unroll=8)

    for k in range(2):
        pltpu.make_async_copy(x_ref, xb_ref.at[pl.ds(0, T * ROW_TILE)], sem.at[k]).wait()


def _dispatch(xn, d0, d1, n_rows):
    n_tok = xn.shape[0] // ROW_TILE
    T = T_TOK
    xb0 = jnp.zeros((n_rows * ROW_TILE, LANES), F32)
    return pl.pallas_call(
        _dispatch_kernel,
        grid_spec=pltpu.PrefetchScalarGridSpec(
            num_scalar_prefetch=2,
            grid=(n_tok // T,),
            in_specs=[pl.BlockSpec((T * ROW_TILE, LANES), lambda i, d0, d1: (i, 0)),
                      pl.BlockSpec(memory_space=pl.ANY)],
            out_specs=pl.BlockSpec(memory_space=pl.ANY),
            scratch_shapes=[pltpu.SemaphoreType.DMA((2,))],
        ),
        out_shape=jax.ShapeDtypeStruct((n_rows * ROW_TILE, LANES), F32),
        input_output_aliases={3: 0},
        compiler_params=_cparams(("arbitrary",)),
        name="dispatch",
    )(d0, d1, xn, xb0)


def _expert_kernel(info_ref, xb_ref, *refs, n_blocks):
    yb_ref = refs[-1]
    i = pl.program_id(0)
    blk0 = i * BLOCKS_PER_STEP
    n_used = info_ref[n_blocks]

    def swiglu(r0, n, wg_ref, wu_ref, wd_ref):
        x = _load_row_tiles(xb_ref, r0, n).astype(BF16)
        g = _dot(x, wg_ref[0])
        up = _dot(x, wu_ref[0])
        hmid = (g * jax.nn.sigmoid(g) * up).astype(BF16)
        _store_row_tiles(yb_ref, r0, _dot(hmid, wd_ref[0]))

    one_expert = jnp.logical_and(info_ref[blk0] == info_ref[blk0 + BLOCKS_PER_STEP - 1],
                                 blk0 + BLOCKS_PER_STEP - 1 < n_used)

    @pl.when(one_expert)
    def _():
        swiglu(0, BLOCKS_PER_STEP * EXPERT_BLOCK, *refs[0:3])

    for u in range(BLOCKS_PER_STEP):
        blk = blk0 + u

        @pl.when(jnp.logical_and(jnp.logical_not(one_expert), blk < n_used))
        def _():
            swiglu(u * EXPERT_BLOCK, EXPERT_BLOCK, *refs[3 * u:3 * u + 3])

        @pl.when(blk >= n_used)
        def _():
            n = EXPERT_BLOCK * ROW_TILE
            yb_ref[pl.ds(u * n, n), :] = jnp.zeros((n, LANES), F32)


def _experts(info, xb, lw):
    n_rows = xb.shape[0] // ROW_TILE
    n_blocks = n_rows // EXPERT_BLOCK
    assert n_blocks % BLOCKS_PER_STEP == 0
    step_rows = BLOCKS_PER_STEP * EXPERT_BLOCK * ROW_TILE
    w_specs, w_args = [], []
    for u in range(BLOCKS_PER_STEP):
        pick = lambda i, info, u=u: (info[i * BLOCKS_PER_STEP + u], 0, 0)
        w_specs += [pl.BlockSpec((1, D_MODEL, D_EXPERT), pick), pl.BlockSpec((1, D_MODEL, D_EXPERT), pick),
                    pl.BlockSpec((1, D_EXPERT, D_MODEL), pick)]
        w_args += [lw["w_gate"], lw["w_up"], lw["w_down"]]
    return pl.pallas_call(
        functools.partial(_expert_kernel, n_blocks=n_blocks),
        grid_spec=pltpu.PrefetchScalarGridSpec(
            num_scalar_prefetch=1,
            grid=(n_blocks // BLOCKS_PER_STEP,),
            in_specs=[pl.BlockSpec((step_rows, LANES), lambda i, info: (i, 0))] + w_specs,
            out_specs=pl.BlockSpec((step_rows, LANES), lambda i, info: (i, 0)),
        ),
        out_shape=jax.ShapeDtypeStruct((n_rows * ROW_TILE, LANES), F32),
        compiler_params=_cparams(("arbitrary",)),
        name="experts",
    )(info, xb, *w_args)


def _combine_kernel(d0_ref, d1_ref, h_ref, rf_ref, p_ref, yb_ref, gn_ref, wg_ref, wp_ref, o_ref,
                    y0_ref, y1_ref, sem):
    T = h_ref.shape[0]
    base = pl.program_id(0) * T

    def start(t, c):
        _row_copy(yb_ref, d0_ref[base + t], y0_ref, t, sem.at[0]).start(priority=0)
        _row_copy(yb_ref, d1_ref[base + t], y1_ref, t, sem.at[1]).start(priority=1)
        return c

    lax.fori_loop(0, T, start, 0, unroll=8)

    proj = _dot(p_ref[...].astype(BF16), wp_ref[...])
    gates = jnp.concatenate([rf_ref[...], jnp.zeros((LANES - SUBLANES, T), F32)], axis=0).T

    pltpu.make_async_copy(yb_ref.at[pl.ds(0, T * ROW_TILE)], y0_ref, sem.at[0]).wait()
    pltpu.make_async_copy(yb_ref.at[pl.ds(0, T * ROW_TILE)], y1_ref, sem.at[1]).wait()

    y0 = _load_row_tiles(y0_ref, 0, T)
    y1 = _load_row_tiles(y1_ref, 0, T)
    h2 = h_ref[...] + gates[:, 0:1] * y0 + gates[:, 1:2] * y1
    gate = jax.nn.sigmoid(_dot(_rms(h2, gn_ref[...]).astype(BF16), wg_ref[...]))
    o_ref[...] = h2 + gate * proj


def _combine(h, rf, p, yb, d0, d1, lw):
    n_tok = h.shape[0]
    T = T_TOK
    const = lambda shape: pl.BlockSpec(shape, lambda i, d0, d1: (0,) * len(shape))
    return pl.pallas_call(
        _combine_kernel,
        grid_spec=pltpu.PrefetchScalarGridSpec(
            num_scalar_prefetch=2,
            grid=(n_tok // T,),
            in_specs=[
                pl.BlockSpec((T, D_MODEL), lambda i, d0, d1: (i, 0)),
                pl.BlockSpec((SUBLANES, T), lambda i, d0, d1: (0, i)),
                pl.BlockSpec((T, PLE_DIM), lambda i, d0, d1: (i, 0)),
                pl.BlockSpec(memory_space=pl.ANY),
                const((1, D_MODEL)), const((D_MODEL, D_MODEL)), const((PLE_DIM, D_MODEL)),
            ],
            out_specs=pl.BlockSpec((T, D_MODEL), lambda i, d0, d1: (i, 0)),
            scratch_shapes=[pltpu.VMEM((T * ROW_TILE, LANES), F32), pltpu.VMEM((T * ROW_TILE, LANES), F32),
                            pltpu.SemaphoreType.DMA((2,))],
        ),
        out_shape=jax.ShapeDtypeStruct((n_tok, D_MODEL), F32),
        compiler_params=_cparams(("arbitrary",)),
        name="combine_ple",
    )(d0, d1, h, rf, p, yb, lw["ple_norm"], lw["ple_w_gate"], lw["ple_w_proj"])


def _pad_lanes(x, lo, width=LANES):
    n = x.shape[-1]
    pad = [(0, 0)] * (x.ndim - 1) + [(lo, width - lo - n)]
    return jnp.pad(x, pad)


def _rotate_half_cols(x):
    half = QK_ROPE // 2
    return jnp.concatenate([-x[..., half:], x[..., :half]], axis=-1)


def _swap_halves(g):
    half = QK_ROPE // 2
    return jnp.concatenate([g[..., half:], g[..., :half]], axis=-1)


def _prepare(attn_norm, attn_w_in, attn_q_a_norm, attn_kv_a_norm, attn_w_qb, attn_w_kvb, attn_q_gain,
             attn_k_gain, attn_w_o, conv_norm, conv_w_in, conv_w, conv_w_out, moe_norm, router_group_w,
             router_group_b, router_expert_w, router_expert_b, expert_w_gate, expert_w_up, expert_w_down,
             ple_norm, ple_w_gate, ple_w_proj):
    w = {}
    w_in = attn_w_in[0]
    rope_cols = w_in[:, Q_LORA + KV_LORA:]
    w["w_in_ext"] = jnp.concatenate(
        [w_in[:, :Q_LORA + KV_LORA], _pad_lanes(rope_cols, QK_NOPE), _pad_lanes(_rotate_half_cols(rope_cols), QK_NOPE)],
        axis=1).astype(BF16)
    w["attn_norm"] = attn_norm[0][None]
    w["g_qa"] = attn_q_a_norm[0][None]
    w["g_kva"] = attn_kv_a_norm[0][None]
    wqb = attn_w_qb[0].reshape(Q_LORA, N_HEADS, QK_DIM)
    w["wq"] = _pad_lanes(wqb, 0).reshape(Q_LORA, HEAD_PAD).astype(BF16)
    w["wq_sw"] = _pad_lanes(_rotate_half_cols(wqb[..., QK_NOPE:]), QK_NOPE).reshape(Q_LORA, HEAD_PAD).astype(BF16)
    wkvb = attn_w_kvb[0].reshape(KV_LORA, N_HEADS, QK_NOPE + V_DIM)
    w["wk"] = _pad_lanes(wkvb[..., :QK_NOPE], 0).reshape(KV_LORA, HEAD_PAD).astype(BF16)
    w["wvt"] = _pad_lanes(wkvb[..., QK_NOPE:], 0, V_EXT).reshape(KV_LORA, N_HEADS * V_EXT).T.astype(BF16)
    scale = LOG2_E / float(QK_DIM) ** 0.5
    gqs = attn_q_gain[0] * scale
    gk = attn_k_gain[0]
    zero = jnp.zeros((SUBLANES - 2, LANES), F32)
    key_norm_bound = jnp.full((1, LANES), BOUND_MARGIN * float(QK_DIM) ** 0.5, F32) * jnp.max(jnp.abs(gk))
    w["gq"] = jnp.concatenate([_pad_lanes(gqs, 0)[None], _pad_lanes(_swap_halves(gqs[QK_NOPE:]), QK_NOPE)[None],
                               key_norm_bound, zero[1:]])
    w["gk"] = jnp.concatenate([_pad_lanes(gk[:QK_NOPE], 0)[None], _pad_lanes(gk[QK_NOPE:], QK_NOPE)[None],
                               _pad_lanes(_swap_halves(gk[QK_NOPE:]), QK_NOPE)[None], zero[1:]])
    w["w_o"] = attn_w_o[0].astype(BF16)
    w["conv_norm"] = conv_norm[0][None]
    w["conv_w_in"] = conv_w_in[0].astype(BF16)
    w["conv_w"] = jnp.pad(conv_w[0], ((0, SUBLANES - conv_w.shape[1]), (0, 0)))
    w["conv_w_out"] = conv_w_out[0].astype(BF16)
    tri = lax.broadcasted_iota(I32, (T_TOK, T_TOK), 0) < lax.broadcasted_iota(I32, (T_TOK, T_TOK), 1)
    w["tri"] = tri.astype(BF16)
    layers = []
    for i in range(moe_norm.shape[0]):
        rw = jnp.zeros((D_MODEL, LANES), F32)
        rw = rw.at[:, GROUP_COL0:GROUP_COL0 + N_GROUPS].set(router_group_w[i])
        rw = rw.at[:, EXPERT_COL0:EXPERT_COL0 + N_EXPERTS].set(router_expert_w[i])
        rw_hi = rw.astype(BF16)
        rw_lo = (rw - rw_hi.astype(F32)).astype(BF16)
        rb = jnp.zeros((1, LANES), F32)
        rb = rb.at[0, GROUP_COL0:GROUP_COL0 + N_GROUPS].set(router_group_b[i])
        rb = rb.at[0, EXPERT_COL0:EXPERT_COL0 + N_EXPERTS].set(router_expert_b[i])
        layers.append(dict(
            moe_norm=moe_norm[i][None], rw=jnp.concatenate([rw_hi, rw_lo], axis=1), rb=rb,
            w_gate=expert_w_gate[i].astype(BF16), w_up=expert_w_up[i].astype(BF16),
            w_down=expert_w_down[i].astype(BF16),
            ple_norm=ple_norm[i][None], ple_w_gate=ple_w_gate[i].astype(BF16),
            ple_w_proj=ple_w_proj[i].astype(BF16)))
    return w, layers


def _rope_tiles(S):
    pos = jnp.arange(S, dtype=F32)
    inv = ROPE_THETA ** (-jnp.arange(0, QK_ROPE, 2, dtype=F32) / QK_ROPE)
    ang = pos[:, None] * inv[None, :]
    cos, sin = jnp.cos(ang), jnp.sin(ang)
    cos_t = jnp.concatenate([jnp.ones((S, QK_NOPE), F32), cos, cos, jnp.zeros((S, LANES - QK_DIM), F32)], axis=1)
    sin_t = jnp.concatenate([jnp.zeros((S, QK_NOPE), F32), sin, sin, jnp.zeros((S, LANES - QK_DIM), F32)], axis=1)
    return cos_t, sin_t


def _moe_ple(h1, xn, ri, rf, cnt, p, lw):
    n_tok = h1.shape[0]
    n_assign = 2 * n_tok
    n_rows = -(-n_assign // EXPERT_BLOCK) * EXPERT_BLOCK + N_EXPERTS * EXPERT_BLOCK
    dest, info = _plan(cnt, ri, n_rows // EXPERT_BLOCK)
    d0, d1 = dest[0], dest[1]
    xb = _dispatch(xn, d0, d1, n_rows)
    yb = _experts(info, xb, lw)
    return _combine(h1, rf, p, yb, d0, d1, lw)


def _trunk(x, p, w, layers):
    B, S, _ = x.shape
    n_tok = B * S
    cos_t, sin_t = _rope_tiles(S)
    h = x.reshape(n_tok, D_MODEL)
    p = p.reshape(p.shape[0], n_tok, PLE_DIM)
    q, k, vt = _attn_in(x, w, cos_t, sin_t)
    ot = _flash(q, k, vt)
    h1, xn, ri, rf, cnt = _attn_out(h, ot, w, layers[0])
    h = _moe_ple(h1, xn, ri, rf, cnt, p[0], layers[0])
    h1, xn, ri, rf, cnt = _conv(h, S, w, layers[1])
    h = _moe_ple(h1, xn, ri, rf, cnt, p[1], layers[1])
    return h.reshape(B, S, D_MODEL)


def kernel(x_prompt, x_sample, p_prompt, p_sample, attn_norm, attn_w_in, attn_q_a_norm, attn_kv_a_norm, attn_w_qb, attn_w_kvb, attn_q_gain, attn_k_gain, attn_w_o, conv_norm, conv_w_in, conv_w, conv_w_out, moe_norm, router_group_w, router_group_b, router_expert_w, router_expert_b, expert_w_gate, expert_w_up, expert_w_down, ple_norm, ple_w_gate, ple_w_proj):
    assert x_prompt.shape[-1] == D_MODEL and moe_norm.shape[0] == 2
    assert attn_w_in.shape == (1, D_MODEL, Q_LORA + KV_LORA + QK_ROPE)
    assert attn_w_qb.shape == (1, Q_LORA, N_HEADS * QK_DIM)
    assert attn_w_kvb.shape == (1, KV_LORA, N_HEADS * (QK_NOPE + V_DIM))
    assert expert_w_gate.shape[1:] == (N_EXPERTS, D_MODEL, D_EXPERT)
    w, layers = _prepare(attn_norm, attn_w_in, attn_q_a_norm, attn_kv_a_norm, attn_w_qb, attn_w_kvb,
                         attn_q_gain, attn_k_gain, attn_w_o, conv_norm, conv_w_in, conv_w, conv_w_out,
                         moe_norm, router_group_w, router_group_b, router_expert_w, router_expert_b,
                         expert_w_gate, expert_w_up, expert_w_down, ple_norm, ple_w_gate, ple_w_proj)
    return (_trunk(x_prompt, p_prompt, w, layers), _trunk(x_sample, p_sample, w, layers))
```
